```python
import math
import jax
import jax.numpy as jnp
from jax import lax
import numpy as np


D_MODEL = 4096
BATCH = 8
SEQ = 2048
DEPTH = 2

GRID_W = 64
CTX_LEN = 256
NORM_EPS = 1e-6
ROPE_THETA = 10000.0
Q_BLOCK = 128

N_EVEN = (DEPTH + 1) // 2
N_ODD = DEPTH // 2

MLA_HEADS = 16
MLA_Q_RANK = 1024
MLA_KV_RANK = 512
MLA_NOPE_DIM = 128
MLA_ROPE_DIM = 64
MLA_V_DIM = 128
MLA_QK_DIM = MLA_NOPE_DIM + MLA_ROPE_DIM
MLA_OUT = MLA_HEADS * MLA_V_DIM
SC_WIDTH = D_MODEL - MLA_OUT
EVEN_IN = MLA_Q_RANK + MLA_KV_RANK + MLA_ROPE_DIM + 3 * SC_WIDTH

DIFF_HEADS = 16
DIFF_HEAD_DIM = 64
DIFF_QK = DIFF_HEADS * 2 * DIFF_HEAD_DIM
DIFF_V = DIFF_HEADS * 2 * DIFF_HEAD_DIM
HY_WIDTH = D_MODEL - DIFF_V
HY_ORDER = 2
HY_BANDS = 16
HY_EMB = 1 + 2 * HY_BANDS
HY_FFN = 64
HY_SHIFT = 0.05
HY_MIN_DECAY = math.log(1e-2) / 1.5
HY_MAX_DECAY = math.log(1e-2) / 0.3
ODD_IN = 2 * DIFF_QK + DIFF_V + 3 * HY_WIDTH

N_EXPERTS = 32
TOP_K = 4
D_EXPERT = 640
SWIGLU_LIMIT = 7.0
SWIGLU_ALPHA = 1.702
MOE_BLOCK = 128

kernel_name = 'hybrid_mla_hyena_diffattn_moe_dit'


def rms_norm(x, gain=None):
    xf = x.astype(jnp.float32)
    y = xf * lax.rsqrt(jnp.mean(xf * xf, -1, keepdims=True) + NORM_EPS)
    if gain is not None:
        y = y * gain.astype(jnp.float32)
    return y.astype(x.dtype)


def adaln(cond, w, b):
    m = jax.nn.silu(cond) @ w + b
    return jnp.split(m[:, None, :], 6, -1)


def modulate(x, shift, scale):
    return rms_norm(x) * (1.0 + scale) + shift


def axial_rope(n_tok, rot_dim):
    rows = n_tok // GRID_W
    row = jnp.repeat(jnp.arange(rows), GRID_W).astype(jnp.float32)
    col = jnp.tile(jnp.arange(GRID_W), rows).astype(jnp.float32)
    quarter = rot_dim // 4
    inv = 1.0 / (ROPE_THETA ** (jnp.arange(quarter, dtype=jnp.float32) / quarter))
    ang = jnp.concatenate([row[:, None] * inv, col[:, None] * inv], -1)
    return jnp.cos(ang), jnp.sin(ang)


def apply_rope(x, cos, sin):
    shape = (x.shape[1],) + (1,) * (x.ndim - 3) + (cos.shape[-1],)
    cos = cos.reshape(shape).astype(x.dtype)
    sin = sin.reshape(shape).astype(x.dtype)
    x1, x2 = jnp.split(x, 2, axis=-1)
    return jnp.concatenate([x1 * cos - x2 * sin, x1 * sin + x2 * cos], -1)


def conv3(u, w):
    up = jnp.pad(u, ((0, 0), (1, 1), (0, 0)))
    return up[:, :-2] * w[0] + up[:, 1:-1] * w[1] + up[:, 2:] * w[2]


def sweep_query_blocks(block_fn, q):
    b, n = q.shape[:2]
    qb = jnp.moveaxis(q.reshape((b, n // Q_BLOCK, Q_BLOCK) + q.shape[2:]), 1, 0)
    out = jnp.moveaxis(lax.map(block_fn, qb), 0, 1)
    return out.reshape((b, n) + out.shape[3:])


def mla_q(cq, q_norm, w_uq, rope):
    b, n = cq.shape[:2]
    q = (rms_norm(cq, q_norm) @ w_uq).reshape(b, n, MLA_HEADS, MLA_QK_DIM)
    if rope is None:
        return q
    q_nope, q_pe = jnp.split(q, [MLA_NOPE_DIM], -1)
    return jnp.concatenate([q_nope, apply_rope(q_pe, *rope)], -1)


def mla_kv(ckv_pe, kv_norm, w_ukv, rope):
    b, n = ckv_pe.shape[:2]
    ckv, k_pe = jnp.split(ckv_pe, [MLA_KV_RANK], -1)
    kv = (rms_norm(ckv, kv_norm) @ w_ukv).reshape(b, n, MLA_HEADS, MLA_NOPE_DIM + MLA_V_DIM)
    k_nope, v = jnp.split(kv, [MLA_NOPE_DIM], -1)
    if rope is not None:
        k_pe = apply_rope(k_pe, *rope)
    k_pe = jnp.broadcast_to(k_pe[:, :, None, :], (b, n, MLA_HEADS, MLA_ROPE_DIM))
    return jnp.concatenate([k_nope, k_pe], -1), v


def mla_attend(q, k, v):
    scale = MLA_QK_DIM ** -0.5

    def blk(qb):
        s = jnp.einsum('bqhd,bkhd->bhqk', qb, k).astype(jnp.float32) * scale
        p = jax.nn.softmax(s, -1).astype(v.dtype)
        return jnp.einsum('bhqk,bkhd->bqhd', p, v)

    return sweep_query_blocks(blk, q)


def even_out(attn, sc_in, conv_w, out_w):
    b, n = attn.shape[:2]
    gb, gc, hh = jnp.split(sc_in, 3, -1)
    sc = gb * conv3(gc * hh, conv_w)
    return jnp.concatenate([attn.reshape(b, n, MLA_OUT), sc], -1) @ out_w


def even_mixer(u, uc, in_w, q_norm, kv_norm, w_uq, w_ukv, conv_w, out_w, rope, need_ctx):
    kv0, kv1 = MLA_Q_RANK, MLA_Q_RANK + MLA_KV_RANK + MLA_ROPE_DIM
    hl = u @ in_w
    hc = uc @ (in_w if need_ctx else in_w[:, kv0:kv1])
    hc_kv = hc[..., kv0:kv1] if need_ctx else hc
    k_ctx, v_ctx = mla_kv(hc_kv, kv_norm, w_ukv, None)
    k_lat, v_lat = mla_kv(hl[..., kv0:kv1], kv_norm, w_ukv, rope)
    q_lat = mla_q(hl[..., :kv0], q_norm, w_uq, rope)
    attn = mla_attend(q_lat, jnp.concatenate([k_lat, k_ctx], 1), jnp.concatenate([v_lat, v_ctx], 1))
    y = even_out(attn, hl[..., kv1:], conv_w, out_w)
    yc = None
    if need_ctx:
        q_ctx = mla_q(hc[..., :kv0], q_norm, w_uq, None)
        yc = even_out(mla_attend(q_ctx, k_ctx, v_ctx), hc[..., kv1:], conv_w, out_w)
    return y, yc


def diff_q(hq, rope):
    b, n = hq.shape[:2]
    q = hq.reshape(b, n, DIFF_HEADS, 2, DIFF_HEAD_DIM)
    return q if rope is None else apply_rope(q, *rope)


def diff_kv(hkv, rope):
    b, n = hkv.shape[:2]
    k, v = jnp.split(hkv, [DIFF_QK], -1)
    k = k.reshape(b, n, DIFF_HEADS, 2, DIFF_HEAD_DIM)
    if rope is not None:
        k = apply_rope(k, *rope)
    return k, v.reshape(b, n, DIFF_HEADS, 2 * DIFF_HEAD_DIM)


def diff_attend(q, k, v, lam):
    scale = DIFF_HEAD_DIM ** -0.5

    def blk(qb):
        s = jnp.einsum('bqhcd,bkhcd->bhcqk', qb, k).astype(jnp.float32) * scale
        p = jax.nn.softmax(s, -1)
        pd = (p[:, :, 0] - lam * p[:, :, 1]).astype(v.dtype)
        return jnp.einsum('bhqk,bkhe->bqhe', pd, v)

    return sweep_query_blocks(blk, q)


def hyena_filters(n, w1, b1, w2, b2, w3):
    f32 = jnp.float32
    t = jnp.linspace(0.0, 1.0, n, dtype=f32)[:, None]
    ang = (2.0 * math.pi / n) * jnp.arange(n, dtype=f32)[:, None] * jnp.linspace(1e-4, HY_BANDS - 1, HY_BANDS, dtype=f32)[None, :]
    z = jnp.concatenate([t, jnp.cos(ang), -jnp.sin(ang)], -1)
    hid = jnp.sin(z @ w1.astype(f32) + b1.astype(f32))
    hid = jnp.sin(hid @ w2.astype(f32) + b2.astype(f32))
    k = (hid @ w3.astype(f32)).reshape(n, HY_ORDER, 2, HY_WIDTH)
    decay = jnp.abs(jnp.linspace(HY_MIN_DECAY, HY_MAX_DECAY, HY_WIDTH, dtype=f32))
    k = k * (jnp.exp(-t[:, :, None, None] * decay) + HY_SHIFT)
    fwd, bwd = k[:, :, 0], k[:, :, 1]
    full = jnp.concatenate([fwd[:1] + bwd[:1], fwd[1:], jnp.zeros_like(fwd[:1]), bwd[:0:-1]], 0)
    full = full / jnp.sum(jnp.abs(full), 0, keepdims=True)
    return jnp.moveaxis(full, 1, 0)


def fft_conv(u, filt):
    n = u.shape[1]
    uf = jnp.fft.rfft(u.astype(jnp.float32), n=2 * n, axis=1)
    hf = jnp.fft.rfft(filt, axis=0)
    return jnp.fft.irfft(uf * hf[None], n=2 * n, axis=1)[:, :n]


def hyena(proj, conv_w, filt, skip):
    dt = proj.dtype
    v, x1, x2 = jnp.split(conv3(proj, conv_w), 3, -1)
    z = v
    for o, gate in enumerate((x1, x2)):
        z = gate * (fft_conv(z, filt[o]).astype(dt) + z * skip[o])
    return z


def odd_out(attn, hy_in, subln, lam_init, conv_w, filt, skip, out_w):
    b, n = attn.shape[:2]
    a = (rms_norm(attn, subln) * (1.0 - lam_init)).reshape(b, n, DIFF_V)
    return jnp.concatenate([a, hyena(hy_in, conv_w, filt, skip)], -1) @ out_w


def odd_mixer(u, uc, in_w, lam_p, subln, conv_w, filt_p, skip, out_w, rope, lam_init, need_ctx):
    kv0, kv1 = DIFF_QK, 2 * DIFF_QK + DIFF_V
    hl = u @ in_w
    hc = uc @ (in_w if need_ctx else in_w[:, kv0:kv1])
    hc_kv = hc[..., kv0:kv1] if need_ctx else hc
    lq1, lk1, lq2, lk2 = lam_p.astype(jnp.float32)
    lam = jnp.exp(jnp.sum(lq1 * lk1)) - jnp.exp(jnp.sum(lq2 * lk2)) + lam_init
    k_ctx, v_ctx = diff_kv(hc_kv, None)
    k_lat, v_lat = diff_kv(hl[..., kv0:kv1], rope)
    q_lat = diff_q(hl[..., :kv0], rope)
    attn = diff_attend(q_lat, jnp.concatenate([k_lat, k_ctx], 1), jnp.concatenate([v_lat, v_ctx], 1), lam)
    filt_lat = hyena_filters(u.shape[1], *filt_p)
    y = odd_out(attn, hl[..., kv1:], subln, lam_init, conv_w, filt_lat, skip, out_w)
    yc = None
    if need_ctx:
        attn_c = diff_attend(diff_q(hc[..., :kv0], None), k_ctx, v_ctx, lam)
        filt_ctx = hyena_filters(uc.shape[1], *filt_p)
        yc = odd_out(attn_c, hc[..., kv1:], subln, lam_init, conv_w, filt_ctx, skip, out_w)
    return y, yc


def moe(h, router_w, router_b, w_gu, b_gu, w_down, b_down):
    t_tok, d = h.shape
    tk = t_tok * TOP_K
    logits = (h @ router_w).astype(jnp.float32) + router_b.astype(jnp.float32)
    top_v, top_i = lax.top_k(logits, TOP_K)
    gates = jax.nn.softmax(top_v, -1)
    flat_e = top_i.reshape(-1)
    order = jnp.argsort(flat_e)
    sorted_e = flat_e[order]
    sorted_tok = (order // TOP_K).astype(jnp.int32)
    sorted_gate = gates.reshape(-1)[order]
    counts = jnp.bincount(flat_e, length=N_EXPERTS)
    padded = (counts + MOE_BLOCK - 1) // MOE_BLOCK * MOE_BLOCK
    pad_end = jnp.cumsum(padded)
    pad_start = pad_end - padded
    start = jnp.cumsum(counts) - counts
    dest = pad_start[sorted_e] + jnp.arange(tk) - start[sorted_e]
    n_blocks = -(-tk // MOE_BLOCK) + N_EXPERTS
    n_slots = n_blocks * MOE_BLOCK
    slot_tok = jnp.full((n_slots,), t_tok, jnp.int32).at[dest].set(sorted_tok)
    slot_gate = jnp.zeros((n_slots,), jnp.float32).at[dest].set(sorted_gate)
    block_e = jnp.minimum(jnp.searchsorted(pad_end, jnp.arange(n_blocks) * MOE_BLOCK, side='right'), N_EXPERTS - 1)
    h_pad = jnp.concatenate([h, jnp.zeros((1, d), h.dtype)], 0)

    def run(args):
        tok, e = args
        gu = h_pad[tok] @ w_gu[e] + b_gu[e]
        g, up = jnp.split(gu, 2, -1)
        g = jnp.minimum(g, SWIGLU_LIMIT)
        up = jnp.clip(up, -SWIGLU_LIMIT, SWIGLU_LIMIT)
        a = (up + 1.0) * g * jax.nn.sigmoid(SWIGLU_ALPHA * g)
        return a @ w_down[e] + b_down[e]

    yb = lax.map(run, (slot_tok.reshape(n_blocks, MOE_BLOCK), block_e)).reshape(n_slots, d)
    y = jax.ops.segment_sum(yb * slot_gate[:, None].astype(yb.dtype), slot_tok, num_segments=t_tok + 1)
    return y[:t_tok]


def setup_inputs(seed: int = 0) -> dict:
    key = jax.random.key(seed)
    ks = iter(jax.random.split(key, 32))
    D = D_MODEL

    def nrm(shape, scale):
        return jax.random.normal(next(ks), shape, jnp.float32) * scale

    def gain(shape):
        return 1.0 + nrm(shape, 0.02)

    return {
        'x': nrm((BATCH, SEQ, D), 1.0),
        'c': nrm((BATCH, D), 1.0),
        'ctx': nrm((BATCH, CTX_LEN, D), 1.0),
        'c_ctx': nrm((D,), 1.0),
        'ada_w': nrm((DEPTH, D, 6 * D), 0.5 * D ** -0.5),
        'ada_b': nrm((DEPTH, 6 * D), 0.02),
        'mla_in_w': nrm((N_EVEN, D, EVEN_IN), D ** -0.5),
        'mla_q_norm': gain((N_EVEN, MLA_Q_RANK)),
        'mla_kv_norm': gain((N_EVEN, MLA_KV_RANK)),
        'mla_w_uq': nrm((N_EVEN, MLA_Q_RANK, MLA_HEADS * MLA_QK_DIM), MLA_Q_RANK ** -0.5),
        'mla_w_ukv': nrm((N_EVEN, MLA_KV_RANK, MLA_HEADS * (MLA_NOPE_DIM + MLA_V_DIM)), MLA_KV_RANK ** -0.5),
        'sc_conv_w': nrm((N_EVEN, 3, SC_WIDTH), 3 ** -0.5),
        'even_out_w': nrm((N_EVEN, D, D), D ** -0.5),
        'odd_in_w': nrm((N_ODD, D, ODD_IN), D ** -0.5),
        'diff_lambda': nrm((N_ODD, 4, DIFF_HEAD_DIM), 0.1),
        'diff_subln': gain((N_ODD, 2 * DIFF_HEAD_DIM)),
        'hy_conv_w': nrm((N_ODD, 3, 3 * HY_WIDTH), 3 ** -0.5),
        'hy_w1': nrm((N_ODD, HY_EMB, HY_FFN), 1.0),
        'hy_b1': nrm((N_ODD, HY_FFN), 0.02),
        'hy_w2': nrm((N_ODD, HY_FFN, HY_FFN), HY_FFN ** -0.5),
        'hy_b2': nrm((N_ODD, HY_FFN), 0.02),
        'hy_w3': nrm((N_ODD, HY_FFN, HY_ORDER * 2 * HY_WIDTH), HY_FFN ** -0.5),
        'hy_skip': nrm((N_ODD, HY_ORDER, HY_WIDTH), 1.0),
        'odd_out_w': nrm((N_ODD, D, D), D ** -0.5),
        'router_w': nrm((DEPTH, D, N_EXPERTS), D ** -0.5),
        'router_b': nrm((DEPTH, N_EXPERTS), 0.01),
        'moe_w_gu': nrm((DEPTH, N_EXPERTS, D, 2 * D_EXPERT), D ** -0.5),
        'moe_b_gu': nrm((DEPTH, N_EXPERTS, 2 * D_EXPERT), 0.02),
        'moe_w_down': nrm((DEPTH, N_EXPERTS, D_EXPERT, D), D_EXPERT ** -0.5),
        'moe_b_down': nrm((DEPTH, N_EXPERTS, D), 0.02),
        'final_norm': gain((D,)),
    }


def reference(x, c, ctx, c_ctx, ada_w, ada_b, mla_in_w, mla_q_norm, mla_kv_norm, mla_w_uq, mla_w_ukv,
              sc_conv_w, even_out_w, odd_in_w, diff_lambda, diff_subln, hy_conv_w, hy_w1, hy_b1, hy_w2,
              hy_b2, hy_w3, hy_skip, odd_out_w, router_w, router_b, moe_w_gu, moe_b_gu, moe_w_down,
              moe_b_down, final_norm):
    b, n, d = x.shape
    m = ctx.shape[1]
    rope_mla = axial_rope(n, MLA_ROPE_DIM)
    rope_diff = axial_rope(n, DIFF_HEAD_DIM)
    h, hc = x, ctx
    for l in range(DEPTH):
        need_ctx = l < DEPTH - 1
        i = l // 2
        sh1, sc1, g1, sh2, sc2, g2 = adaln(c, ada_w[l], ada_b[l])
        csh1, csc1, cg1, csh2, csc2, cg2 = adaln(c_ctx[None], ada_w[l], ada_b[l])
        u = modulate(h, sh1, sc1)
        uc = modulate(hc, csh1, csc1)
        if l % 2 == 0:
            y, yc = even_mixer(u, uc, mla_in_w[i], mla_q_norm[i], mla_kv_norm[i], mla_w_uq[i], mla_w_ukv[i],
                               sc_conv_w[i], even_out_w[i], rope_mla, need_ctx)
        else:
            filt_p = (hy_w1[i], hy_b1[i], hy_w2[i], hy_b2[i], hy_w3[i])
            lam_init = 0.8 - 0.6 * math.exp(-0.3 * l)
            y, yc = odd_mixer(u, uc, odd_in_w[i], diff_lambda[i], diff_subln[i], hy_conv_w[i], filt_p,
                              hy_skip[i], odd_out_w[i], rope_diff, lam_init, need_ctx)
        h = h + g1 * y
        moe_p = (router_w[l], router_b[l], moe_w_gu[l], moe_b_gu[l], moe_w_down[l], moe_b_down[l])
        u2 = modulate(h, sh2, sc2).reshape(b * n, d)
        if need_ctx:
            hc = hc + cg1 * yc
            u2c = modulate(hc, csh2, csc2).reshape(b * m, d)
            f = moe(jnp.concatenate([u2, u2c], 0), *moe_p)
            h = h + g2 * f[:b * n].reshape(b, n, d)
            hc = hc + cg2 * f[b * n:].reshape(b, m, d)
        else:
            h = h + g2 * moe(u2, *moe_p).reshape(b, n, d)
    return rms_norm(h, final_norm)
```

```python
import functools
import math

import jax
import jax.numpy as jnp
from jax import lax
from jax.experimental import pallas as pl
from jax.experimental.pallas import tpu as pltpu

F32 = jnp.float32
BF16 = jnp.bfloat16
U32 = jnp.uint32
I32 = jnp.int32

GRID_W = 64
NORM_EPS = 1e-6
ROPE_THETA = 10000.0

MLA_HEADS = 16
MLA_Q_RANK = 1024
MLA_KV_RANK = 512
MLA_NOPE_DIM = 128
MLA_ROPE_DIM = 64
MLA_V_DIM = 128
MLA_QK_DIM = MLA_NOPE_DIM + MLA_ROPE_DIM
MLA_OUT = MLA_HEADS * MLA_V_DIM

DIFF_HEADS = 16
DIFF_HEAD_DIM = 64
DIFF_QK = DIFF_HEADS * 2 * DIFF_HEAD_DIM
DIFF_V = DIFF_HEADS * 2 * DIFF_HEAD_DIM

HY_ORDER = 2
HY_BANDS = 16
HY_EMB = 1 + 2 * HY_BANDS
HY_FFN = 64
HY_SHIFT = 0.05
HY_MIN_DECAY = math.log(1e-2) / 1.5
HY_MAX_DECAY = math.log(1e-2) / 0.3

N_EXPERTS = 32
TOP_K = 4
SWIGLU_LIMIT = 7.0
SWIGLU_ALPHA = 1.702

LANES = 128
MOD_ROWS = 16
VMEM_LIMIT = 56 * 1024 * 1024
MOE_BLK = 256
COMBINE_BLK = 128


def _cparams(sem):
    return pltpu.CompilerParams(dimension_semantics=sem, vmem_limit_bytes=VMEM_LIMIT)


def _rms(x):
    return x * lax.rsqrt(jnp.mean(x * x, -1, keepdims=True) + NORM_EPS)


def _split_bf16(a):
    hi = a.astype(BF16)
    lo = (a - hi.astype(F32)).astype(BF16)
    return hi, lo


def _dot3(a, b):
    ah, al = _split_bf16(a)
    bh, bl = _split_bf16(b)
    d = functools.partial(jnp.dot, preferred_element_type=F32)
    return d(ah, bh) + d(al, bh) + d(ah, bl)


def _pack_halves(y):
    c = y.shape[1] // 2
    lo = lax.bitcast_convert_type(y[:, :c].astype(BF16).astype(F32), U32)
    hi = lax.bitcast_convert_type(y[:, c:].astype(BF16).astype(F32), U32)
    return (lo >> 16) | (hi & jnp.uint32(0xFFFF0000))


def _unpack_halves(w):
    lo = lax.bitcast_convert_type(w << 16, F32)
    hi = lax.bitcast_convert_type(w & jnp.uint32(0xFFFF0000), F32)
    return lo, hi


def _rope_pairs(x, cos_t, sin_t):
    lane = lax.broadcasted_iota(I32, x.shape, 1)
    swapped = jnp.where((lane & 63) < 32, pltpu.roll(x, 96, 1), pltpu.roll(x, 32, 1))
    return x * cos_t + swapped * sin_t


def _adaln_kernel(c_ref, w_ref, b_ref, o_ref):
    x = c_ref[...]
    s = (x * jax.nn.sigmoid(x)).astype(BF16)
    o_ref[...] = jnp.dot(s, w_ref[...].astype(BF16), preferred_element_type=F32) + b_ref[...]


def _adaln(cond, ada_w, ada_b):
    depth, d, d6 = ada_w.shape
    tn = 512
    out = pl.pallas_call(
        _adaln_kernel,
        grid=(depth, d6 // tn),
        in_specs=[
            pl.BlockSpec((MOD_ROWS, d), lambda l, j: (0, 0)),
            pl.BlockSpec((None, d, tn), lambda l, j: (l, 0, j)),
            pl.BlockSpec((None, 1, tn), lambda l, j: (l, 0, j)),
        ],
        out_specs=pl.BlockSpec((None, MOD_ROWS, tn), lambda l, j: (l, 0, j)),
        out_shape=jax.ShapeDtypeStruct((depth, MOD_ROWS, d6), F32),
        compiler_params=_cparams(("arbitrary", "arbitrary")),
        name="adaln",
    )(cond, ada_w, ada_b.reshape(depth, 1, d6))
    return out.reshape(depth, MOD_ROWS, 1, d6)


class _Rows:
    def __init__(self, batch, seq, ctx):
        self.batch, self.seq, self.ctx = batch, seq, ctx
        self.lat = batch * seq
        self.all = batch * (seq + ctx)

    def mod_row(self, i, tm):
        r = i * tm
        return jnp.where(r < self.lat, r // self.seq, self.batch)


def _mod_spec(rows, tm, tn, layer, chunk, d, grid_rank, row_axis, col_axis=None):
    per = d // tn

    def imap(*g):
        j = 0 if col_axis is None else g[col_axis]
        return (layer, rows.mod_row(g[row_axis], tm), 0, chunk * per + j)

    return pl.BlockSpec((None, None, 1, tn), imap)


def _modulate_kernel(x_ref, sh_ref, sc_ref, o_ref):
    u = _rms(x_ref[...]) * (1.0 + sc_ref[...]) + sh_ref[...]
    o_ref[...] = u.astype(o_ref.dtype)


def _modulate(h, mod4, rows, layer, n_rows):
    d = h.shape[1]
    tm = 256
    return pl.pallas_call(
        _modulate_kernel,
        grid=(n_rows // tm,),
        in_specs=[
            pl.BlockSpec((tm, d), lambda i: (i, 0)),
            _mod_spec(rows, tm, d, layer, 0, d, 1, 0),
            _mod_spec(rows, tm, d, layer, 1, d, 1, 0),
        ],
        out_specs=pl.BlockSpec((tm, d), lambda i: (i, 0)),
        out_shape=jax.ShapeDtypeStruct((n_rows, d), BF16),
        compiler_params=_cparams(("arbitrary",)),
        name="modulate",
    )(h, mod4, mod4)


def _mm_kernel(*refs, ks, resid):
    a_refs = refs[:len(ks)]
    w_ref = refs[len(ks)]
    o_ref = refs[-1]
    acc = None
    off = 0
    for a_ref, k in zip(a_refs, ks):
        part = jnp.dot(a_ref[...], w_ref[off:off + k, :], preferred_element_type=F32)
        acc = part if acc is None else acc + part
        off += k
    if resid:
        h_ref, g_ref = refs[len(ks) + 1], refs[len(ks) + 2]
        o_ref[...] = h_ref[...] + g_ref[...] * acc
    else:
        o_ref[...] = acc.astype(o_ref.dtype)


def _matmul(a_list, w, *, n_rows, n_cols, a_row0=0, w_col0=0, tm=512, tn=1024, out_dtype=BF16,
            resid=None, name="matmul"):
    tn = min(tn, n_cols)
    tm = min(tm, n_rows)
    ks = tuple(a.shape[1] for a in a_list)
    k_all = sum(ks)
    r0, c0 = a_row0 // tm, w_col0 // tn
    in_specs = [pl.BlockSpec((tm, k), lambda j, i: (r0 + i, 0)) for k in ks]
    in_specs.append(pl.BlockSpec((k_all, tn), lambda j, i: (0, c0 + j)))
    args = list(a_list) + [w]
    if resid is not None:
        h, mod4, rows, layer, chunk = resid
        in_specs.append(pl.BlockSpec((tm, tn), lambda j, i: (i, j)))
        in_specs.append(_mod_spec(rows, tm, tn, layer, chunk, h.shape[1], 2, 1, 0))
        args += [h, mod4]
    return pl.pallas_call(
        functools.partial(_mm_kernel, ks=ks, resid=resid is not None),
        grid=(n_cols // tn, n_rows // tm),
        in_specs=in_specs,
        out_specs=pl.BlockSpec((tm, tn), lambda j, i: (i, j)),
        out_shape=jax.ShapeDtypeStruct((n_rows, n_cols), out_dtype),
        compiler_params=_cparams(("arbitrary", "arbitrary")),
        name=name,
    )(*args)


def _norm_mm_kernel(a_ref, g_ref, w_ref, o_ref):
    y = _rms(a_ref[...].astype(F32)) * g_ref[...]
    o_ref[...] = jnp.dot(y.astype(BF16), w_ref[...], preferred_element_type=F32).astype(o_ref.dtype)


def _norm_matmul(a, a_col0, gain, w, n_rows, name):
    kw, n = w.shape
    tm = 512
    cb = a_col0 // kw
    return pl.pallas_call(
        _norm_mm_kernel,
        grid=(n_rows // tm,),
        in_specs=[
            pl.BlockSpec((tm, kw), lambda i: (i, cb)),
            pl.BlockSpec((1, kw), lambda i: (0, 0)),
            pl.BlockSpec((kw, n), lambda i: (0, 0)),
        ],
        out_specs=pl.BlockSpec((tm, n), lambda i: (i, 0)),
        out_shape=jax.ShapeDtypeStruct((n_rows, n), BF16),
        compiler_params=_cparams(("arbitrary",)),
        name=name,
    )(a, gain.reshape(1, kw).astype(F32), w)


def _mla_attn_kernel(qn_ref, qp_ref, kvl_ref, kvc_ref, kpl_ref, kpc_ref, cq_ref, sq_ref, ck_ref, sk_ref,
                     o_ref, kf_ref, vf_ref, *, seq, ctx, n_lat_blocks, scale):
    qi = pl.program_id(2)

    @pl.when(qi == 0)
    def _build_keys():
        kvl = kvl_ref[...]
        kvc = kvc_ref[...]
        kf_ref[0:seq, 0:LANES] = kvl[:, :LANES]
        kf_ref[seq:seq + ctx, 0:LANES] = kvc[:, :LANES]
        vf_ref[0:seq, :] = kvl[:, LANES:]
        vf_ref[seq:seq + ctx, :] = kvc[:, LANES:]
        kpe = _rope_pairs(kpl_ref[...].astype(F32), ck_ref[...], sk_ref[...])
        kf_ref[0:seq, LANES:2 * LANES] = kpe.astype(BF16)
        kf_ref[seq:seq + ctx, LANES:2 * LANES] = kpc_ref[...]

    def attend(q, k, v):
        s = lax.dot_general(q, k, (((1,), (1,)), ((), ())), preferred_element_type=F32) * scale
        m = jnp.max(s, -1, keepdims=True)
        e = jnp.exp(s - m)
        den = jnp.sum(e, -1, keepdims=True)
        o = jnp.dot(e.astype(BF16), v, preferred_element_type=F32)
        o_ref[...] = (o / den).astype(o_ref.dtype)

    @pl.when(qi < n_lat_blocks)
    def _latent_queries():
        qpe = _rope_pairs(qp_ref[...].astype(F32), cq_ref[...], sq_ref[...]).astype(BF16)
        q = jnp.concatenate([qn_ref[...], qpe], -1)
        attend(q, kf_ref[...], vf_ref[...])

    @pl.when(qi >= n_lat_blocks)
    def _context_queries():
        q = jnp.concatenate([qn_ref[...], qp_ref[...]], -1)
        attend(q, kf_ref[seq:seq + ctx, :], vf_ref[seq:seq + ctx, :])


def _mla_attention(qup, kvup, hl, cos_t, sin_t, rows, kpe_col):
    batch, seq, ctx = rows.batch, rows.seq, rows.ctx
    tq = 256
    nl, nc = seq // tq, ctx // tq
    lat_blocks = rows.lat // tq

    def qrow(b, qi):
        return jnp.where(qi < nl, b * nl + qi, lat_blocks + b * nc + (qi - nl))

    def qtab(b, h, qi):
        return (jnp.minimum(qi, nl - 1), 0)

    kb = kpe_col // LANES
    ctx0 = rows.lat // ctx
    return pl.pallas_call(
        functools.partial(_mla_attn_kernel, seq=seq, ctx=ctx, n_lat_blocks=nl, scale=MLA_QK_DIM ** -0.5),
        grid=(batch, MLA_HEADS, nl + nc),
        in_specs=[
            pl.BlockSpec((tq, LANES), lambda b, h, qi: (qrow(b, qi), h)),
            pl.BlockSpec((tq, LANES), lambda b, h, qi: (qrow(b, qi), MLA_HEADS + h)),
            pl.BlockSpec((seq, 2 * LANES), lambda b, h, qi: (b, h)),
            pl.BlockSpec((ctx, 2 * LANES), lambda b, h, qi: (ctx0 + b, h)),
            pl.BlockSpec((seq, LANES), lambda b, h, qi: (b, kb)),
            pl.BlockSpec((ctx, LANES), lambda b, h, qi: (ctx0 + b, kb)),
            pl.BlockSpec((tq, LANES), qtab),
            pl.BlockSpec((tq, LANES), qtab),
            pl.BlockSpec((seq, LANES), lambda b, h, qi: (0, 0)),
            pl.BlockSpec((seq, LANES), lambda b, h, qi: (0, 0)),
        ],
        out_specs=pl.BlockSpec((tq, LANES), lambda b, h, qi: (qrow(b, qi), h)),
        out_shape=jax.ShapeDtypeStruct((rows.all, MLA_OUT), BF16),
        scratch_shapes=[pltpu.VMEM((seq + ctx, 2 * LANES), BF16), pltpu.VMEM((seq + ctx, LANES), BF16)],
        compiler_params=_cparams(("arbitrary", "arbitrary", "arbitrary")),
        name="mla_attn",
    )(qup, qup, kvup, kvup, hl, hl, cos_t, sin_t, cos_t, sin_t)


def _conv3(p, w):
    n = p.shape[0]
    row = lax.broadcasted_iota(I32, p.shape, 0)
    prev = jnp.where(row == 0, 0.0, pltpu.roll(p, 1, 0))
    nxt = jnp.where(row == n - 1, 0.0, pltpu.roll(p, n - 1, 0))
    return prev * w[0:1] + p * w[1:2] + nxt * w[2:3]


HALO = 16


def _sconv_kernel(gb_ref, gc_ref, hh_ref, gcp_ref, hhp_ref, gcn_ref, hhn_ref, w_ref, o_ref, *, seq, ctx, lat):
    i = pl.program_id(0)
    tr = gc_ref.shape[0]
    r0 = i * tr
    length = jnp.where(r0 < lat, seq, ctx)
    pos = jnp.where(r0 < lat, r0, r0 - lat)
    starts = lax.rem(pos, length) == 0
    ends = lax.rem(pos + tr, length) == 0
    p = gc_ref[...].astype(F32) * hh_ref[...].astype(F32)
    before = gcp_ref[HALO - 1:HALO, :].astype(F32) * hhp_ref[HALO - 1:HALO, :].astype(F32)
    after = gcn_ref[0:1, :].astype(F32) * hhn_ref[0:1, :].astype(F32)
    before = jnp.where(starts, 0.0, before)
    after = jnp.where(ends, 0.0, after)
    row = lax.broadcasted_iota(I32, p.shape, 0)
    prev = jnp.where(row == 0, before, pltpu.roll(p, 1, 0))
    nxt = jnp.where(row == tr - 1, after, pltpu.roll(p, tr - 1, 0))
    w = w_ref[...]
    conv = prev * w[0:1] + p * w[1:2] + nxt * w[2:3]
    o_ref[...] = (gb_ref[...].astype(F32) * conv).astype(o_ref.dtype)


def _short_conv(hl, col0, width, conv_w, rows):
    tr, tc = 256, 1024
    assert rows.seq % tr == 0 and rows.ctx % tr == 0
    nct = width // tc
    c0 = col0 // tc
    per = tr // HALO
    last = rows.all // HALO - 1

    def main(k):
        return pl.BlockSpec((tr, tc), lambda i, j: (i, c0 + k * nct + j))

    def before(k):
        return pl.BlockSpec((HALO, tc), lambda i, j: (jnp.maximum(i * per - 1, 0), c0 + k * nct + j))

    def after(k):
        return pl.BlockSpec((HALO, tc), lambda i, j: (jnp.minimum((i + 1) * per, last), c0 + k * nct + j))

    return pl.pallas_call(
        functools.partial(_sconv_kernel, seq=rows.seq, ctx=rows.ctx, lat=rows.lat),
        grid=(rows.all // tr, nct),
        in_specs=[main(0), main(1), main(2), before(1), before(2), after(1), after(2),
                  pl.BlockSpec((3, tc), lambda i, j: (0, j))],
        out_specs=pl.BlockSpec((tr, tc), lambda i, j: (i, j)),
        out_shape=jax.ShapeDtypeStruct((rows.all, width), BF16),
        compiler_params=_cparams(("arbitrary", "arbitrary")),
        name="short_conv",
    )(hl, hl, hl, hl, hl, hl, hl, conv_w)


def _dwconv_kernel(x_ref, w_ref, o_ref):
    o_ref[...] = _conv3(x_ref[...].astype(F32), w_ref[...]).astype(o_ref.dtype)


def _hyena_conv3(hl, col0, width, conv_w, rows):
    tc = 256
    c0 = col0 // tc
    return pl.pallas_call(
        _dwconv_kernel,
        grid=(rows.batch, width // tc),
        in_specs=[
            pl.BlockSpec((rows.seq, tc), lambda s, j: (s, c0 + j)),
            pl.BlockSpec((3, tc), lambda s, j: (0, j)),
        ],
        out_specs=pl.BlockSpec((rows.seq, tc), lambda s, j: (s, j)),
        out_shape=jax.ShapeDtypeStruct((rows.lat, width), BF16),
        compiler_params=_cparams(("arbitrary", "arbitrary")),
        name="hyena_conv3",
    )(hl, conv_w)


def _diff_attn_kernel(q_ref, kl_ref, kc_ref, vl_ref, vc_ref, cq_ref, sq_ref, ck_ref, sk_ref, lam_ref, sub_ref,
                      o_ref, kf_ref, vf_ref, *, seq, ctx, scale, lam_init):
    qi = pl.program_id(2)

    @pl.when(qi == 0)
    def _build_keys():
        kf_ref[0:seq, :] = _rope_pairs(kl_ref[...].astype(F32), ck_ref[...], sk_ref[...]).astype(BF16)
        kf_ref[seq:seq + ctx, :] = kc_ref[...]
        vf_ref[0:seq, :] = vl_ref[...]
        vf_ref[seq:seq + ctx, :] = vc_ref[...]

    q = _rope_pairs(q_ref[...].astype(F32), cq_ref[...], sq_ref[...])
    tq = q.shape[0]
    lane = lax.broadcasted_iota(I32, q.shape, 1)
    first = lane < DIFF_HEAD_DIM
    qq = jnp.concatenate([jnp.where(first, q, 0.0), jnp.where(first, 0.0, q)], 0).astype(BF16)
    s = lax.dot_general(qq, kf_ref[...], (((1,), (1,)), ((), ())), preferred_element_type=F32) * scale
    e = jnp.exp(s - jnp.max(s, -1, keepdims=True))
    p = e / jnp.sum(e, -1, keepdims=True)
    lp = lam_ref[...]
    lam = (jnp.exp(jnp.sum(lp[0:1] * lp[1:2], -1, keepdims=True))
           - jnp.exp(jnp.sum(lp[2:3] * lp[3:4], -1, keepdims=True)) + lam_init)
    pd = (p[:tq] - lam * p[tq:]).astype(BF16)
    o = jnp.dot(pd, vf_ref[...], preferred_element_type=F32)
    o_ref[...] = (_rms(o) * sub_ref[...] * (1.0 - lam_init)).astype(o_ref.dtype)


def _diff_attention(hl, hckv, cos_t, sin_t, lam_p, subln, rows, lam_init):
    batch, seq, ctx = rows.batch, rows.seq, rows.ctx
    tq = 256
    nl = seq // tq
    kc0 = DIFF_QK // LANES
    vc0 = 2 * DIFF_QK // LANES
    lam_pad = jnp.pad(lam_p.astype(F32), ((0, 0), (0, LANES - lam_p.shape[1])))
    return pl.pallas_call(
        functools.partial(_diff_attn_kernel, seq=seq, ctx=ctx, scale=DIFF_HEAD_DIM ** -0.5, lam_init=lam_init),
        grid=(batch, DIFF_HEADS, nl),
        in_specs=[
            pl.BlockSpec((tq, LANES), lambda b, h, qi: (b * nl + qi, h)),
            pl.BlockSpec((seq, LANES), lambda b, h, qi: (b, kc0 + h)),
            pl.BlockSpec((ctx, LANES), lambda b, h, qi: (b, h)),
            pl.BlockSpec((seq, LANES), lambda b, h, qi: (b, vc0 + h)),
            pl.BlockSpec((ctx, LANES), lambda b, h, qi: (b, kc0 + h)),
            pl.BlockSpec((tq, LANES), lambda b, h, qi: (qi, 0)),
            pl.BlockSpec((tq, LANES), lambda b, h, qi: (qi, 0)),
            pl.BlockSpec((seq, LANES), lambda b, h, qi: (0, 0)),
            pl.BlockSpec((seq, LANES), lambda b, h, qi: (0, 0)),
            pl.BlockSpec((4, LANES), lambda b, h, qi: (0, 0)),
            pl.BlockSpec((1, LANES), lambda b, h, qi: (0, 0)),
        ],
        out_specs=pl.BlockSpec((tq, LANES), lambda b, h, qi: (b * nl + qi, h)),
        out_shape=jax.ShapeDtypeStruct((rows.lat, DIFF_V), BF16),
        scratch_shapes=[pltpu.VMEM((seq + ctx, LANES), BF16), pltpu.VMEM((seq + ctx, LANES), BF16)],
        compiler_params=_cparams(("arbitrary", "arbitrary", "arbitrary")),
        name="diff_attn",
    )(hl, hl, hckv, hl, hckv, cos_t, sin_t, cos_t, sin_t, lam_pad, subln.reshape(1, LANES).astype(F32))


def _hid_kernel(z_ref, w1_ref, b1_ref, w2_ref, b2_ref, o_ref):
    h1 = jnp.sin(_dot3(z_ref[...], w1_ref[...]) + b1_ref[...])
    o_ref[...] = jnp.sin(_dot3(h1, w2_ref[...]) + b2_ref[...])


def _filt_kernel(hid_ref, w3f_ref, w3b_ref, dec_ref, o_ref, *, n):
    hf = hid_ref[0:n, :]
    hr = hid_ref[n:2 * n, :]
    w3f, w3b = w3f_ref[...], w3b_ref[...]
    kf = _dot3(hf, w3f)
    kb0 = _dot3(hf[0:8], w3b)[0:1]
    kbr = _dot3(hr, w3b)
    dec = dec_ref[...]
    row = lax.broadcasted_iota(I32, kf.shape, 0)
    inv = 1.0 / (n - 1)
    win_f = jnp.exp(-(row.astype(F32) * inv) * dec) + HY_SHIFT
    win_r = jnp.exp(-((n - row).astype(F32) * inv) * dec) + HY_SHIFT
    head = kf * win_f + jnp.where(row == 0, kb0 * (1.0 + HY_SHIFT), 0.0)
    tail = jnp.where(row == 0, 0.0, kbr * win_r)
    nrm = jnp.sum(jnp.abs(head), 0, keepdims=True) + jnp.sum(jnp.abs(tail), 0, keepdims=True)
    o_ref[0:n, :] = (head / nrm).astype(o_ref.dtype)
    o_ref[n:2 * n, :] = (tail / nrm).astype(o_ref.dtype)


def _hyena_filters(n, width, w1, b1, w2, b2, w3):
    t = jnp.linspace(0.0, 1.0, n, dtype=F32)[:, None]
    ang = (2.0 * math.pi / n) * jnp.arange(n, dtype=F32)[:, None] * jnp.linspace(1e-4, HY_BANDS - 1, HY_BANDS, dtype=F32)[None, :]
    z = jnp.concatenate([t, jnp.cos(ang), -jnp.sin(ang)], -1)
    z_rev = jnp.concatenate([z[:1], z[:0:-1]], 0)
    z2 = jnp.pad(jnp.concatenate([z, z_rev], 0), ((0, 0), (0, LANES - HY_EMB)))
    pad = LANES - HY_FFN
    w1p = jnp.pad(w1.astype(F32), ((0, LANES - HY_EMB), (0, pad)))
    b1p = jnp.pad(b1.astype(F32), (0, pad)).reshape(1, LANES)
    w2p = jnp.pad(w2.astype(F32), ((0, pad), (0, pad)))
    b2p = jnp.pad(b2.astype(F32), (0, pad)).reshape(1, LANES)
    w3p = jnp.pad(w3.astype(F32), ((0, pad), (0, 0)))
    th = min(1024, 2 * n)
    hid = pl.pallas_call(
        _hid_kernel,
        grid=(2 * n // th,),
        in_specs=[
            pl.BlockSpec((th, LANES), lambda i: (i, 0)),
            pl.BlockSpec((LANES, LANES), lambda i: (0, 0)),
            pl.BlockSpec((1, LANES), lambda i: (0, 0)),
            pl.BlockSpec((LANES, LANES), lambda i: (0, 0)),
            pl.BlockSpec((1, LANES), lambda i: (0, 0)),
        ],
        out_specs=pl.BlockSpec((th, LANES), lambda i: (i, 0)),
        out_shape=jax.ShapeDtypeStruct((2 * n, LANES), F32),
        compiler_params=_cparams(("arbitrary",)),
        name="hyena_hidden",
    )(z2, w1p, b1p, w2p, b2p)
    decay = jnp.abs(jnp.linspace(HY_MIN_DECAY, HY_MAX_DECAY, width, dtype=F32)).reshape(1, width)
    tn = 256
    per = width // tn
    return pl.pallas_call(
        functools.partial(_filt_kernel, n=n),
        grid=(HY_ORDER, per),
        in_specs=[
            pl.BlockSpec((2 * n, LANES), lambda o, j: (0, 0)),
            pl.BlockSpec((LANES, tn), lambda o, j: (0, (2 * o) * per + j)),
            pl.BlockSpec((LANES, tn), lambda o, j: (0, (2 * o + 1) * per + j)),
            pl.BlockSpec((1, tn), lambda o, j: (0, j)),
        ],
        out_specs=pl.BlockSpec((None, 2 * n, tn), lambda o, j: (o, 0, j)),
        out_shape=jax.ShapeDtypeStruct((HY_ORDER, 2 * n, width), BF16),
        compiler_params=_cparams(("arbitrary", "arbitrary")),
        name="hyena_filters",
    )(hid, w3p, w3p, decay)


def _dft_tables(n):
    p = 2 * n
    k = jnp.arange(n, dtype=I32)[:, None]
    t = jnp.arange(p, dtype=I32)[None, :]
    ang = ((k * t) % p).astype(F32) * (2.0 * math.pi / p)
    alt = jnp.where(t % 2 == 0, 1.0, -1.0)
    fx = jnp.cos(ang)
    fy = jnp.where(k == 0, alt, -jnp.sin(ang))
    fwd = jnp.stack([fx, fy]).astype(BF16)
    tt = jnp.arange(n, dtype=I32)[:, None]
    kk = jnp.arange(n, dtype=I32)[None, :]
    ang2 = ((tt * kk) % p).astype(F32) * (2.0 * math.pi / p)
    alt2 = jnp.where(tt % 2 == 0, 1.0, -1.0)
    gx = jnp.where(kk == 0, 1.0, 2.0 * jnp.cos(ang2)) / p
    gy = jnp.where(kk == 0, alt2, -2.0 * jnp.sin(ang2)) / p
    inv = jnp.stack([gx, gy]).astype(BF16)
    return fwd, inv


def _spec_kernel(fx_ref, fy_ref, f_ref, o_ref):
    f = f_ref[...]
    o_ref[0] = jnp.dot(fx_ref[...], f, preferred_element_type=F32)
    o_ref[1] = jnp.dot(fy_ref[...], f, preferred_element_type=F32)


def _filter_spectrum(fwd, filt):
    _, n, p = fwd.shape
    width = filt.shape[2]
    tm, tn = min(512, n), 512
    return pl.pallas_call(
        _spec_kernel,
        grid=(HY_ORDER, width // tn, n // tm),
        in_specs=[
            pl.BlockSpec((None, tm, p), lambda o, j, i: (0, i, 0)),
            pl.BlockSpec((None, tm, p), lambda o, j, i: (1, i, 0)),
            pl.BlockSpec((None, p, tn), lambda o, j, i: (o, 0, j)),
        ],
        out_specs=pl.BlockSpec((None, 2, tm, tn), lambda o, j, i: (o, 0, i, j)),
        out_shape=jax.ShapeDtypeStruct((HY_ORDER, 2, n, width), F32),
        compiler_params=_cparams(("arbitrary", "arbitrary", "arbitrary")),
        name="filter_spectrum",
    )(fwd, fwd, filt)


def _dft_fwd_kernel(fx_ref, fy_ref, z_ref, hx_ref, hy_ref, o_ref):
    i = pl.program_id(2)
    z = z_ref[...]
    ux = jnp.dot(fx_ref[...], z, preferred_element_type=F32)
    uy = jnp.dot(fy_ref[...], z, preferred_element_type=F32)
    hx, hy = hx_ref[...], hy_ref[...]
    row = lax.broadcasted_iota(I32, ux.shape, 0) + i * ux.shape[0]
    dc = row == 0
    o_ref[0] = (ux * hx - jnp.where(dc, 0.0, uy * hy)).astype(o_ref.dtype)
    o_ref[1] = jnp.where(dc, uy * hy, ux * hy + uy * hx).astype(o_ref.dtype)


def _dft_forward(fwd, z, z_col0, spec, order, rows):
    n = rows.seq
    width = spec.shape[3]
    tm, tn = min(1024, n), 512
    c0 = z_col0 // tn
    return pl.pallas_call(
        _dft_fwd_kernel,
        grid=(rows.batch, width // tn, n // tm),
        in_specs=[
            pl.BlockSpec((None, tm, n), lambda b, j, i: (0, i, 0)),
            pl.BlockSpec((None, tm, n), lambda b, j, i: (1, i, 0)),
            pl.BlockSpec((n, tn), lambda b, j, i: (b, c0 + j)),
            pl.BlockSpec((None, None, tm, tn), lambda b, j, i: (order, 0, i, j)),
            pl.BlockSpec((None, None, tm, tn), lambda b, j, i: (order, 1, i, j)),
        ],
        out_specs=pl.BlockSpec((None, 2, tm, tn), lambda b, j, i: (b, 0, i, j)),
        out_shape=jax.ShapeDtypeStruct((rows.batch, 2, n, width), BF16),
        compiler_params=_cparams(("arbitrary", "arbitrary", "arbitrary")),
        name="hyena_dft_fwd",
    )(fwd, fwd, z, spec, spec)


def _dft_inv_kernel(gx_ref, gy_ref, s_ref, gate_ref, z_ref, skip_ref, o_ref):
    y = (jnp.dot(gx_ref[...], s_ref[0], preferred_element_type=F32)
         + jnp.dot(gy_ref[...], s_ref[1], preferred_element_type=F32))
    z = z_ref[...].astype(F32)
    o_ref[...] = (gate_ref[...].astype(F32) * (y + z * skip_ref[...])).astype(o_ref.dtype)


def _dft_inverse(inv, spec_z, gates, gate_col0, z, z_col0, skip, order, rows):
    n = rows.seq
    width = spec_z.shape[3]
    tm, tn = min(1024, n), 512
    nb = n // tm
    g0, z0 = gate_col0 // tn, z_col0 // tn
    return pl.pallas_call(
        _dft_inv_kernel,
        grid=(rows.batch, width // tn, nb),
        in_specs=[
            pl.BlockSpec((None, tm, n), lambda b, j, i: (0, i, 0)),
            pl.BlockSpec((None, tm, n), lambda b, j, i: (1, i, 0)),
            pl.BlockSpec((None, 2, n, tn), lambda b, j, i: (b, 0, 0, j)),
            pl.BlockSpec((tm, tn), lambda b, j, i: (b * nb + i, g0 + j)),
            pl.BlockSpec((tm, tn), lambda b, j, i: (b * nb + i, z0 + j)),
            pl.BlockSpec((None, 1, tn), lambda b, j, i: (order, 0, j)),
        ],
        out_specs=pl.BlockSpec((tm, tn), lambda b, j, i: (b * nb + i, j)),
        out_shape=jax.ShapeDtypeStruct((rows.lat, width), BF16),
        compiler_params=_cparams(("arbitrary", "arbitrary", "arbitrary")),
        name="hyena_dft_inv",
    )(inv, inv, spec_z, gates, z, skip.reshape(HY_ORDER, 1, width).astype(F32))


def _router_kernel(x_ref, sh_ref, sc_ref, rw_ref, rb_ref, u_ref, ti_ref, tg_ref, rk_ref, cnt_ref, carry_ref):
    i = pl.program_id(0)

    @pl.when(i == 0)
    def _init():
        carry_ref[...] = jnp.zeros_like(carry_ref)

    u = _rms(x_ref[...]) * (1.0 + sc_ref[...]) + sh_ref[...]
    u_ref[...] = _pack_halves(u)
    logits = _dot3(u, rw_ref[...]) + rb_ref[...]
    tm = logits.shape[0]
    lane = lax.broadcasted_iota(I32, logits.shape, 1).astype(F32)
    neg = jnp.float32(-jnp.inf)
    work = jnp.where(lane < N_EXPERTS, logits, neg)
    vals, idxs = [], []
    for _ in range(TOP_K):
        m = jnp.max(work, -1, keepdims=True)
        idx = jnp.min(jnp.where(work == m, lane, float(LANES)), -1, keepdims=True)
        vals.append(m)
        idxs.append(idx)
        work = jnp.where(lane == idx, neg, work)
    es = [jnp.exp(v - vals[0]) for v in vals]
    den = es[0] + es[1] + es[2] + es[3]
    onehot = jnp.zeros(logits.shape, F32)
    for idx in idxs:
        onehot = onehot + (lane == idx).astype(F32)
    r = lax.broadcasted_iota(I32, (tm, tm), 0)
    c = lax.broadcasted_iota(I32, (tm, tm), 1)
    before = (c < r).astype(BF16)
    carry = carry_ref[0:1, :]
    prefix = jnp.dot(before, onehot.astype(BF16), preferred_element_type=F32) + carry
    ti = jnp.zeros(logits.shape, F32)
    tg = jnp.zeros(logits.shape, F32)
    rk = jnp.zeros(logits.shape, F32)
    for k in range(TOP_K):
        pos = jnp.sum(jnp.where(lane == idxs[k], prefix, 0.0), -1, keepdims=True)
        ti = jnp.where(lane == k, idxs[k], ti)
        tg = jnp.where(lane == k, es[k] / den, tg)
        rk = jnp.where(lane == k, pos, rk)
    ti_ref[...] = ti
    tg_ref[...] = tg
    rk_ref[...] = rk
    total = carry + jnp.sum(onehot, 0, keepdims=True)
    carry_ref[...] = jnp.broadcast_to(total, carry_ref.shape)
    cnt_ref[...] = jnp.broadcast_to(total, cnt_ref.shape)


def _router(h, mod4, rows, layer, n_rows, router_w, router_b):
    d = h.shape[1]
    tm = 256
    rw = jnp.pad(router_w.astype(F32), ((0, 0), (0, LANES - N_EXPERTS)))
    rb = jnp.pad(router_b.astype(F32), (0, LANES - N_EXPERTS)).reshape(1, LANES)
    small = lambda dt: jax.ShapeDtypeStruct((n_rows, LANES), dt)
    return pl.pallas_call(
        _router_kernel,
        grid=(n_rows // tm,),
        in_specs=[
            pl.BlockSpec((tm, d), lambda i: (i, 0)),
            _mod_spec(rows, tm, d, layer, 3, d, 1, 0),
            _mod_spec(rows, tm, d, layer, 4, d, 1, 0),
            pl.BlockSpec((d, LANES), lambda i: (0, 0)),
            pl.BlockSpec((1, LANES), lambda i: (0, 0)),
        ],
        out_specs=[
            pl.BlockSpec((tm, d // 2), lambda i: (i, 0)),
            pl.BlockSpec((tm, LANES), lambda i: (i, 0)),
            pl.BlockSpec((tm, LANES), lambda i: (i, 0)),
            pl.BlockSpec((tm, LANES), lambda i: (i, 0)),
            pl.BlockSpec((8, LANES), lambda i: (0, 0)),
        ],
        out_shape=[jax.ShapeDtypeStruct((n_rows, d // 2), U32), small(F32), small(F32), small(F32),
                   jax.ShapeDtypeStruct((8, LANES), F32)],
        scratch_shapes=[pltpu.VMEM((8, LANES), F32)],
        compiler_params=_cparams(("arbitrary",)),
        name="router",
    )(h, mod4, mod4, rw, rb)


def _row_copy(src_ref, src_row, dst_ref, dst_row, sem):
    return pltpu.make_async_copy(src_ref.at[pl.ds(src_row, 1)], dst_ref.at[pl.ds(dst_row, 1)], sem)


def _gather_kernel(tok_ref, src_ref, o_ref, sem):
    n = o_ref.shape[0]

    def start(r, carry):
        _row_copy(src_ref, tok_ref[0, 0, r], o_ref, r, sem).start()
        return carry

    def wait(r, carry):
        _row_copy(src_ref, 0, o_ref, r, sem).wait()
        return carry

    lax.fori_loop(0, n, start, 0)
    lax.fori_loop(0, n, wait, 0)


def _gather_rows(src, slot_tok, n_blocks):
    width = src.shape[1]
    return pl.pallas_call(
        _gather_kernel,
        grid=(n_blocks,),
        in_specs=[
            pl.BlockSpec((1, 1, MOE_BLK), lambda i: (i, 0, 0), memory_space=pltpu.SMEM),
            pl.BlockSpec(memory_space=pl.ANY),
        ],
        out_specs=pl.BlockSpec((MOE_BLK, width), lambda i: (i, 0)),
        out_shape=jax.ShapeDtypeStruct((n_blocks * MOE_BLK, width), src.dtype),
        scratch_shapes=[pltpu.SemaphoreType.DMA(())],
        compiler_params=_cparams(("arbitrary",)),
        name="moe_gather",
    )(slot_tok.reshape(n_blocks, 1, MOE_BLK), src)


def _expert_kernel(be_ref, x_ref, wgu_ref, bgu_ref, wd_ref, bd_ref, o_ref):
    lo, hi = _unpack_halves(x_ref[...])
    half = lo.shape[1]
    dot = functools.partial(jnp.dot, preferred_element_type=F32)
    gu = dot(lo.astype(BF16), wgu_ref[0:half, :]) + dot(hi.astype(BF16), wgu_ref[half:2 * half, :]) + bgu_ref[...]
    de = gu.shape[1] // 2
    g = jnp.minimum(gu[:, :de], SWIGLU_LIMIT)
    up = jnp.clip(gu[:, de:], -SWIGLU_LIMIT, SWIGLU_LIMIT)
    a = (up + 1.0) * g * jax.nn.sigmoid(SWIGLU_ALPHA * g)
    y = dot(a.astype(BF16), wd_ref[...]) + bd_ref[...]
    o_ref[...] = _pack_halves(y)


def _experts(xs, block_e, w_gu, b_gu, w_down, b_down):
    n_exp, d, de2 = w_gu.shape
    de = de2 // 2
    n_blocks = block_e.shape[0]
    grid_spec = pltpu.PrefetchScalarGridSpec(
        num_scalar_prefetch=1,
        grid=(n_blocks,),
        in_specs=[
            pl.BlockSpec((MOE_BLK, d // 2), lambda i, be: (i, 0)),
            pl.BlockSpec((None, d, de2), lambda i, be: (be[i], 0, 0)),
            pl.BlockSpec((None, 1, de2), lambda i, be: (be[i], 0, 0)),
            pl.BlockSpec((None, de, d), lambda i, be: (be[i], 0, 0)),
            pl.BlockSpec((None, 1, d), lambda i, be: (be[i], 0, 0)),
        ],
        out_specs=pl.BlockSpec((MOE_BLK, d // 2), lambda i, be: (i, 0)),
    )
    return pl.pallas_call(
        _expert_kernel,
        grid_spec=grid_spec,
        out_shape=jax.ShapeDtypeStruct((n_blocks * MOE_BLK, d // 2), U32),
        compiler_params=_cparams(("arbitrary",)),
        name="moe_experts",
    )(block_e, xs, w_gu, b_gu.reshape(n_exp, 1, de2).astype(F32), w_down, b_down.reshape(n_exp, 1, d).astype(F32))


def _combine_kernel(dest_ref, gate_ref, h_ref, g2_ref, fn_ref, yb_ref, o_ref, buf, sem, *, final):
    n = h_ref.shape[0]

    def start(r, carry):
        for k in range(TOP_K):
            _row_copy(yb_ref, dest_ref[0, 0, r * TOP_K + k], buf.at[k], r, sem).start()
        return carry

    def wait(r, carry):
        for k in range(TOP_K):
            _row_copy(yb_ref, 0, buf.at[k], r, sem).wait()
        return carry

    lax.fori_loop(0, n, start, 0)
    lax.fori_loop(0, n, wait, 0)
    gates = gate_ref[...]
    acc_lo = None
    acc_hi = None
    for k in range(TOP_K):
        lo, hi = _unpack_halves(buf[k])
        gk = gates[:, k:k + 1]
        acc_lo = gk * lo if acc_lo is None else acc_lo + gk * lo
        acc_hi = gk * hi if acc_hi is None else acc_hi + gk * hi
    out = h_ref[...] + g2_ref[...] * jnp.concatenate([acc_lo, acc_hi], -1)
    if final:
        out = _rms(out) * fn_ref[...]
    o_ref[...] = out


def _combine(dest, gates, h, mod4, rows, layer, n_rows, yb, final_gain, final):
    d = h.shape[1]
    tm = COMBINE_BLK
    nb = n_rows // tm
    return pl.pallas_call(
        functools.partial(_combine_kernel, final=final),
        grid=(nb,),
        in_specs=[
            pl.BlockSpec((1, 1, tm * TOP_K), lambda i: (i, 0, 0), memory_space=pltpu.SMEM),
            pl.BlockSpec((tm, LANES), lambda i: (i, 0)),
            pl.BlockSpec((tm, d), lambda i: (i, 0)),
            _mod_spec(rows, tm, d, layer, 5, d, 1, 0),
            pl.BlockSpec((1, d), lambda i: (0, 0)),
            pl.BlockSpec(memory_space=pl.ANY),
        ],
        out_specs=pl.BlockSpec((tm, d), lambda i: (i, 0)),
        out_shape=jax.ShapeDtypeStruct((n_rows, d), F32),
        scratch_shapes=[pltpu.VMEM((TOP_K, tm, d // 2), U32), pltpu.SemaphoreType.DMA(())],
        compiler_params=_cparams(("arbitrary",)),
        name="moe_combine",
    )(dest.reshape(nb, 1, tm * TOP_K), gates, h, mod4, final_gain.reshape(1, d).astype(F32), yb)


def _moe(h, mod4, rows, layer, n_rows, router_w, router_b, w_gu, b_gu, w_down, b_down, final_gain, final):
    u_packed, ti, tg, rk, cnt = _router(h, mod4, rows, layer, n_rows, router_w, router_b)
    counts = cnt[0, :N_EXPERTS].astype(I32)
    padded = (counts + MOE_BLK - 1) // MOE_BLK * MOE_BLK
    pad_end = jnp.cumsum(padded)
    pad_start = pad_end - padded
    top_i = ti[:, :TOP_K].astype(I32)
    dest = pad_start[top_i] + rk[:, :TOP_K].astype(I32)
    n_blocks = n_rows * TOP_K // MOE_BLK + N_EXPERTS
    tok = jnp.repeat(jnp.arange(n_rows, dtype=I32), TOP_K)
    slot_tok = jnp.zeros((n_blocks * MOE_BLK,), I32).at[dest.reshape(-1)].set(tok)
    block_e = jnp.minimum(jnp.searchsorted(pad_end, jnp.arange(n_blocks, dtype=I32) * MOE_BLK, side="right"),
                          N_EXPERTS - 1).astype(I32)
    xs = _gather_rows(u_packed, slot_tok, n_blocks)
    yb = _experts(xs, block_e, w_gu.astype(BF16), b_gu, w_down.astype(BF16), b_down)
    return _combine(dest, tg, h, mod4, rows, layer, n_rows, yb, final_gain, final)


def _rope_tables(seq):
    n_rows = seq // GRID_W
    row = jnp.repeat(jnp.arange(n_rows), GRID_W).astype(F32)
    col = jnp.tile(jnp.arange(GRID_W), n_rows).astype(F32)
    quarter = MLA_ROPE_DIM // 4
    inv = 1.0 / (ROPE_THETA ** (jnp.arange(quarter, dtype=F32) / quarter))
    ang = jnp.concatenate([row[:, None] * inv, col[:, None] * inv], -1)
    cos, sin = jnp.cos(ang), jnp.sin(ang)
    return jnp.tile(cos, (1, 4)), jnp.tile(jnp.concatenate([-sin, sin], -1), (1, 2))


def kernel(x, c, ctx, c_ctx, ada_w, ada_b, mla_in_w, mla_q_norm, mla_kv_norm, mla_w_uq, mla_w_ukv, sc_conv_w, even_out_w, odd_in_w, diff_lambda, diff_subln, hy_conv_w, hy_w1, hy_b1, hy_w2, hy_b2, hy_w3, hy_skip, odd_out_w, router_w, router_b, moe_w_gu, moe_b_gu, moe_w_down, moe_b_down, final_norm):
    batch, seq, d = x.shape
    n_ctx = ctx.shape[1]
    depth = ada_w.shape[0]
    assert depth == 2 and batch + 1 <= MOD_ROWS
    rows = _Rows(batch, seq, n_ctx)
    sc_width = d - MLA_OUT
    hy_width = d - DIFF_V

    cond = jnp.zeros((MOD_ROWS, d), F32).at[:batch].set(c).at[batch].set(c_ctx)
    mod4 = _adaln(cond, ada_w, ada_b)
    cos_t, sin_t = _rope_tables(seq)
    h = jnp.concatenate([x.reshape(rows.lat, d), ctx.reshape(batch * n_ctx, d)], 0)

    kpe_col = MLA_Q_RANK + MLA_KV_RANK
    sc_col = 2048
    w_in = mla_in_w[0]
    w_in = jnp.concatenate([
        w_in[:, :kpe_col + MLA_ROPE_DIM],
        jnp.zeros((d, sc_col - kpe_col - MLA_ROPE_DIM), w_in.dtype),
        w_in[:, kpe_col + MLA_ROPE_DIM:],
    ], 1).astype(BF16)
    wq = mla_w_uq[0].reshape(MLA_Q_RANK, MLA_HEADS, MLA_QK_DIM)
    wq_pe = jnp.pad(wq[:, :, MLA_NOPE_DIM:], ((0, 0), (0, 0), (0, LANES - MLA_ROPE_DIM)))
    wq = jnp.concatenate([wq[:, :, :MLA_NOPE_DIM].reshape(MLA_Q_RANK, -1), wq_pe.reshape(MLA_Q_RANK, -1)], 1).astype(BF16)

    u = _modulate(h, mod4, rows, 0, rows.all)
    hl = _matmul([u], w_in, n_rows=rows.all, n_cols=w_in.shape[1], name="even_in_proj")
    qup = _norm_matmul(hl, 0, mla_q_norm[0], wq, rows.all, "mla_q_up")
    kvup = _norm_matmul(hl, MLA_Q_RANK, mla_kv_norm[0], mla_w_ukv[0].astype(BF16), rows.all, "mla_kv_up")
    attn = _mla_attention(qup, kvup, hl, cos_t, sin_t, rows, kpe_col)
    sc = _short_conv(hl, sc_col, sc_width, sc_conv_w[0].astype(F32), rows)
    h = _matmul([attn, sc], even_out_w[0].astype(BF16), n_rows=rows.all, n_cols=d, out_dtype=F32,
                resid=(h, mod4, rows, 0, 2), name="even_out_proj")
    h = _moe(h, mod4, rows, 0, rows.all, router_w[0], router_b[0], moe_w_gu[0], moe_b_gu[0], moe_w_down[0],
             moe_b_down[0], final_norm, False)

    lam_init = 0.8 - 0.6 * math.exp(-0.3 * 1)
    w_in1 = odd_in_w[0].astype(BF16)
    u = _modulate(h, mod4, rows, 1, rows.all)
    hl = _matmul([u], w_in1, n_rows=rows.lat, n_cols=w_in1.shape[1], name="odd_in_proj")
    hckv = _matmul([u], w_in1, n_rows=batch * n_ctx, n_cols=DIFF_QK + DIFF_V, a_row0=rows.lat, w_col0=DIFF_QK,
                   name="odd_ctx_kv_proj")
    attn = _diff_attention(hl, hckv, cos_t, sin_t, diff_lambda[0], diff_subln[0], rows, lam_init)

    hy_col = 2 * DIFF_QK + DIFF_V
    hc3 = _hyena_conv3(hl, hy_col, 3 * hy_width, hy_conv_w[0].astype(F32), rows)
    filt = _hyena_filters(seq, hy_width, hy_w1[0], hy_b1[0], hy_w2[0], hy_b2[0], hy_w3[0])
    fwd, inv = _dft_tables(seq)
    spec = _filter_spectrum(fwd, filt)
    z, z_col = hc3, 0
    for o in range(HY_ORDER):
        spec_z = _dft_forward(fwd, z, z_col, spec, o, rows)
        z = _dft_inverse(inv, spec_z, hc3, (1 + o) * hy_width, z, z_col, hy_skip[0], o, rows)
        z_col = 0

    h = _matmul([attn, z], odd_out_w[0].astype(BF16), n_rows=rows.lat, n_cols=d, out_dtype=F32,
                resid=(h, mod4, rows, 1, 2), name="odd_out_proj")
    out = _moe(h, mod4, rows, 1, rows.lat, router_w[1], router_b[1], moe_w_gu[1], moe_b_gu[1], moe_w_down[1],
               moe_b_down[1], final_norm, True)
    return out.reshape(batch, seq, d)
```

```python
import functools
import math

import jax
import jax.numpy as jnp
from jax import lax
from jax.experimental import pallas as pl
from jax.experimental.pallas import tpu as pltpu

F32 = jnp.float32
BF16 = jnp.bfloat16
U32 = jnp.uint32
I32 = jnp.int32

GRID_W = 64
NORM_EPS = 1e-6
ROPE_THETA = 10000.0

MLA_HEADS = 16
MLA_Q_RANK = 1024
MLA_KV_RANK = 512
MLA_NOPE_DIM = 128
MLA_ROPE_DIM = 64
MLA_V_DIM = 128
MLA_QK_DIM = MLA_NOPE_DIM + MLA_ROPE_DIM
MLA_OUT = MLA_HEADS * MLA_V_DIM

DIFF_HEADS = 16
DIFF_HEAD_DIM = 64
DIFF_QK = DIFF_HEADS * 2 * DIFF_HEAD_DIM
DIFF_V = DIFF_HEADS * 2 * DIFF_HEAD_DIM

HY_ORDER = 2
HY_BANDS = 16
HY_EMB = 1 + 2 * HY_BANDS
HY_FFN = 64
HY_SHIFT = 0.05
HY_MIN_DECAY = math.log(1e-2) / 1.5
HY_MAX_DECAY = math.log(1e-2) / 0.3

N_EXPERTS = 32
TOP_K = 4
SWIGLU_LIMIT = 7.0
SWIGLU_ALPHA = 1.702

LOG2E = 1.4426950408889634
LANES = 128
MOD_ROWS = 16
VMEM_LIMIT = 56 * 1024 * 1024
MOE_BLK = 256
COMBINE_BLK = 128
ATTN_CHUNK = 128


def _cparams(sem):
    return pltpu.CompilerParams(dimension_semantics=sem, vmem_limit_bytes=VMEM_LIMIT)


def _rms(x):
    return x * lax.rsqrt(jnp.mean(x * x, -1, keepdims=True) + NORM_EPS)


def _split_bf16(a):
    hi = a.astype(BF16)
    lo = (a - hi.astype(F32)).astype(BF16)
    return hi, lo


def _dot3(a, b):
    ah, al = _split_bf16(a)
    bh, bl = _split_bf16(b)
    d = functools.partial(jnp.dot, preferred_element_type=F32)
    return d(ah, bh) + d(al, bh) + d(ah, bl)


def _pack_halves(y):
    c = y.shape[1] // 2
    lo = lax.bitcast_convert_type(y[:, :c].astype(BF16).astype(F32), U32)
    hi = lax.bitcast_convert_type(y[:, c:].astype(BF16).astype(F32), U32)
    return (lo >> 16) | (hi & jnp.uint32(0xFFFF0000))


def _unpack_halves(w):
    lo = lax.bitcast_convert_type(w << 16, F32)
    hi = lax.bitcast_convert_type(w & jnp.uint32(0xFFFF0000), F32)
    return lo, hi


def _ones_column(shape):
    lane = lax.broadcasted_iota(I32, shape, 1)
    return jnp.where(lane == 0, 1.0, 0.0).astype(BF16)


def _rope_pairs(x, cos_t, sin_t):
    lane = lax.broadcasted_iota(I32, x.shape, 1)
    swapped = jnp.where((lane & 63) < 32, pltpu.roll(x, 96, 1), pltpu.roll(x, 32, 1))
    return x * cos_t + swapped * sin_t


def _adaln_kernel(c_ref, w_ref, b_ref, o_ref):
    x = c_ref[...]
    s = (x * jax.nn.sigmoid(x)).astype(BF16)
    o_ref[...] = jnp.dot(s, w_ref[...].astype(BF16), preferred_element_type=F32) + b_ref[...]


def _adaln(cond, ada_w, ada_b):
    depth, d, d6 = ada_w.shape
    tn = 512
    out = pl.pallas_call(
        _adaln_kernel,
        grid=(depth, d6 // tn),
        in_specs=[
            pl.BlockSpec((MOD_ROWS, d), lambda l, j: (0, 0)),
            pl.BlockSpec((None, d, tn), lambda l, j: (l, 0, j)),
            pl.BlockSpec((None, 1, tn), lambda l, j: (l, 0, j)),
        ],
        out_specs=pl.BlockSpec((None, MOD_ROWS, tn), lambda l, j: (l, 0, j)),
        out_shape=jax.ShapeDtypeStruct((depth, MOD_ROWS, d6), F32),
        compiler_params=_cparams(("arbitrary", "arbitrary")),
        name="adaln",
    )(cond, ada_w, ada_b.reshape(depth, 1, d6))
    return out.reshape(depth, MOD_ROWS, 1, d6)


class _Rows:
    def __init__(self, batch, seq, ctx):
        self.batch, self.seq, self.ctx = batch, seq, ctx
        self.lat = batch * seq
        self.all = batch * (seq + ctx)

    def mod_row(self, i, tm):
        r = i * tm
        return jnp.where(r < self.lat, r // self.seq, self.batch)


def _mod_spec(rows, tm, tn, layer, chunk, d, grid_rank, row_axis, col_axis=None):
    per = d // tn

    def imap(*g):
        j = 0 if col_axis is None else g[col_axis]
        return (layer, rows.mod_row(g[row_axis], tm), 0, chunk * per + j)

    return pl.BlockSpec((None, None, 1, tn), imap)


def _modulate_kernel(x_ref, sh_ref, sc_ref, o_ref):
    u = _rms(x_ref[...]) * (1.0 + sc_ref[...]) + sh_ref[...]
    o_ref[...] = u.astype(o_ref.dtype)


def _modulate(h, mod4, rows, layer, n_rows):
    d = h.shape[1]
    tm = 256
    return pl.pallas_call(
        _modulate_kernel,
        grid=(n_rows // tm,),
        in_specs=[
            pl.BlockSpec((tm, d), lambda i: (i, 0)),
            _mod_spec(rows, tm, d, layer, 0, d, 1, 0),
            _mod_spec(rows, tm, d, layer, 1, d, 1, 0),
        ],
        out_specs=pl.BlockSpec((tm, d), lambda i: (i, 0)),
        out_shape=jax.ShapeDtypeStruct((n_rows, d), BF16),
        compiler_params=_cparams(("arbitrary",)),
        name="modulate",
    )(h, mod4, mod4)


def _mm_kernel(*refs, ks, resid):
    a_refs = refs[:len(ks)]
    w_ref = refs[len(ks)]
    o_ref = refs[-1]
    acc = None
    off = 0
    for a_ref, k in zip(a_refs, ks):
        part = jnp.dot(a_ref[...], w_ref[off:off + k, :], preferred_element_type=F32)
        acc = part if acc is None else acc + part
        off += k
    if resid:
        h_ref, g_ref = refs[len(ks) + 1], refs[len(ks) + 2]
        o_ref[...] = h_ref[...] + g_ref[...] * acc
    else:
        o_ref[...] = acc.astype(o_ref.dtype)


def _matmul(a_list, w, *, n_rows, n_cols, a_row0=0, w_col0=0, tm=512, tn=1024, out_dtype=BF16,
            resid=None, name="matmul"):
    tn = min(tn, n_cols)
    tm = min(tm, n_rows)
    ks = tuple(a.shape[1] for a in a_list)
    k_all = sum(ks)
    r0, c0 = a_row0 // tm, w_col0 // tn
    in_specs = [pl.BlockSpec((tm, k), lambda j, i: (r0 + i, 0)) for k in ks]
    in_specs.append(pl.BlockSpec((k_all, tn), lambda j, i: (0, c0 + j)))
    args = list(a_list) + [w]
    if resid is not None:
        h, mod4, rows, layer, chunk = resid
        in_specs.append(pl.BlockSpec((tm, tn), lambda j, i: (i, j)))
        in_specs.append(_mod_spec(rows, tm, tn, layer, chunk, h.shape[1], 2, 1, 0))
        args += [h, mod4]
    return pl.pallas_call(
        functools.partial(_mm_kernel, ks=ks, resid=resid is not None),
        grid=(n_cols // tn, n_rows // tm),
        in_specs=in_specs,
        out_specs=pl.BlockSpec((tm, tn), lambda j, i: (i, j)),
        out_shape=jax.ShapeDtypeStruct((n_rows, n_cols), out_dtype),
        compiler_params=_cparams(("arbitrary", "arbitrary")),
        name=name,
    )(*args)


def _norm_mm_kernel(a_ref, g_ref, w_ref, o_ref):
    y = _rms(a_ref[...].astype(F32)) * g_ref[...]
    o_ref[...] = jnp.dot(y.astype(BF16), w_ref[...], preferred_element_type=F32).astype(o_ref.dtype)


def _norm_matmul(a, a_col0, gain, w, n_rows, name):
    kw, n = w.shape
    tm = 512
    cb = a_col0 // kw
    return pl.pallas_call(
        _norm_mm_kernel,
        grid=(n_rows // tm,),
        in_specs=[
            pl.BlockSpec((tm, kw), lambda i: (i, cb)),
            pl.BlockSpec((1, kw), lambda i: (0, 0)),
            pl.BlockSpec((kw, n), lambda i: (0, 0)),
        ],
        out_specs=pl.BlockSpec((tm, n), lambda i: (i, 0)),
        out_shape=jax.ShapeDtypeStruct((n_rows, n), BF16),
        compiler_params=_cparams(("arbitrary",)),
        name=name,
    )(a, gain.reshape(1, kw).astype(F32), w)


def _mla_attn_kernel(qn_ref, qp_ref, kvl_ref, kvc_ref, kpl_ref, kpc_ref, cq_ref, sq_ref, ck_ref, sk_ref,
                     o_ref, kf_ref, vf_ref, *, seq, ctx, n_lat_blocks):
    qi = pl.program_id(2)

    @pl.when(qi == 0)
    def _build_keys():
        kvl = kvl_ref[...]
        kvc = kvc_ref[...]
        kf_ref[0:seq, 0:LANES] = kvl[:, :LANES]
        kf_ref[seq:seq + ctx, 0:LANES] = kvc[:, :LANES]
        vf_ref[0:seq, 0:LANES] = kvl[:, LANES:]
        vf_ref[seq:seq + ctx, 0:LANES] = kvc[:, LANES:]
        vf_ref[:, LANES:2 * LANES] = _ones_column((seq + ctx, LANES))
        kpe = _rope_pairs(kpl_ref[...].astype(F32), ck_ref[...], sk_ref[...])
        kf_ref[0:seq, LANES:2 * LANES] = kpe.astype(BF16)
        kf_ref[seq:seq + ctx, LANES:2 * LANES] = kpc_ref[...]

    def attend(q_all, k, v):
        for r in range(0, q_all.shape[0], ATTN_CHUNK):
            q = q_all[r:r + ATTN_CHUNK]
            s = lax.dot_general(q, k, (((1,), (1,)), ((), ())), preferred_element_type=F32)
            e = jnp.exp2(s - jnp.max(s, -1, keepdims=True)).astype(BF16)
            ov = jnp.dot(e, v, preferred_element_type=F32)
            o_ref[r:r + ATTN_CHUNK, :] = (ov[:, :LANES] / ov[:, LANES:LANES + 1]).astype(o_ref.dtype)

    @pl.when(qi < n_lat_blocks)
    def _latent_queries():
        qpe = _rope_pairs(qp_ref[...].astype(F32), cq_ref[...], sq_ref[...]).astype(BF16)
        q = jnp.concatenate([qn_ref[...], qpe], -1)
        attend(q, kf_ref[...], vf_ref[...])

    @pl.when(qi >= n_lat_blocks)
    def _context_queries():
        q = jnp.concatenate([qn_ref[...], qp_ref[...]], -1)
        attend(q, kf_ref[seq:seq + ctx, :], vf_ref[seq:seq + ctx, :])


def _mla_attention(qup, kvup, hl, cos_t, sin_t, rows, kpe_col):
    batch, seq, ctx = rows.batch, rows.seq, rows.ctx
    tq = 256
    nl, nc = seq // tq, ctx // tq
    lat_blocks = rows.lat // tq

    def qrow(b, qi):
        return jnp.where(qi < nl, b * nl + qi, lat_blocks + b * nc + (qi - nl))

    def qtab(b, h, qi):
        return (jnp.minimum(qi, nl - 1), 0)

    kb = kpe_col // LANES
    ctx0 = rows.lat // ctx
    return pl.pallas_call(
        functools.partial(_mla_attn_kernel, seq=seq, ctx=ctx, n_lat_blocks=nl),
        grid=(batch, MLA_HEADS, nl + nc),
        in_specs=[
            pl.BlockSpec((tq, LANES), lambda b, h, qi: (qrow(b, qi), h)),
            pl.BlockSpec((tq, LANES), lambda b, h, qi: (qrow(b, qi), MLA_HEADS + h)),
            pl.BlockSpec((seq, 2 * LANES), lambda b, h, qi: (b, h)),
            pl.BlockSpec((ctx, 2 * LANES), lambda b, h, qi: (ctx0 + b, h)),
            pl.BlockSpec((seq, LANES), lambda b, h, qi: (b, kb)),
            pl.BlockSpec((ctx, LANES), lambda b, h, qi: (ctx0 + b, kb)),
            pl.BlockSpec((tq, LANES), qtab),
            pl.BlockSpec((tq, LANES), qtab),
            pl.BlockSpec((seq, LANES), lambda b, h, qi: (0, 0)),
            pl.BlockSpec((seq, LANES), lambda b, h, qi: (0, 0)),
        ],
        out_specs=pl.BlockSpec((tq, LANES), lambda b, h, qi: (qrow(b, qi), h)),
        out_shape=jax.ShapeDtypeStruct((rows.all, MLA_OUT), BF16),
        scratch_shapes=[pltpu.VMEM((seq + ctx, 2 * LANES), BF16), pltpu.VMEM((seq + ctx, 2 * LANES), BF16)],
        compiler_params=_cparams(("arbitrary", "arbitrary", "arbitrary")),
        name="mla_attn",
    )(qup, qup, kvup, kvup, hl, hl, cos_t, sin_t, cos_t, sin_t)


def _conv3(p, w):
    n = p.shape[0]
    row = lax.broadcasted_iota(I32, p.shape, 0)
    prev = jnp.where(row == 0, 0.0, pltpu.roll(p, 1, 0))
    nxt = jnp.where(row == n - 1, 0.0, pltpu.roll(p, n - 1, 0))
    return prev * w[0:1] + p * w[1:2] + nxt * w[2:3]


HALO = 16


def _sconv_kernel(gb_ref, gc_ref, hh_ref, gcp_ref, hhp_ref, gcn_ref, hhn_ref, w_ref, o_ref, *, seq, ctx, lat):
    i = pl.program_id(0)
    tr = gc_ref.shape[0]
    r0 = i * tr
    length = jnp.where(r0 < lat, seq, ctx)
    pos = jnp.where(r0 < lat, r0, r0 - lat)
    starts = lax.rem(pos, length) == 0
    ends = lax.rem(pos + tr, length) == 0
    p = gc_ref[...].astype(F32) * hh_ref[...].astype(F32)
    before = gcp_ref[HALO - 1:HALO, :].astype(F32) * hhp_ref[HALO - 1:HALO, :].astype(F32)
    after = gcn_ref[0:1, :].astype(F32) * hhn_ref[0:1, :].astype(F32)
    before = jnp.where(starts, 0.0, before)
    after = jnp.where(ends, 0.0, after)
    row = lax.broadcasted_iota(I32, p.shape, 0)
    prev = jnp.where(row == 0, before, pltpu.roll(p, 1, 0))
    nxt = jnp.where(row == tr - 1, after, pltpu.roll(p, tr - 1, 0))
    w = w_ref[...]
    conv = prev * w[0:1] + p * w[1:2] + nxt * w[2:3]
    o_ref[...] = (gb_ref[...].astype(F32) * conv).astype(o_ref.dtype)


def _short_conv(hl, col0, width, conv_w, rows):
    tr, tc = 256, 1024
    assert rows.seq % tr == 0 and rows.ctx % tr == 0
    nct = width // tc
    c0 = col0 // tc
    per = tr // HALO
    last = rows.all // HALO - 1

    def main(k):
        return pl.BlockSpec((tr, tc), lambda i, j: (i, c0 + k * nct + j))

    def before(k):
        return pl.BlockSpec((HALO, tc), lambda i, j: (jnp.maximum(i * per - 1, 0), c0 + k * nct + j))

    def after(k):
        return pl.BlockSpec((HALO, tc), lambda i, j: (jnp.minimum((i + 1) * per, last), c0 + k * nct + j))

    return pl.pallas_call(
        functools.partial(_sconv_kernel, seq=rows.seq, ctx=rows.ctx, lat=rows.lat),
        grid=(rows.all // tr, nct),
        in_specs=[main(0), main(1), main(2), before(1), before(2), after(1), after(2),
                  pl.BlockSpec((3, tc), lambda i, j: (0, j))],
        out_specs=pl.BlockSpec((tr, tc), lambda i, j: (i, j)),
        out_shape=jax.ShapeDtypeStruct((rows.all, width), BF16),
        compiler_params=_cparams(("arbitrary", "arbitrary")),
        name="short_conv",
    )(hl, hl, hl, hl, hl, hl, hl, conv_w)


def _dwconv_kernel(x_ref, w_ref, o_ref):
    o_ref[...] = _conv3(x_ref[...].astype(F32), w_ref[...]).astype(o_ref.dtype)


def _hyena_conv3(hl, col0, width, conv_w, rows):
    tc = 256
    c0 = col0 // tc
    return pl.pallas_call(
        _dwconv_kernel,
        grid=(rows.batch, width // tc),
        in_specs=[
            pl.BlockSpec((rows.seq, tc), lambda s, j: (s, c0 + j)),
            pl.BlockSpec((3, tc), lambda s, j: (0, j)),
        ],
        out_specs=pl.BlockSpec((rows.seq, tc), lambda s, j: (s, j)),
        out_shape=jax.ShapeDtypeStruct((rows.lat, width), BF16),
        compiler_params=_cparams(("arbitrary", "arbitrary")),
        name="hyena_conv3",
    )(hl, conv_w)


def _diff_attn_kernel(q_ref, kl_ref, kc_ref, vl_ref, vc_ref, cq_ref, sq_ref, ck_ref, sk_ref, lam_ref, sub_ref,
                      o_ref, kf_ref, vf_ref, *, seq, ctx, lam_init):
    qi = pl.program_id(2)

    @pl.when(qi == 0)
    def _build_keys():
        kf_ref[0:seq, :] = _rope_pairs(kl_ref[...].astype(F32), ck_ref[...], sk_ref[...]).astype(BF16)
        kf_ref[seq:seq + ctx, :] = kc_ref[...]
        vf_ref[0:seq, 0:LANES] = vl_ref[...]
        vf_ref[seq:seq + ctx, 0:LANES] = vc_ref[...]
        vf_ref[:, LANES:2 * LANES] = _ones_column((seq + ctx, LANES))

    q_all = _rope_pairs(q_ref[...].astype(F32), cq_ref[...], sq_ref[...])
    lp = lam_ref[...]
    lam = (jnp.exp(jnp.sum(lp[0:1] * lp[1:2], -1, keepdims=True))
           - jnp.exp(jnp.sum(lp[2:3] * lp[3:4], -1, keepdims=True)) + lam_init)
    k, v = kf_ref[...], vf_ref[...]
    for r in range(0, q_all.shape[0], ATTN_CHUNK):
        q = q_all[r:r + ATTN_CHUNK]
        lane = lax.broadcasted_iota(I32, q.shape, 1)
        first = lane < DIFF_HEAD_DIM
        qq = jnp.concatenate([jnp.where(first, q, 0.0), jnp.where(first, 0.0, q)], 0).astype(BF16)
        s = lax.dot_general(qq, k, (((1,), (1,)), ((), ())), preferred_element_type=F32)
        e = jnp.exp2(s - jnp.max(s, -1, keepdims=True)).astype(BF16)
        ov = jnp.dot(e, v, preferred_element_type=F32)
        on = ov[:, :LANES] / ov[:, LANES:LANES + 1]
        o = on[:ATTN_CHUNK] - lam * on[ATTN_CHUNK:]
        o_ref[r:r + ATTN_CHUNK, :] = (_rms(o) * sub_ref[...] * (1.0 - lam_init)).astype(o_ref.dtype)


def _diff_attention(hl, hckv, cos_t, sin_t, lam_p, subln, rows, lam_init):
    batch, seq, ctx = rows.batch, rows.seq, rows.ctx
    tq = 512
    nl = seq // tq
    kc0 = DIFF_QK // LANES
    vc0 = 2 * DIFF_QK // LANES
    lam_pad = jnp.pad(lam_p.astype(F32), ((0, 0), (0, LANES - lam_p.shape[1])))
    return pl.pallas_call(
        functools.partial(_diff_attn_kernel, seq=seq, ctx=ctx, lam_init=lam_init),
        grid=(batch, DIFF_HEADS, nl),
        in_specs=[
            pl.BlockSpec((tq, LANES), lambda b, h, qi: (b * nl + qi, h)),
            pl.BlockSpec((seq, LANES), lambda b, h, qi: (b, kc0 + h)),
            pl.BlockSpec((ctx, LANES), lambda b, h, qi: (b, h)),
            pl.BlockSpec((seq, LANES), lambda b, h, qi: (b, vc0 + h)),
            pl.BlockSpec((ctx, LANES), lambda b, h, qi: (b, kc0 + h)),
            pl.BlockSpec((tq, LANES), lambda b, h, qi: (qi, 0)),
            pl.BlockSpec((tq, LANES), lambda b, h, qi: (qi, 0)),
            pl.BlockSpec((seq, LANES), lambda b, h, qi: (0, 0)),
            pl.BlockSpec((seq, LANES), lambda b, h, qi: (0, 0)),
            pl.BlockSpec((4, LANES), lambda b, h, qi: (0, 0)),
            pl.BlockSpec((1, LANES), lambda b, h, qi: (0, 0)),
        ],
        out_specs=pl.BlockSpec((tq, LANES), lambda b, h, qi: (b * nl + qi, h)),
        out_shape=jax.ShapeDtypeStruct((rows.lat, DIFF_V), BF16),
        scratch_shapes=[pltpu.VMEM((seq + ctx, LANES), BF16), pltpu.VMEM((seq + ctx, 2 * LANES), BF16)],
        compiler_params=_cparams(("arbitrary", "arbitrary", "arbitrary")),
        name="diff_attn",
    )(hl, hl, hckv, hl, hckv, cos_t, sin_t, cos_t, sin_t, lam_pad, subln.reshape(1, LANES).astype(F32))


def _hid_kernel(z_ref, w1_ref, b1_ref, w2_ref, b2_ref, o_ref):
    h1 = jnp.sin(_dot3(z_ref[...], w1_ref[...]) + b1_ref[...])
    o_ref[...] = jnp.sin(_dot3(h1, w2_ref[...]) + b2_ref[...])


def _filt_kernel(hid_ref, w3f_ref, w3b_ref, dec_ref, o_ref, *, n):
    hf = hid_ref[0:n, :]
    hr = hid_ref[n:2 * n, :]
    w3f, w3b = w3f_ref[...], w3b_ref[...]
    kf = _dot3(hf, w3f)
    kb0 = _dot3(hf[0:8], w3b)[0:1]
    kbr = _dot3(hr, w3b)
    dec = dec_ref[...]
    row = lax.broadcasted_iota(I32, kf.shape, 0)
    inv = 1.0 / (n - 1)
    win_f = jnp.exp(-(row.astype(F32) * inv) * dec) + HY_SHIFT
    win_r = jnp.exp(-((n - row).astype(F32) * inv) * dec) + HY_SHIFT
    head = kf * win_f + jnp.where(row == 0, kb0 * (1.0 + HY_SHIFT), 0.0)
    tail = jnp.where(row == 0, 0.0, kbr * win_r)
    nrm = jnp.sum(jnp.abs(head), 0, keepdims=True) + jnp.sum(jnp.abs(tail), 0, keepdims=True)
    o_ref[0:n, :] = (head / nrm).astype(o_ref.dtype)
    o_ref[n:2 * n, :] = (tail / nrm).astype(o_ref.dtype)


def _hyena_filters(n, width, w1, b1, w2, b2, w3):
    t = jnp.linspace(0.0, 1.0, n, dtype=F32)[:, None]
    ang = (2.0 * math.pi / n) * jnp.arange(n, dtype=F32)[:, None] * jnp.linspace(1e-4, HY_BANDS - 1, HY_BANDS, dtype=F32)[None, :]
    z = jnp.concatenate([t, jnp.cos(ang), -jnp.sin(ang)], -1)
    z_rev = jnp.concatenate([z[:1], z[:0:-1]], 0)
    z2 = jnp.pad(jnp.concatenate([z, z_rev], 0), ((0, 0), (0, LANES - HY_EMB)))
    pad = LANES - HY_FFN
    w1p = jnp.pad(w1.astype(F32), ((0, LANES - HY_EMB), (0, pad)))
    b1p = jnp.pad(b1.astype(F32), (0, pad)).reshape(1, LANES)
    w2p = jnp.pad(w2.astype(F32), ((0, pad), (0, pad)))
    b2p = jnp.pad(b2.astype(F32), (0, pad)).reshape(1, LANES)
    w3p = jnp.pad(w3.astype(F32), ((0, pad), (0, 0)))
    th = min(1024, 2 * n)
    hid = pl.pallas_call(
        _hid_kernel,
        grid=(2 * n // th,),
        in_specs=[
            pl.BlockSpec((th, LANES), lambda i: (i, 0)),
            pl.BlockSpec((LANES, LANES), lambda i: (0, 0)),
            pl.BlockSpec((1, LANES), lambda i: (0, 0)),
            pl.BlockSpec((LANES, LANES), lambda i: (0, 0)),
            pl.BlockSpec((1, LANES), lambda i: (0, 0)),
        ],
        out_specs=pl.BlockSpec((th, LANES), lambda i: (i, 0)),
        out_shape=jax.ShapeDtypeStruct((2 * n, LANES), F32),
        compiler_params=_cparams(("arbitrary",)),
        name="hyena_hidden",
    )(z2, w1p, b1p, w2p, b2p)
    decay = jnp.abs(jnp.linspace(HY_MIN_DECAY, HY_MAX_DECAY, width, dtype=F32)).reshape(1, width)
    tn = 256
    per = width // tn
    return pl.pallas_call(
        functools.partial(_filt_kernel, n=n),
        grid=(HY_ORDER, per),
        in_specs=[
            pl.BlockSpec((2 * n, LANES), lambda o, j: (0, 0)),
            pl.BlockSpec((LANES, tn), lambda o, j: (0, (2 * o) * per + j)),
            pl.BlockSpec((LANES, tn), lambda o, j: (0, (2 * o + 1) * per + j)),
            pl.BlockSpec((1, tn), lambda o, j: (0, j)),
        ],
        out_specs=pl.BlockSpec((None, 2 * n, tn), lambda o, j: (o, 0, j)),
        out_shape=jax.ShapeDtypeStruct((HY_ORDER, 2 * n, width), BF16),
        compiler_params=_cparams(("arbitrary", "arbitrary")),
        name="hyena_filters",
    )(hid, w3p, w3p, decay)


def _dft_tables(n):
    p = 2 * n
    k = jnp.arange(n, dtype=I32)[:, None]
    t = jnp.arange(p, dtype=I32)[None, :]
    ang = ((k * t) % p).astype(F32) * (2.0 * math.pi / p)
    alt = jnp.where(t % 2 == 0, 1.0, -1.0)
    fx = jnp.cos(ang)
    fy = jnp.where(k == 0, alt, -jnp.sin(ang))
    fwd = jnp.stack([fx, fy]).astype(BF16)
    tt = jnp.arange(n, dtype=I32)[:, None]
    kk = jnp.arange(n, dtype=I32)[None, :]
    ang2 = ((tt * kk) % p).astype(F32) * (2.0 * math.pi / p)
    alt2 = jnp.where(tt % 2 == 0, 1.0, -1.0)
    gx = jnp.where(kk == 0, 1.0, 2.0 * jnp.cos(ang2)) / p
    gy = jnp.where(kk == 0, alt2, -2.0 * jnp.sin(ang2)) / p
    inv = jnp.stack([gx, gy]).astype(BF16)
    return fwd, inv


def _spec_kernel(fx_ref, fy_ref, f_ref, o_ref):
    f = f_ref[...]
    o_ref[0] = jnp.dot(fx_ref[...], f, preferred_element_type=F32)
    o_ref[1] = jnp.dot(fy_ref[...], f, preferred_element_type=F32)


def _filter_spectrum(fwd, filt):
    _, n, p = fwd.shape
    width = filt.shape[2]
    tm, tn = min(512, n), 512
    return pl.pallas_call(
        _spec_kernel,
        grid=(HY_ORDER, width // tn, n // tm),
        in_specs=[
            pl.BlockSpec((None, tm, p), lambda o, j, i: (0, i, 0)),
            pl.BlockSpec((None, tm, p), lambda o, j, i: (1, i, 0)),
            pl.BlockSpec((None, p, tn), lambda o, j, i: (o, 0, j)),
        ],
        out_specs=pl.BlockSpec((None, 2, tm, tn), lambda o, j, i: (o, 0, i, j)),
        out_shape=jax.ShapeDtypeStruct((HY_ORDER, 2, n, width), F32),
        compiler_params=_cparams(("arbitrary", "arbitrary", "arbitrary")),
        name="filter_spectrum",
    )(fwd, fwd, filt)


def _dft_fwd_kernel(fx_ref, fy_ref, z_ref, hx_ref, hy_ref, o_ref):
    i = pl.program_id(2)
    z = z_ref[...]
    ux = jnp.dot(fx_ref[...], z, preferred_element_type=F32)
    uy = jnp.dot(fy_ref[...], z, preferred_element_type=F32)
    hx, hy = hx_ref[...], hy_ref[...]
    row = lax.broadcasted_iota(I32, ux.shape, 0) + i * ux.shape[0]
    dc = row == 0
    o_ref[0] = (ux * hx - jnp.where(dc, 0.0, uy * hy)).astype(o_ref.dtype)
    o_ref[1] = jnp.where(dc, uy * hy, ux * hy + uy * hx).astype(o_ref.dtype)


def _dft_forward(fwd, z, z_col0, spec, order, rows):
    n = rows.seq
    width = spec.shape[3]
    tm, tn = min(1024, n), 512
    c0 = z_col0 // tn
    return pl.pallas_call(
        _dft_fwd_kernel,
        grid=(rows.batch, width // tn, n // tm),
        in_specs=[
            pl.BlockSpec((None, tm, n), lambda b, j, i: (0, i, 0)),
            pl.BlockSpec((None, tm, n), lambda b, j, i: (1, i, 0)),
            pl.BlockSpec((n, tn), lambda b, j, i: (b, c0 + j)),
            pl.BlockSpec((None, None, tm, tn), lambda b, j, i: (order, 0, i, j)),
            pl.BlockSpec((None, None, tm, tn), lambda b, j, i: (order, 1, i, j)),
        ],
        out_specs=pl.BlockSpec((None, 2, tm, tn), lambda b, j, i: (b, 0, i, j)),
        out_shape=jax.ShapeDtypeStruct((rows.batch, 2, n, width), BF16),
        compiler_params=_cparams(("arbitrary", "arbitrary", "arbitrary")),
        name="hyena_dft_fwd",
    )(fwd, fwd, z, spec, spec)


def _dft_inv_kernel(gx_ref, gy_ref, s_ref, gate_ref, z_ref, skip_ref, o_ref):
    y = (jnp.dot(gx_ref[...], s_ref[0], preferred_element_type=F32)
         + jnp.dot(gy_ref[...], s_ref[1], preferred_element_type=F32))
    z = z_ref[...].astype(F32)
    o_ref[...] = (gate_ref[...].astype(F32) * (y + z * skip_ref[...])).astype(o_ref.dtype)


def _dft_inverse(inv, spec_z, gates, gate_col0, z, z_col0, skip, order, rows):
    n = rows.seq
    width = spec_z.shape[3]
    tm, tn = min(1024, n), 512
    nb = n // tm
    g0, z0 = gate_col0 // tn, z_col0 // tn
    return pl.pallas_call(
        _dft_inv_kernel,
        grid=(rows.batch, width // tn, nb),
        in_specs=[
            pl.BlockSpec((None, tm, n), lambda b, j, i: (0, i, 0)),
            pl.BlockSpec((None, tm, n), lambda b, j, i: (1, i, 0)),
            pl.BlockSpec((None, 2, n, tn), lambda b, j, i: (b, 0, 0, j)),
            pl.BlockSpec((tm, tn), lambda b, j, i: (b * nb + i, g0 + j)),
            pl.BlockSpec((tm, tn), lambda b, j, i: (b * nb + i, z0 + j)),
            pl.BlockSpec((None, 1, tn), lambda b, j, i: (order, 0, j)),
        ],
        out_specs=pl.BlockSpec((tm, tn), lambda b, j, i: (b * nb + i, j)),
        out_shape=jax.ShapeDtypeStruct((rows.lat, width), BF16),
        compiler_params=_cparams(("arbitrary", "arbitrary", "arbitrary")),
        name="hyena_dft_inv",
    )(inv, inv, spec_z, gates, z, skip.reshape(HY_ORDER, 1, width).astype(F32))


def _router_kernel(x_ref, sh_ref, sc_ref, rw_ref, rb_ref, u_ref, ti_ref, tg_ref, rk_ref, cnt_ref, carry_ref):
    i = pl.program_id(0)

    @pl.when(i == 0)
    def _init():
        carry_ref[...] = jnp.zeros_like(carry_ref)

    u = _rms(x_ref[...]) * (1.0 + sc_ref[...]) + sh_ref[...]
    u_ref[...] = _pack_halves(u)
    logits = _dot3(u, rw_ref[...]) + rb_ref[...]
    tm = logits.shape[0]
    lane = lax.broadcasted_iota(I32, logits.shape, 1).astype(F32)
    neg = jnp.float32(-jnp.inf)
    work = jnp.where(lane < N_EXPERTS, logits, neg)
    vals, idxs = [], []
    for _ in range(TOP_K):
        m = jnp.max(work, -1, keepdims=True)
        idx = jnp.min(jnp.where(work == m, lane, float(LANES)), -1, keepdims=True)
        vals.append(m)
        idxs.append(idx)
        work = jnp.where(lane == idx, neg, work)
    es = [jnp.exp(v - vals[0]) for v in vals]
    den = es[0] + es[1] + es[2] + es[3]
    onehot = jnp.zeros(logits.shape, F32)
    for idx in idxs:
        onehot = onehot + (lane == idx).astype(F32)
    r = lax.broadcasted_iota(I32, (tm, tm), 0)
    c = lax.broadcasted_iota(I32, (tm, tm), 1)
    before = (c < r).astype(BF16)
    carry = carry_ref[0:1, :]
    prefix = jnp.dot(before, onehot.astype(BF16), preferred_element_type=F32) + carry
    ti = jnp.zeros(logits.shape, F32)
    tg = jnp.zeros(logits.shape, F32)
    rk = jnp.zeros(logits.shape, F32)
    for k in range(TOP_K):
        pos = jnp.sum(jnp.where(lane == idxs[k], prefix, 0.0), -1, keepdims=True)
        ti = jnp.where(lane == k, idxs[k], ti)
        tg = jnp.where(lane == k, es[k] / den, tg)
        rk = jnp.where(lane == k, pos, rk)
    ti_ref[...] = ti
    tg_ref[...] = tg
    rk_ref[...] = rk
    total = carry + jnp.sum(onehot, 0, keepdims=True)
    carry_ref[...] = jnp.broadcast_to(total, carry_ref.shape)
    cnt_ref[...] = jnp.broadcast_to(total, cnt_ref.shape)


def _router(h, mod4, rows, layer, n_rows, router_w, router_b):
    d = h.shape[1]
    tm = 256
    rw = jnp.pad(router_w.astype(F32), ((0, 0), (0, LANES - N_EXPERTS)))
    rb = jnp.pad(router_b.astype(F32), (0, LANES - N_EXPERTS)).reshape(1, LANES)
    small = lambda dt: jax.ShapeDtypeStruct((n_rows, LANES), dt)
    return pl.pallas_call(
        _router_kernel,
        grid=(n_rows // tm,),
        in_specs=[
            pl.BlockSpec((tm, d), lambda i: (i, 0)),
            _mod_spec(rows, tm, d, layer, 3, d, 1, 0),
            _mod_spec(rows, tm, d, layer, 4, d, 1, 0),
            pl.BlockSpec((d, LANES), lambda i: (0, 0)),
            pl.BlockSpec((1, LANES), lambda i: (0, 0)),
        ],
        out_specs=[
            pl.BlockSpec((tm, d // 2), lambda i: (i, 0)),
            pl.BlockSpec((tm, LANES), lambda i: (i, 0)),
            pl.BlockSpec((tm, LANES), lambda i: (i, 0)),
            pl.BlockSpec((tm, LANES), lambda i: (i, 0)),
            pl.BlockSpec((8, LANES), lambda i: (0, 0)),
        ],
        out_shape=[jax.ShapeDtypeStruct((n_rows, d // 2), U32), small(F32), small(F32), small(F32),
                   jax.ShapeDtypeStruct((8, LANES), F32)],
        scratch_shapes=[pltpu.VMEM((8, LANES), F32)],
        compiler_params=_cparams(("arbitrary",)),
        name="router",
    )(h, mod4, mod4, rw, rb)


def _row_copy(src_ref, src_row, dst_ref, dst_row, sem):
    return pltpu.make_async_copy(src_ref.at[pl.ds(src_row, 1)], dst_ref.at[pl.ds(dst_row, 1)], sem)


def _block_copy(src_ref, dst_ref, sem):
    return pltpu.make_async_copy(src_ref.at[pl.ds(0, dst_ref.shape[0])], dst_ref, sem)


def _expert_kernel(be_ref, tok_ref, tok_next_ref, src_ref, wgu_ref, bgu_ref, wd_ref, bd_ref, o_ref, xbuf, sems):
    i = pl.program_id(0)
    slot = lax.rem(i, 2)

    def gather(ids_ref, s):
        def start(r, carry):
            _row_copy(src_ref, ids_ref[0, 0, r], xbuf.at[s], r, sems.at[s]).start()
            return carry

        lax.fori_loop(0, MOE_BLK, start, 0, unroll=8)

    @pl.when(i == 0)
    def _first():
        gather(tok_ref, 0)

    @pl.when(i + 1 < pl.num_programs(0))
    def _ahead():
        gather(tok_next_ref, 1 - slot)

    _block_copy(src_ref, xbuf.at[slot], sems.at[slot]).wait()
    lo, hi = _unpack_halves(xbuf[slot])
    half = lo.shape[1]
    dot = functools.partial(jnp.dot, preferred_element_type=F32)
    gu = dot(lo.astype(BF16), wgu_ref[0:half, :]) + dot(hi.astype(BF16), wgu_ref[half:2 * half, :]) + bgu_ref[...]
    de = gu.shape[1] // 2
    g = jnp.minimum(gu[:, :de], SWIGLU_LIMIT)
    up = jnp.clip(gu[:, de:], -SWIGLU_LIMIT, SWIGLU_LIMIT)
    a = (up + 1.0) * g * jax.nn.sigmoid(SWIGLU_ALPHA * g)
    y = dot(a.astype(BF16), wd_ref[...]) + bd_ref[...]
    o_ref[...] = _pack_halves(y)


def _experts(u_packed, slot_tok, block_e, layer, w_gu, b_gu, w_down, b_down):
    _, n_exp, d, de2 = w_gu.shape
    de = de2 // 2
    n_blocks = block_e.shape[0]
    tok = slot_tok.reshape(n_blocks, 1, MOE_BLK)
    grid_spec = pltpu.PrefetchScalarGridSpec(
        num_scalar_prefetch=1,
        grid=(n_blocks,),
        in_specs=[
            pl.BlockSpec((1, 1, MOE_BLK), lambda i, be: (i, 0, 0), memory_space=pltpu.SMEM),
            pl.BlockSpec((1, 1, MOE_BLK), lambda i, be: (jnp.minimum(i + 1, n_blocks - 1), 0, 0),
                         memory_space=pltpu.SMEM),
            pl.BlockSpec(memory_space=pl.ANY),
            pl.BlockSpec((None, None, d, de2), lambda i, be: (layer, be[i], 0, 0)),
            pl.BlockSpec((None, None, 1, de2), lambda i, be: (layer, be[i], 0, 0)),
            pl.BlockSpec((None, None, de, d), lambda i, be: (layer, be[i], 0, 0)),
            pl.BlockSpec((None, None, 1, d), lambda i, be: (layer, be[i], 0, 0)),
        ],
        out_specs=pl.BlockSpec((MOE_BLK, d // 2), lambda i, be: (i, 0)),
        scratch_shapes=[pltpu.VMEM((2, MOE_BLK, d // 2), U32), pltpu.SemaphoreType.DMA((2,))],
    )
    return pl.pallas_call(
        _expert_kernel,
        grid_spec=grid_spec,
        out_shape=jax.ShapeDtypeStruct((n_blocks * MOE_BLK, d // 2), U32),
        compiler_params=_cparams(("arbitrary",)),
        name="moe_experts",
    )(block_e, tok, tok, u_packed, w_gu, b_gu.reshape(-1, n_exp, 1, de2).astype(F32), w_down,
      b_down.reshape(-1, n_exp, 1, d).astype(F32))


def _combine_kernel(dest_ref, dest_next_ref, gate_ref, h_ref, g2_ref, fn_ref, yb_ref, o_ref, buf, sems, *, final):
    n = h_ref.shape[0]
    i = pl.program_id(0)
    slot = lax.rem(i, 2)

    def gather(ids_ref, s):
        def start(r, carry):
            for k in range(TOP_K):
                _row_copy(yb_ref, ids_ref[0, 0, r * TOP_K + k], buf.at[s], k * n + r, sems.at[s]).start()
            return carry

        lax.fori_loop(0, n, start, 0, unroll=4)

    @pl.when(i == 0)
    def _first():
        gather(dest_ref, 0)

    @pl.when(i + 1 < pl.num_programs(0))
    def _ahead():
        gather(dest_next_ref, 1 - slot)

    _block_copy(yb_ref, buf.at[slot], sems.at[slot]).wait()
    gates = gate_ref[...]
    acc_lo = None
    acc_hi = None
    for k in range(TOP_K):
        lo, hi = _unpack_halves(buf[slot, k * n:(k + 1) * n, :])
        gk = gates[:, k:k + 1]
        acc_lo = gk * lo if acc_lo is None else acc_lo + gk * lo
        acc_hi = gk * hi if acc_hi is None else acc_hi + gk * hi
    out = h_ref[...] + g2_ref[...] * jnp.concatenate([acc_lo, acc_hi], -1)
    if final:
        out = _rms(out) * fn_ref[...]
    o_ref[...] = out


def _combine(dest, gates, h, mod4, rows, layer, n_rows, yb, final_gain, final):
    d = h.shape[1]
    tm = COMBINE_BLK
    nb = n_rows // tm
    dest3 = dest.reshape(nb, 1, tm * TOP_K)
    return pl.pallas_call(
        functools.partial(_combine_kernel, final=final),
        grid=(nb,),
        in_specs=[
            pl.BlockSpec((1, 1, tm * TOP_K), lambda i: (i, 0, 0), memory_space=pltpu.SMEM),
            pl.BlockSpec((1, 1, tm * TOP_K), lambda i: (jnp.minimum(i + 1, nb - 1), 0, 0), memory_space=pltpu.SMEM),
            pl.BlockSpec((tm, LANES), lambda i: (i, 0)),
            pl.BlockSpec((tm, d), lambda i: (i, 0)),
            _mod_spec(rows, tm, d, layer, 5, d, 1, 0),
            pl.BlockSpec((1, d), lambda i: (0, 0)),
            pl.BlockSpec(memory_space=pl.ANY),
        ],
        out_specs=pl.BlockSpec((tm, d), lambda i: (i, 0)),
        out_shape=jax.ShapeDtypeStruct((n_rows, d), F32),
        scratch_shapes=[pltpu.VMEM((2, TOP_K * tm, d // 2), U32), pltpu.SemaphoreType.DMA((2,))],
        compiler_params=_cparams(("arbitrary",)),
        name="moe_combine",
    )(dest3, dest3, gates, h, mod4, final_gain.reshape(1, d).astype(F32), yb)


def _moe(h, mod4, rows, layer, n_rows, router_w, router_b, w_gu, b_gu, w_down, b_down, final_gain, final):
    u_packed, ti, tg, rk, cnt = _router(h, mod4, rows, layer, n_rows, router_w, router_b)
    counts = cnt[0, :N_EXPERTS].astype(I32)
    padded = (counts + MOE_BLK - 1) // MOE_BLK * MOE_BLK
    pad_end = jnp.cumsum(padded)
    pad_start = pad_end - padded
    top_i = ti[:, :TOP_K].astype(I32)
    dest = pad_start[top_i] + rk[:, :TOP_K].astype(I32)
    n_blocks = n_rows * TOP_K // MOE_BLK + N_EXPERTS
    tok = jnp.repeat(jnp.arange(n_rows, dtype=I32), TOP_K)
    slot_tok = jnp.zeros((n_blocks * MOE_BLK,), I32).at[dest.reshape(-1)].set(tok)
    block_start = jnp.arange(n_blocks, dtype=I32)[:, None] * MOE_BLK
    block_e = jnp.minimum(jnp.sum((pad_end[None, :] <= block_start).astype(I32), 1), N_EXPERTS - 1)
    yb = _experts(u_packed, slot_tok, block_e, layer, w_gu, b_gu, w_down, b_down)
    return _combine(dest, tg, h, mod4, rows, layer, n_rows, yb, final_gain, final)


def _rope_tables(seq):
    n_rows = seq // GRID_W
    row = jnp.repeat(jnp.arange(n_rows), GRID_W).astype(F32)
    col = jnp.tile(jnp.arange(GRID_W), n_rows).astype(F32)
    quarter = MLA_ROPE_DIM // 4
    inv = 1.0 / (ROPE_THETA ** (jnp.arange(quarter, dtype=F32) / quarter))
    ang = jnp.concatenate([row[:, None] * inv, col[:, None] * inv], -1)
    cos, sin = jnp.cos(ang), jnp.sin(ang)
    return jnp.tile(cos, (1, 4)), jnp.tile(jnp.concatenate([-sin, sin], -1), (1, 2))


def kernel(x, c, ctx, c_ctx, ada_w, ada_b, mla_in_w, mla_q_norm, mla_kv_norm, mla_w_uq, mla_w_ukv, sc_conv_w, even_out_w, odd_in_w, diff_lambda, diff_subln, hy_conv_w, hy_w1, hy_b1, hy_w2, hy_b2, hy_w3, hy_skip, odd_out_w, router_w, router_b, moe_w_gu, moe_b_gu, moe_w_down, moe_b_down, final_norm):
    batch, seq, d = x.shape
    n_ctx = ctx.shape[1]
    depth = ada_w.shape[0]
    assert depth == 2 and batch + 1 <= MOD_ROWS
    rows = _Rows(batch, seq, n_ctx)
    sc_width = d - MLA_OUT
    hy_width = d - DIFF_V

    cond = jnp.zeros((MOD_ROWS, d), F32).at[:batch].set(c).at[batch].set(c_ctx)
    mod4 = _adaln(cond, ada_w, ada_b)
    cos_t, sin_t = _rope_tables(seq)
    h = jnp.concatenate([x.reshape(rows.lat, d), ctx.reshape(batch * n_ctx, d)], 0)

    kpe_col = MLA_Q_RANK + MLA_KV_RANK
    sc_col = 2048
    w_in = mla_in_w[0]
    w_in = jnp.concatenate([
        w_in[:, :kpe_col + MLA_ROPE_DIM],
        jnp.zeros((d, sc_col - kpe_col - MLA_ROPE_DIM), w_in.dtype),
        w_in[:, kpe_col + MLA_ROPE_DIM:],
    ], 1).astype(BF16)
    wq = (mla_w_uq[0] * (MLA_QK_DIM ** -0.5 * LOG2E)).reshape(MLA_Q_RANK, MLA_HEADS, MLA_QK_DIM)
    wq_pe = jnp.pad(wq[:, :, MLA_NOPE_DIM:], ((0, 0), (0, 0), (0, LANES - MLA_ROPE_DIM)))
    wq = jnp.concatenate([wq[:, :, :MLA_NOPE_DIM].reshape(MLA_Q_RANK, -1), wq_pe.reshape(MLA_Q_RANK, -1)], 1).astype(BF16)

    u = _modulate(h, mod4, rows, 0, rows.all)
    hl = _matmul([u], w_in, n_rows=rows.all, n_cols=w_in.shape[1], name="even_in_proj")
    qup = _norm_matmul(hl, 0, mla_q_norm[0], wq, rows.all, "mla_q_up")
    kvup = _norm_matmul(hl, MLA_Q_RANK, mla_kv_norm[0], mla_w_ukv[0].astype(BF16), rows.all, "mla_kv_up")
    attn = _mla_attention(qup, kvup, hl, cos_t, sin_t, rows, kpe_col)
    sc = _short_conv(hl, sc_col, sc_width, sc_conv_w[0].astype(F32), rows)
    h = _matmul([attn, sc], even_out_w[0].astype(BF16), n_rows=rows.all, n_cols=d, out_dtype=F32,
                resid=(h, mod4, rows, 0, 2), name="even_out_proj")
    w_gu_bf, w_down_bf = moe_w_gu.astype(BF16), moe_w_down.astype(BF16)
    h = _moe(h, mod4, rows, 0, rows.all, router_w[0], router_b[0], w_gu_bf, moe_b_gu, w_down_bf, moe_b_down,
             final_norm, False)

    lam_init = 0.8 - 0.6 * math.exp(-0.3 * 1)
    w_in1 = jnp.concatenate([odd_in_w[0][:, :DIFF_QK] * (DIFF_HEAD_DIM ** -0.5 * LOG2E), odd_in_w[0][:, DIFF_QK:]],
                            1).astype(BF16)
    u = _modulate(h, mod4, rows, 1, rows.all)
    hl = _matmul([u], w_in1, n_rows=rows.lat, n_cols=w_in1.shape[1], name="odd_in_proj")
    hckv = _matmul([u], w_in1, n_rows=batch * n_ctx, n_cols=DIFF_QK + DIFF_V, a_row0=rows.lat, w_col0=DIFF_QK,
                   name="odd_ctx_kv_proj")
    attn = _diff_attention(hl, hckv, cos_t, sin_t, diff_lambda[0], diff_subln[0], rows, lam_init)

    hy_col = 2 * DIFF_QK + DIFF_V
    hc3 = _hyena_conv3(hl, hy_col, 3 * hy_width, hy_conv_w[0].astype(F32), rows)
    filt = _hyena_filters(seq, hy_width, hy_w1[0], hy_b1[0], hy_w2[0], hy_b2[0], hy_w3[0])
    fwd, inv = _dft_tables(seq)
    spec = _filter_spectrum(fwd, filt)
    z, z_col = hc3, 0
    for o in range(HY_ORDER):
        spec_z = _dft_forward(fwd, z, z_col, spec, o, rows)
        z = _dft_inverse(inv, spec_z, hc3, (1 + o) * hy_width, z, z_col, hy_skip[0], o, rows)
        z_col = 0

    h = _matmul([attn, z], odd_out_w[0].astype(BF16), n_rows=rows.lat, n_cols=d, out_dtype=F32,
                resid=(h, mod4, rows, 1, 2), name="odd_out_proj")
    out = _moe(h, mod4, rows, 1, rows.lat, router_w[1], router_b[1], w_gu_bf, moe_b_gu, w_down_bf, moe_b_down,
               final_norm, True)
    return out.reshape(batch, seq, d)
```

```python
import functools
import math

import jax
import jax.numpy as jnp
from jax import lax
from jax.experimental import pallas as pl
from jax.experimental.pallas import tpu as pltpu

F32 = jnp.float32
BF16 = jnp.bfloat16
U32 = jnp.uint32
I32 = jnp.int32

GRID_W = 64
NORM_EPS = 1e-6
ROPE_THETA = 10000.0

MLA_HEADS = 16
MLA_Q_RANK = 1024
MLA_KV_RANK = 512
MLA_NOPE_DIM = 128
MLA_ROPE_DIM = 64
MLA_V_DIM = 128
MLA_QK_DIM = MLA_NOPE_DIM + MLA_ROPE_DIM
MLA_OUT = MLA_HEADS * MLA_V_DIM

DIFF_HEADS = 16
DIFF_HEAD_DIM = 64
DIFF_QK = DIFF_HEADS * 2 * DIFF_HEAD_DIM
DIFF_V = DIFF_HEADS * 2 * DIFF_HEAD_DIM

HY_ORDER = 2
HY_BANDS = 16
HY_EMB = 1 + 2 * HY_BANDS
HY_FFN = 64
HY_SHIFT = 0.05
HY_MIN_DECAY = math.log(1e-2) / 1.5
HY_MAX_DECAY = math.log(1e-2) / 0.3

N_EXPERTS = 32
TOP_K = 4
SWIGLU_LIMIT = 7.0
SWIGLU_ALPHA = 1.702

LOG2E = 1.4426950408889634
LANES = 128
MOD_ROWS = 16
VMEM_LIMIT = 56 * 1024 * 1024
MOE_BLK = 256
EXPERT_CHUNK = 128
COMBINE_BLK = 128
ATTN_CHUNK = 128
ATTN_HEADS = 2


def _cparams(sem):
    return pltpu.CompilerParams(dimension_semantics=sem, vmem_limit_bytes=VMEM_LIMIT)


def _rms(x):
    return x * lax.rsqrt(jnp.mean(x * x, -1, keepdims=True) + NORM_EPS)


def _split_bf16(a):
    hi = a.astype(BF16)
    lo = (a - hi.astype(F32)).astype(BF16)
    return hi, lo


def _dot3(a, b):
    ah, al = _split_bf16(a)
    bh, bl = _split_bf16(b)
    d = functools.partial(jnp.dot, preferred_element_type=F32)
    return d(ah, bh) + d(al, bh) + d(ah, bl)


def _pack_halves(y):
    c = y.shape[1] // 2
    lo = lax.bitcast_convert_type(y[:, :c].astype(BF16).astype(F32), U32)
    hi = lax.bitcast_convert_type(y[:, c:].astype(BF16).astype(F32), U32)
    return (lo >> 16) | (hi & jnp.uint32(0xFFFF0000))


def _unpack_halves(w):
    lo = lax.bitcast_convert_type(w << 16, F32)
    hi = lax.bitcast_convert_type(w & jnp.uint32(0xFFFF0000), F32)
    return lo, hi


def _ones_column(shape):
    lane = lax.broadcasted_iota(I32, shape, 1)
    return jnp.where(lane == 0, 1.0, 0.0).astype(BF16)


def _rope_pairs(x, cos_t, sin_t):
    lane = lax.broadcasted_iota(I32, x.shape, 1)
    swapped = jnp.where((lane & 63) < 32, pltpu.roll(x, 96, 1), pltpu.roll(x, 32, 1))
    return x * cos_t + swapped * sin_t


def _adaln_kernel(c_ref, w_ref, b_ref, o_ref):
    x = c_ref[...]
    s = (x * jax.nn.sigmoid(x)).astype(BF16)
    o_ref[...] = jnp.dot(s, w_ref[...].astype(BF16), preferred_element_type=F32) + b_ref[...]


def _adaln(cond, ada_w, ada_b):
    depth, d, d6 = ada_w.shape
    tn = 512
    out = pl.pallas_call(
        _adaln_kernel,
        grid=(depth, d6 // tn),
        in_specs=[
            pl.BlockSpec((MOD_ROWS, d), lambda l, j: (0, 0)),
            pl.BlockSpec((None, d, tn), lambda l, j: (l, 0, j)),
            pl.BlockSpec((None, 1, tn), lambda l, j: (l, 0, j)),
        ],
        out_specs=pl.BlockSpec((None, MOD_ROWS, tn), lambda l, j: (l, 0, j)),
        out_shape=jax.ShapeDtypeStruct((depth, MOD_ROWS, d6), F32),
        compiler_params=_cparams(("arbitrary", "arbitrary")),
        name="adaln",
    )(cond, ada_w, ada_b.reshape(depth, 1, d6))
    return out.reshape(depth, MOD_ROWS, 1, d6)


class _Rows:
    def __init__(self, batch, seq, ctx):
        self.batch, self.seq, self.ctx = batch, seq, ctx
        self.lat = batch * seq
        self.all = batch * (seq + ctx)

    def mod_row(self, i, tm):
        r = i * tm
        return jnp.where(r < self.lat, r // self.seq, self.batch)


def _mod_spec(rows, tm, tn, layer, chunk, d, grid_rank, row_axis, col_axis=None):
    per = d // tn

    def imap(*g):
        j = 0 if col_axis is None else g[col_axis]
        return (layer, rows.mod_row(g[row_axis], tm), 0, chunk * per + j)

    return pl.BlockSpec((None, None, 1, tn), imap)


def _modulate_kernel(x_ref, sh_ref, sc_ref, o_ref):
    u = _rms(x_ref[...]) * (1.0 + sc_ref[...]) + sh_ref[...]
    o_ref[...] = u.astype(o_ref.dtype)


def _modulate(h, mod4, rows, layer, n_rows):
    d = h.shape[1]
    tm = 256
    return pl.pallas_call(
        _modulate_kernel,
        grid=(n_rows // tm,),
        in_specs=[
            pl.BlockSpec((tm, d), lambda i: (i, 0)),
            _mod_spec(rows, tm, d, layer, 0, d, 1, 0),
            _mod_spec(rows, tm, d, layer, 1, d, 1, 0),
        ],
        out_specs=pl.BlockSpec((tm, d), lambda i: (i, 0)),
        out_shape=jax.ShapeDtypeStruct((n_rows, d), BF16),
        compiler_params=_cparams(("arbitrary",)),
        name="modulate",
    )(h, mod4, mod4)


def _mm_kernel(*refs, ks, resid):
    a_refs = refs[:len(ks)]
    w_ref = refs[len(ks)]
    o_ref = refs[-1]
    acc = None
    off = 0
    for a_ref, k in zip(a_refs, ks):
        part = jnp.dot(a_ref[...], w_ref[off:off + k, :], preferred_element_type=F32)
        acc = part if acc is None else acc + part
        off += k
    if resid:
        h_ref, g_ref = refs[len(ks) + 1], refs[len(ks) + 2]
        o_ref[...] = h_ref[...] + g_ref[...] * acc
    else:
        o_ref[...] = acc.astype(o_ref.dtype)


def _matmul(a_list, w, *, n_rows, n_cols, a_row0=0, w_col0=0, tm=512, tn=1024, out_dtype=BF16,
            resid=None, name="matmul"):
    tn = min(tn, n_cols)
    tm = min(tm, n_rows)
    ks = tuple(a.shape[1] for a in a_list)
    k_all = sum(ks)
    r0, c0 = a_row0 // tm, w_col0 // tn
    in_specs = [pl.BlockSpec((tm, k), lambda j, i: (r0 + i, 0)) for k in ks]
    in_specs.append(pl.BlockSpec((k_all, tn), lambda j, i: (0, c0 + j)))
    args = list(a_list) + [w]
    if resid is not None:
        h, mod4, rows, layer, chunk = resid
        in_specs.append(pl.BlockSpec((tm, tn), lambda j, i: (i, j)))
        in_specs.append(_mod_spec(rows, tm, tn, layer, chunk, h.shape[1], 2, 1, 0))
        args += [h, mod4]
    return pl.pallas_call(
        functools.partial(_mm_kernel, ks=ks, resid=resid is not None),
        grid=(n_cols // tn, n_rows // tm),
        in_specs=in_specs,
        out_specs=pl.BlockSpec((tm, tn), lambda j, i: (i, j)),
        out_shape=jax.ShapeDtypeStruct((n_rows, n_cols), out_dtype),
        compiler_params=_cparams(("arbitrary", "arbitrary")),
        name=name,
    )(*args)


def _norm_mm_kernel(a_ref, g_ref, w_ref, o_ref):
    y = _rms(a_ref[...].astype(F32)) * g_ref[...]
    o_ref[...] = jnp.dot(y.astype(BF16), w_ref[...], preferred_element_type=F32).astype(o_ref.dtype)


def _norm_matmul(a, a_col0, gain, w, n_rows, name):
    kw, n = w.shape
    tm = 512
    cb = a_col0 // kw
    return pl.pallas_call(
        _norm_mm_kernel,
        grid=(n_rows // tm,),
        in_specs=[
            pl.BlockSpec((tm, kw), lambda i: (i, cb)),
            pl.BlockSpec((1, kw), lambda i: (0, 0)),
            pl.BlockSpec((kw, n), lambda i: (0, 0)),
        ],
        out_specs=pl.BlockSpec((tm, n), lambda i: (i, 0)),
        out_shape=jax.ShapeDtypeStruct((n_rows, n), BF16),
        compiler_params=_cparams(("arbitrary",)),
        name=name,
    )(a, gain.reshape(1, kw).astype(F32), w)


def _mla_attn_kernel(qn_ref, qp_ref, kvl_ref, kvc_ref, kpl_ref, kpc_ref, cq_ref, sq_ref, ck_ref, sk_ref,
                     o_ref, kf_ref, vf_ref, *, seq, ctx, n_lat_blocks):
    qi = pl.program_id(2)

    @pl.when(qi == 0)
    def _build_keys():
        kpe = _rope_pairs(kpl_ref[...].astype(F32), ck_ref[...], sk_ref[...]).astype(BF16)
        kpc = kpc_ref[...]
        ones = _ones_column((seq + ctx, LANES))
        for hh in range(ATTN_HEADS):
            c = 2 * LANES * hh
            kf_ref[hh, 0:seq, 0:LANES] = kvl_ref[:, c:c + LANES]
            kf_ref[hh, seq:seq + ctx, 0:LANES] = kvc_ref[:, c:c + LANES]
            vf_ref[hh, 0:seq, 0:LANES] = kvl_ref[:, c + LANES:c + 2 * LANES]
            vf_ref[hh, seq:seq + ctx, 0:LANES] = kvc_ref[:, c + LANES:c + 2 * LANES]
            vf_ref[hh, :, LANES:2 * LANES] = ones
            kf_ref[hh, 0:seq, LANES:2 * LANES] = kpe
            kf_ref[hh, seq:seq + ctx, LANES:2 * LANES] = kpc

    def attend(hh, q_all, k, v):
        for r in range(0, q_all.shape[0], ATTN_CHUNK):
            q = q_all[r:r + ATTN_CHUNK]
            s = lax.dot_general(q, k, (((1,), (1,)), ((), ())), preferred_element_type=F32)
            e = jnp.exp2(s - jnp.max(s, -1, keepdims=True)).astype(BF16)
            ov = jnp.dot(e, v, preferred_element_type=F32)
            out = (ov[:, :LANES] / ov[:, LANES:LANES + 1]).astype(o_ref.dtype)
            o_ref[r:r + ATTN_CHUNK, hh * LANES:(hh + 1) * LANES] = out

    @pl.when(qi < n_lat_blocks)
    def _latent_queries():
        for hh in range(ATTN_HEADS):
            cols = slice(hh * LANES, (hh + 1) * LANES)
            qpe = _rope_pairs(qp_ref[:, cols].astype(F32), cq_ref[...], sq_ref[...]).astype(BF16)
            q = jnp.concatenate([qn_ref[:, cols], qpe], -1)
            attend(hh, q, kf_ref[hh], vf_ref[hh])

    @pl.when(qi >= n_lat_blocks)
    def _context_queries():
        for hh in range(ATTN_HEADS):
            cols = slice(hh * LANES, (hh + 1) * LANES)
            q = jnp.concatenate([qn_ref[:, cols], qp_ref[:, cols]], -1)
            attend(hh, q, kf_ref[hh, seq:seq + ctx, :], vf_ref[hh, seq:seq + ctx, :])


def _mla_attention(qup, kvup, hl, cos_t, sin_t, rows, kpe_col):
    batch, seq, ctx = rows.batch, rows.seq, rows.ctx
    tq = 256
    nl, nc = seq // tq, ctx // tq
    lat_blocks = rows.lat // tq

    def qrow(b, qi):
        return jnp.where(qi < nl, b * nl + qi, lat_blocks + b * nc + (qi - nl))

    def qtab(b, h, qi):
        return (jnp.minimum(qi, nl - 1), 0)

    kb = kpe_col // LANES
    ctx0 = rows.lat // ctx
    hw = ATTN_HEADS * LANES
    groups = MLA_HEADS // ATTN_HEADS
    kv_scratch = pltpu.VMEM((ATTN_HEADS, seq + ctx, 2 * LANES), BF16)
    return pl.pallas_call(
        functools.partial(_mla_attn_kernel, seq=seq, ctx=ctx, n_lat_blocks=nl),
        grid=(batch, groups, nl + nc),
        in_specs=[
            pl.BlockSpec((tq, hw), lambda b, h, qi: (qrow(b, qi), h)),
            pl.BlockSpec((tq, hw), lambda b, h, qi: (qrow(b, qi), groups + h)),
            pl.BlockSpec((seq, 2 * hw), lambda b, h, qi: (b, h)),
            pl.BlockSpec((ctx, 2 * hw), lambda b, h, qi: (ctx0 + b, h)),
            pl.BlockSpec((seq, LANES), lambda b, h, qi: (b, kb)),
            pl.BlockSpec((ctx, LANES), lambda b, h, qi: (ctx0 + b, kb)),
            pl.BlockSpec((tq, LANES), qtab),
            pl.BlockSpec((tq, LANES), qtab),
            pl.BlockSpec((seq, LANES), lambda b, h, qi: (0, 0)),
            pl.BlockSpec((seq, LANES), lambda b, h, qi: (0, 0)),
        ],
        out_specs=pl.BlockSpec((tq, hw), lambda b, h, qi: (qrow(b, qi), h)),
        out_shape=jax.ShapeDtypeStruct((rows.all, MLA_OUT), BF16),
        scratch_shapes=[kv_scratch, kv_scratch],
        compiler_params=_cparams(("arbitrary", "arbitrary", "arbitrary")),
        name="mla_attn",
    )(qup, qup, kvup, kvup, hl, hl, cos_t, sin_t, cos_t, sin_t)


def _conv3(p, w):
    n = p.shape[0]
    row = lax.broadcasted_iota(I32, p.shape, 0)
    prev = jnp.where(row == 0, 0.0, pltpu.roll(p, 1, 0))
    nxt = jnp.where(row == n - 1, 0.0, pltpu.roll(p, n - 1, 0))
    return prev * w[0:1] + p * w[1:2] + nxt * w[2:3]


HALO = 16


def _sconv_kernel(gb_ref, gc_ref, hh_ref, gcp_ref, hhp_ref, gcn_ref, hhn_ref, w_ref, o_ref, *, seq, ctx, lat):
    i = pl.program_id(0)
    tr = gc_ref.shape[0]
    r0 = i * tr
    length = jnp.where(r0 < lat, seq, ctx)
    pos = jnp.where(r0 < lat, r0, r0 - lat)
    starts = lax.rem(pos, length) == 0
    ends = lax.rem(pos + tr, length) == 0
    p = gc_ref[...].astype(F32) * hh_ref[...].astype(F32)
    before = gcp_ref[HALO - 1:HALO, :].astype(F32) * hhp_ref[HALO - 1:HALO, :].astype(F32)
    after = gcn_ref[0:1, :].astype(F32) * hhn_ref[0:1, :].astype(F32)
    before = jnp.where(starts, 0.0, before)
    after = jnp.where(ends, 0.0, after)
    row = lax.broadcasted_iota(I32, p.shape, 0)
    prev = jnp.where(row == 0, before, pltpu.roll(p, 1, 0))
    nxt = jnp.where(row == tr - 1, after, pltpu.roll(p, tr - 1, 0))
    w = w_ref[...]
    conv = prev * w[0:1] + p * w[1:2] + nxt * w[2:3]
    o_ref[...] = (gb_ref[...].astype(F32) * conv).astype(o_ref.dtype)


def _short_conv(hl, col0, width, conv_w, rows):
    tr, tc = 256, 1024
    assert rows.seq % tr == 0 and rows.ctx % tr == 0
    nct = width // tc
    c0 = col0 // tc
    per = tr // HALO
    last = rows.all // HALO - 1

    def main(k):
        return pl.BlockSpec((tr, tc), lambda i, j: (i, c0 + k * nct + j))

    def before(k):
        return pl.BlockSpec((HALO, tc), lambda i, j: (jnp.maximum(i * per - 1, 0), c0 + k * nct + j))

    def after(k):
        return pl.BlockSpec((HALO, tc), lambda i, j: (jnp.minimum((i + 1) * per, last), c0 + k * nct + j))

    return pl.pallas_call(
        functools.partial(_sconv_kernel, seq=rows.seq, ctx=rows.ctx, lat=rows.lat),
        grid=(rows.all // tr, nct),
        in_specs=[main(0), main(1), main(2), before(1), before(2), after(1), after(2),
                  pl.BlockSpec((3, tc), lambda i, j: (0, j))],
        out_specs=pl.BlockSpec((tr, tc), lambda i, j: (i, j)),
        out_shape=jax.ShapeDtypeStruct((rows.all, width), BF16),
        compiler_params=_cparams(("arbitrary", "arbitrary")),
        name="short_conv",
    )(hl, hl, hl, hl, hl, hl, hl, conv_w)


def _dwconv_kernel(x_ref, w_ref, o_ref):
    o_ref[...] = _conv3(x_ref[...].astype(F32), w_ref[...]).astype(o_ref.dtype)


def _hyena_conv3(hl, col0, width, conv_w, rows):
    tc = 256
    c0 = col0 // tc
    return pl.pallas_call(
        _dwconv_kernel,
        grid=(rows.batch, width // tc),
        in_specs=[
            pl.BlockSpec((rows.seq, tc), lambda s, j: (s, c0 + j)),
            pl.BlockSpec((3, tc), lambda s, j: (0, j)),
        ],
        out_specs=pl.BlockSpec((rows.seq, tc), lambda s, j: (s, j)),
        out_shape=jax.ShapeDtypeStruct((rows.lat, width), BF16),
        compiler_params=_cparams(("arbitrary", "arbitrary")),
        name="hyena_conv3",
    )(hl, conv_w)


def _diff_attn_kernel(q_ref, kl_ref, kc_ref, vl_ref, vc_ref, cq_ref, sq_ref, ck_ref, sk_ref, lam_ref, sub_ref,
                      o_ref, kf_ref, vf_ref, *, seq, ctx, lam_init):
    qi = pl.program_id(2)

    @pl.when(qi == 0)
    def _build_keys():
        ones = _ones_column((seq + ctx, LANES))
        for hh in range(ATTN_HEADS):
            cols = slice(hh * LANES, (hh + 1) * LANES)
            kf_ref[hh, 0:seq, :] = _rope_pairs(kl_ref[:, cols].astype(F32), ck_ref[...], sk_ref[...]).astype(BF16)
            kf_ref[hh, seq:seq + ctx, :] = kc_ref[:, cols]
            vf_ref[hh, 0:seq, 0:LANES] = vl_ref[:, cols]
            vf_ref[hh, seq:seq + ctx, 0:LANES] = vc_ref[:, cols]
            vf_ref[hh, :, LANES:2 * LANES] = ones

    lp = lam_ref[...]
    lam = (jnp.exp(jnp.sum(lp[0:1] * lp[1:2], -1, keepdims=True))
           - jnp.exp(jnp.sum(lp[2:3] * lp[3:4], -1, keepdims=True)) + lam_init)
    for hh in range(ATTN_HEADS):
        cols = slice(hh * LANES, (hh + 1) * LANES)
        q_all = _rope_pairs(q_ref[:, cols].astype(F32), cq_ref[...], sq_ref[...])
        k, v = kf_ref[hh], vf_ref[hh]
        for r in range(0, q_all.shape[0], ATTN_CHUNK):
            q = q_all[r:r + ATTN_CHUNK]
            lane = lax.broadcasted_iota(I32, q.shape, 1)
            first = lane < DIFF_HEAD_DIM
            qq = jnp.concatenate([jnp.where(first, q, 0.0), jnp.where(first, 0.0, q)], 0).astype(BF16)
            s = lax.dot_general(qq, k, (((1,), (1,)), ((), ())), preferred_element_type=F32)
            e = jnp.exp2(s - jnp.max(s, -1, keepdims=True)).astype(BF16)
            ov = jnp.dot(e, v, preferred_element_type=F32)
            on = ov[:, :LANES] / ov[:, LANES:LANES + 1]
            o = on[:ATTN_CHUNK] - lam * on[ATTN_CHUNK:]
            o_ref[r:r + ATTN_CHUNK, cols] = (_rms(o) * sub_ref[...] * (1.0 - lam_init)).astype(o_ref.dtype)


def _diff_attention(hl, hckv, cos_t, sin_t, lam_p, subln, rows, lam_init):
    batch, seq, ctx = rows.batch, rows.seq, rows.ctx
    tq = 512
    nl = seq // tq
    hw = ATTN_HEADS * LANES
    kc0 = DIFF_QK // hw
    vc0 = 2 * DIFF_QK // hw
    lam_pad = jnp.pad(lam_p.astype(F32), ((0, 0), (0, LANES - lam_p.shape[1])))
    return pl.pallas_call(
        functools.partial(_diff_attn_kernel, seq=seq, ctx=ctx, lam_init=lam_init),
        grid=(batch, DIFF_HEADS // ATTN_HEADS, nl),
        in_specs=[
            pl.BlockSpec((tq, hw), lambda b, h, qi: (b * nl + qi, h)),
            pl.BlockSpec((seq, hw), lambda b, h, qi: (b, kc0 + h)),
            pl.BlockSpec((ctx, hw), lambda b, h, qi: (b, h)),
            pl.BlockSpec((seq, hw), lambda b, h, qi: (b, vc0 + h)),
            pl.BlockSpec((ctx, hw), lambda b, h, qi: (b, kc0 + h)),
            pl.BlockSpec((tq, LANES), lambda b, h, qi: (qi, 0)),
            pl.BlockSpec((tq, LANES), lambda b, h, qi: (qi, 0)),
            pl.BlockSpec((seq, LANES), lambda b, h, qi: (0, 0)),
            pl.BlockSpec((seq, LANES), lambda b, h, qi: (0, 0)),
            pl.BlockSpec((4, LANES), lambda b, h, qi: (0, 0)),
            pl.BlockSpec((1, LANES), lambda b, h, qi: (0, 0)),
        ],
        out_specs=pl.BlockSpec((tq, hw), lambda b, h, qi: (b * nl + qi, h)),
        out_shape=jax.ShapeDtypeStruct((rows.lat, DIFF_V), BF16),
        scratch_shapes=[pltpu.VMEM((ATTN_HEADS, seq + ctx, LANES), BF16),
                        pltpu.VMEM((ATTN_HEADS, seq + ctx, 2 * LANES), BF16)],
        compiler_params=_cparams(("arbitrary", "arbitrary", "arbitrary")),
        name="diff_attn",
    )(hl, hl, hckv, hl, hckv, cos_t, sin_t, cos_t, sin_t, lam_pad, subln.reshape(1, LANES).astype(F32))


def _hid_kernel(z_ref, w1_ref, b1_ref, w2_ref, b2_ref, o_ref):
    h1 = jnp.sin(_dot3(z_ref[...], w1_ref[...]) + b1_ref[...])
    o_ref[...] = jnp.sin(_dot3(h1, w2_ref[...]) + b2_ref[...])


def _filt_kernel(hid_ref, w3f_ref, w3b_ref, dec_ref, o_ref, *, n):
    hf = hid_ref[0:n, :]
    hr = hid_ref[n:2 * n, :]
    w3f, w3b = w3f_ref[...], w3b_ref[...]
    kf = _dot3(hf, w3f)
    kb0 = _dot3(hf[0:8], w3b)[0:1]
    kbr = _dot3(hr, w3b)
    dec = dec_ref[...]
    row = lax.broadcasted_iota(I32, kf.shape, 0)
    inv = 1.0 / (n - 1)
    win_f = jnp.exp(-(row.astype(F32) * inv) * dec) + HY_SHIFT
    win_r = jnp.exp(-((n - row).astype(F32) * inv) * dec) + HY_SHIFT
    head = kf * win_f + jnp.where(row == 0, kb0 * (1.0 + HY_SHIFT), 0.0)
    tail = jnp.where(row == 0, 0.0, kbr * win_r)
    nrm = jnp.sum(jnp.abs(head), 0, keepdims=True) + jnp.sum(jnp.abs(tail), 0, keepdims=True)
    o_ref[0:n, :] = (head / nrm).astype(o_ref.dtype)
    o_ref[n:2 * n, :] = (tail / nrm).astype(o_ref.dtype)


def _hyena_filters(n, width, w1, b1, w2, b2, w3):
    t = jnp.linspace(0.0, 1.0, n, dtype=F32)[:, None]
    ang = (2.0 * math.pi / n) * jnp.arange(n, dtype=F32)[:, None] * jnp.linspace(1e-4, HY_BANDS - 1, HY_BANDS, dtype=F32)[None, :]
    z = jnp.concatenate([t, jnp.cos(ang), -jnp.sin(ang)], -1)
    z_rev = jnp.concatenate([z[:1], z[:0:-1]], 0)
    z2 = jnp.pad(jnp.concatenate([z, z_rev], 0), ((0, 0), (0, LANES - HY_EMB)))
    pad = LANES - HY_FFN
    w1p = jnp.pad(w1.astype(F32), ((0, LANES - HY_EMB), (0, pad)))
    b1p = jnp.pad(b1.astype(F32), (0, pad)).reshape(1, LANES)
    w2p = jnp.pad(w2.astype(F32), ((0, pad), (0, pad)))
    b2p = jnp.pad(b2.astype(F32), (0, pad)).reshape(1, LANES)
    w3p = jnp.pad(w3.astype(F32), ((0, pad), (0, 0)))
    th = min(1024, 2 * n)
    hid = pl.pallas_call(
        _hid_kernel,
        grid=(2 * n // th,),
        in_specs=[
            pl.BlockSpec((th, LANES), lambda i: (i, 0)),
            pl.BlockSpec((LANES, LANES), lambda i: (0, 0)),
            pl.BlockSpec((1, LANES), lambda i: (0, 0)),
            pl.BlockSpec((LANES, LANES), lambda i: (0, 0)),
            pl.BlockSpec((1, LANES), lambda i: (0, 0)),
        ],
        out_specs=pl.BlockSpec((th, LANES), lambda i: (i, 0)),
        out_shape=jax.ShapeDtypeStruct((2 * n, LANES), F32),
        compiler_params=_cparams(("arbitrary",)),
        name="hyena_hidden",
    )(z2, w1p, b1p, w2p, b2p)
    decay = jnp.abs(jnp.linspace(HY_MIN_DECAY, HY_MAX_DECAY, width, dtype=F32)).reshape(1, width)
    tn = 256
    per = width // tn
    return pl.pallas_call(
        functools.partial(_filt_kernel, n=n),
        grid=(HY_ORDER, per),
        in_specs=[
            pl.BlockSpec((2 * n, LANES), lambda o, j: (0, 0)),
            pl.BlockSpec((LANES, tn), lambda o, j: (0, (2 * o) * per + j)),
            pl.BlockSpec((LANES, tn), lambda o, j: (0, (2 * o + 1) * per + j)),
            pl.BlockSpec((1, tn), lambda o, j: (0, j)),
        ],
        out_specs=pl.BlockSpec((None, 2 * n, tn), lambda o, j: (o, 0, j)),
        out_shape=jax.ShapeDtypeStruct((HY_ORDER, 2 * n, width), BF16),
        compiler_params=_cparams(("arbitrary", "arbitrary")),
        name="hyena_filters",
    )(hid, w3p, w3p, decay)


def _dft_tables(n):
    p = 2 * n
    k = jnp.arange(n, dtype=I32)[:, None]
    t = jnp.arange(p, dtype=I32)[None, :]
    ang = ((k * t) % p).astype(F32) * (2.0 * math.pi / p)
    alt = jnp.where(t % 2 == 0, 1.0, -1.0)
    fx = jnp.cos(ang)
    fy = jnp.where(k == 0, alt, -jnp.sin(ang))
    fwd = jnp.stack([fx, fy]).astype(BF16)
    tt = jnp.arange(n, dtype=I32)[:, None]
    kk = jnp.arange(n, dtype=I32)[None, :]
    ang2 = ((tt * kk) % p).astype(F32) * (2.0 * math.pi / p)
    alt2 = jnp.where(tt % 2 == 0, 1.0, -1.0)
    gx = jnp.where(kk == 0, 1.0, 2.0 * jnp.cos(ang2)) / p
    gy = jnp.where(kk == 0, alt2, -2.0 * jnp.sin(ang2)) / p
    inv = jnp.stack([gx, gy]).astype(BF16)
    return fwd, inv


def _spec_kernel(fx_ref, fy_ref, f_ref, o_ref):
    f = f_ref[...]
    o_ref[0] = jnp.dot(fx_ref[...], f, preferred_element_type=F32)
    o_ref[1] = jnp.dot(fy_ref[...], f, preferred_element_type=F32)


def _filter_spectrum(fwd, filt):
    _, n, p = fwd.shape
    width = filt.shape[2]
    tm, tn = min(512, n), 512
    return pl.pallas_call(
        _spec_kernel,
        grid=(HY_ORDER, width // tn, n // tm),
        in_specs=[
            pl.BlockSpec((None, tm, p), lambda o, j, i: (0, i, 0)),
            pl.BlockSpec((None, tm, p), lambda o, j, i: (1, i, 0)),
            pl.BlockSpec((None, p, tn), lambda o, j, i: (o, 0, j)),
        ],
        out_specs=pl.BlockSpec((None, 2, tm, tn), lambda o, j, i: (o, 0, i, j)),
        out_shape=jax.ShapeDtypeStruct((HY_ORDER, 2, n, width), F32),
        compiler_params=_cparams(("arbitrary", "arbitrary", "arbitrary")),
        name="filter_spectrum",
    )(fwd, fwd, filt)


def _dft_fwd_kernel(fx_ref, fy_ref, z_ref, hx_ref, hy_ref, o_ref):
    i = pl.program_id(2)
    z = z_ref[...]
    ux = jnp.dot(fx_ref[...], z, preferred_element_type=F32)
    uy = jnp.dot(fy_ref[...], z, preferred_element_type=F32)
    hx, hy = hx_ref[...], hy_ref[...]
    row = lax.broadcasted_iota(I32, ux.shape, 0) + i * ux.shape[0]
    dc = row == 0
    o_ref[0] = (ux * hx - jnp.where(dc, 0.0, uy * hy)).astype(o_ref.dtype)
    o_ref[1] = jnp.where(dc, uy * hy, ux * hy + uy * hx).astype(o_ref.dtype)


def _dft_forward(fwd, z, z_col0, spec, order, rows):
    n = rows.seq
    width = spec.shape[3]
    tm, tn = min(1024, n), 512
    c0 = z_col0 // tn
    return pl.pallas_call(
        _dft_fwd_kernel,
        grid=(rows.batch, width // tn, n // tm),
        in_specs=[
            pl.BlockSpec((None, tm, n), lambda b, j, i: (0, i, 0)),
            pl.BlockSpec((None, tm, n), lambda b, j, i: (1, i, 0)),
            pl.BlockSpec((n, tn), lambda b, j, i: (b, c0 + j)),
            pl.BlockSpec((None, None, tm, tn), lambda b, j, i: (order, 0, i, j)),
            pl.BlockSpec((None, None, tm, tn), lambda b, j, i: (order, 1, i, j)),
        ],
        out_specs=pl.BlockSpec((None, 2, tm, tn), lambda b, j, i: (b, 0, i, j)),
        out_shape=jax.ShapeDtypeStruct((rows.batch, 2, n, width), BF16),
        compiler_params=_cparams(("arbitrary", "arbitrary", "arbitrary")),
        name="hyena_dft_fwd",
    )(fwd, fwd, z, spec, spec)


def _dft_inv_kernel(gx_ref, gy_ref, s_ref, gate_ref, z_ref, skip_ref, o_ref):
    y = (jnp.dot(gx_ref[...], s_ref[0], preferred_element_type=F32)
         + jnp.dot(gy_ref[...], s_ref[1], preferred_element_type=F32))
    z = z_ref[...].astype(F32)
    o_ref[...] = (gate_ref[...].astype(F32) * (y + z * skip_ref[...])).astype(o_ref.dtype)


def _dft_inverse(inv, spec_z, gates, gate_col0, z, z_col0, skip, order, rows):
    n = rows.seq
    width = spec_z.shape[3]
    tm, tn = min(1024, n), 512
    nb = n // tm
    g0, z0 = gate_col0 // tn, z_col0 // tn
    return pl.pallas_call(
        _dft_inv_kernel,
        grid=(rows.batch, width // tn, nb),
        in_specs=[
            pl.BlockSpec((None, tm, n), lambda b, j, i: (0, i, 0)),
            pl.BlockSpec((None, tm, n), lambda b, j, i: (1, i, 0)),
            pl.BlockSpec((None, 2, n, tn), lambda b, j, i: (b, 0, 0, j)),
            pl.BlockSpec((tm, tn), lambda b, j, i: (b * nb + i, g0 + j)),
            pl.BlockSpec((tm, tn), lambda b, j, i: (b * nb + i, z0 + j)),
            pl.BlockSpec((None, 1, tn), lambda b, j, i: (order, 0, j)),
        ],
        out_specs=pl.BlockSpec((tm, tn), lambda b, j, i: (b * nb + i, j)),
        out_shape=jax.ShapeDtypeStruct((rows.lat, width), BF16),
        compiler_params=_cparams(("arbitrary", "arbitrary", "arbitrary")),
        name="hyena_dft_inv",
    )(inv, inv, spec_z, gates, z, skip.reshape(HY_ORDER, 1, width).astype(F32))


def _router_kernel(x_ref, sh_ref, sc_ref, rw_ref, rb_ref, u_ref, ti_ref, tg_ref, rk_ref, cnt_ref, carry_ref):
    i = pl.program_id(0)

    @pl.when(i == 0)
    def _init():
        carry_ref[...] = jnp.zeros_like(carry_ref)

    u = _rms(x_ref[...]) * (1.0 + sc_ref[...]) + sh_ref[...]
    u_ref[...] = _pack_halves(u)
    logits = _dot3(u, rw_ref[...]) + rb_ref[...]
    tm = logits.shape[0]
    lane = lax.broadcasted_iota(I32, logits.shape, 1).astype(F32)
    neg = jnp.float32(-jnp.inf)
    work = jnp.where(lane < N_EXPERTS, logits, neg)
    vals, idxs = [], []
    for _ in range(TOP_K):
        m = jnp.max(work, -1, keepdims=True)
        idx = jnp.min(jnp.where(work == m, lane, float(LANES)), -1, keepdims=True)
        vals.append(m)
        idxs.append(idx)
        work = jnp.where(lane == idx, neg, work)
    es = [jnp.exp(v - vals[0]) for v in vals]
    den = es[0] + es[1] + es[2] + es[3]
    onehot = jnp.zeros(logits.shape, F32)
    for idx in idxs:
        onehot = onehot + (lane == idx).astype(F32)
    r = lax.broadcasted_iota(I32, (tm, tm), 0)
    c = lax.broadcasted_iota(I32, (tm, tm), 1)
    before = (c < r).astype(BF16)
    carry = carry_ref[0:1, :]
    prefix = jnp.dot(before, onehot.astype(BF16), preferred_element_type=F32) + carry
    ti = jnp.zeros(logits.shape, F32)
    tg = jnp.zeros(logits.shape, F32)
    rk = jnp.zeros(logits.shape, F32)
    for k in range(TOP_K):
        pos = jnp.sum(jnp.where(lane == idxs[k], prefix, 0.0), -1, keepdims=True)
        ti = jnp.where(lane == k, idxs[k], ti)
        tg = jnp.where(lane == k, es[k] / den, tg)
        rk = jnp.where(lane == k, pos, rk)
    ti_ref[...] = ti
    tg_ref[...] = tg
    rk_ref[...] = rk
    total = carry + jnp.sum(onehot, 0, keepdims=True)
    carry_ref[...] = jnp.broadcast_to(total, carry_ref.shape)
    cnt_ref[...] = jnp.broadcast_to(total, cnt_ref.shape)


def _router(h, mod4, rows, layer, n_rows, router_w, router_b):
    d = h.shape[1]
    tm = 256
    rw = jnp.pad(router_w.astype(F32), ((0, 0), (0, LANES - N_EXPERTS)))
    rb = jnp.pad(router_b.astype(F32), (0, LANES - N_EXPERTS)).reshape(1, LANES)
    small = lambda dt: jax.ShapeDtypeStruct((n_rows, LANES), dt)
    return pl.pallas_call(
        _router_kernel,
        grid=(n_rows // tm,),
        in_specs=[
            pl.BlockSpec((tm, d), lambda i: (i, 0)),
            _mod_spec(rows, tm, d, layer, 3, d, 1, 0),
            _mod_spec(rows, tm, d, layer, 4, d, 1, 0),
            pl.BlockSpec((d, LANES), lambda i: (0, 0)),
            pl.BlockSpec((1, LANES), lambda i: (0, 0)),
        ],
        out_specs=[
            pl.BlockSpec((tm, d // 2), lambda i: (i, 0)),
            pl.BlockSpec((tm, LANES), lambda i: (i, 0)),
            pl.BlockSpec((tm, LANES), lambda i: (i, 0)),
            pl.BlockSpec((tm, LANES), lambda i: (i, 0)),
            pl.BlockSpec((8, LANES), lambda i: (0, 0)),
        ],
        out_shape=[jax.ShapeDtypeStruct((n_rows, d // 2), U32), small(F32), small(F32), small(F32),
                   jax.ShapeDtypeStruct((8, LANES), F32)],
        scratch_shapes=[pltpu.VMEM((8, LANES), F32)],
        compiler_params=_cparams(("arbitrary",)),
        name="router",
    )(h, mod4, mod4, rw, rb)


def _row_copy(src_ref, src_row, dst_ref, dst_row, sem):
    return pltpu.make_async_copy(src_ref.at[pl.ds(src_row, 1)], dst_ref.at[pl.ds(dst_row, 1)], sem)


def _block_copy(src_ref, dst_ref, sem):
    return pltpu.make_async_copy(src_ref.at[pl.ds(0, dst_ref.shape[0])], dst_ref, sem)


def _expert_kernel(be_ref, used_ref, tok_ref, tok_next_ref, src_ref, wgu_ref, bgu_ref, wd_ref, bd_ref, o_ref,
                   xbuf, sems):
    i = pl.program_id(0)
    slot = lax.rem(i, 2)
    n_used = used_ref[0]

    def gather(ids_ref, s):
        for r in range(MOE_BLK):
            _row_copy(src_ref, ids_ref[0, 0, r], xbuf.at[s], r, sems.at[s]).start()

    @pl.when(i == 0)
    def _first():
        gather(tok_ref, 0)

    @pl.when(i + 1 < n_used)
    def _ahead():
        gather(tok_next_ref, 1 - slot)

    @pl.when(i < n_used)
    def _compute():
        _block_copy(src_ref, xbuf.at[slot], sems.at[slot]).wait()
        dot = functools.partial(jnp.dot, preferred_element_type=F32)
        for r in range(0, MOE_BLK, EXPERT_CHUNK):
            lo, hi = _unpack_halves(xbuf[slot, r:r + EXPERT_CHUNK, :])
            half = lo.shape[1]
            gu = (dot(lo.astype(BF16), wgu_ref[0:half, :]) + dot(hi.astype(BF16), wgu_ref[half:2 * half, :])
                  + bgu_ref[...])
            de = gu.shape[1] // 2
            g = jnp.minimum(gu[:, :de], SWIGLU_LIMIT)
            up = jnp.clip(gu[:, de:], -SWIGLU_LIMIT, SWIGLU_LIMIT)
            a = (up + 1.0) * g * jax.nn.sigmoid(SWIGLU_ALPHA * g)
            y = dot(a.astype(BF16), wd_ref[...]) + bd_ref[...]
            o_ref[r:r + EXPERT_CHUNK, :] = _pack_halves(y)

    @pl.when(i >= n_used)
    def _empty():
        o_ref[...] = jnp.zeros(o_ref.shape, o_ref.dtype)


def _experts(u_packed, slot_tok, block_e, n_used, layer, w_gu, b_gu, w_down, b_down):
    _, n_exp, d, de2 = w_gu.shape
    de = de2 // 2
    n_blocks = block_e.shape[0]
    tok = slot_tok.reshape(n_blocks, 1, MOE_BLK)
    grid_spec = pltpu.PrefetchScalarGridSpec(
        num_scalar_prefetch=2,
        grid=(n_blocks,),
        in_specs=[
            pl.BlockSpec((1, 1, MOE_BLK), lambda i, be, nu: (i, 0, 0), memory_space=pltpu.SMEM),
            pl.BlockSpec((1, 1, MOE_BLK), lambda i, be, nu: (jnp.minimum(i + 1, n_blocks - 1), 0, 0),
                         memory_space=pltpu.SMEM),
            pl.BlockSpec(memory_space=pl.ANY),
            pl.BlockSpec((None, None, d, de2), lambda i, be, nu: (layer, be[i], 0, 0)),
            pl.BlockSpec((None, None, 1, de2), lambda i, be, nu: (layer, be[i], 0, 0)),
            pl.BlockSpec((None, None, de, d), lambda i, be, nu: (layer, be[i], 0, 0)),
            pl.BlockSpec((None, None, 1, d), lambda i, be, nu: (layer, be[i], 0, 0)),
        ],
        out_specs=pl.BlockSpec((MOE_BLK, d // 2), lambda i, be, nu: (i, 0)),
        scratch_shapes=[pltpu.VMEM((2, MOE_BLK, d // 2), U32), pltpu.SemaphoreType.DMA((2,))],
    )
    return pl.pallas_call(
        _expert_kernel,
        grid_spec=grid_spec,
        out_shape=jax.ShapeDtypeStruct((n_blocks * MOE_BLK, d // 2), U32),
        compiler_params=_cparams(("arbitrary",)),
        name="moe_experts",
    )(block_e, n_used, tok, tok, u_packed, w_gu, b_gu.reshape(-1, n_exp, 1, de2).astype(F32), w_down,
      b_down.reshape(-1, n_exp, 1, d).astype(F32))


def _combine_kernel(dest_ref, dest_next_ref, gate_ref, h_ref, g2_ref, fn_ref, yb_ref, o_ref, buf, sems, *, final):
    n = h_ref.shape[0]
    i = pl.program_id(0)
    slot = lax.rem(i, 2)

    def gather(ids_ref, s):
        for r in range(n):
            for k in range(TOP_K):
                _row_copy(yb_ref, ids_ref[0, 0, r * TOP_K + k], buf.at[s], k * n + r, sems.at[s]).start()

    @pl.when(i == 0)
    def _first():
        gather(dest_ref, 0)

    @pl.when(i + 1 < pl.num_programs(0))
    def _ahead():
        gather(dest_next_ref, 1 - slot)

    _block_copy(yb_ref, buf.at[slot], sems.at[slot]).wait()
    gates = gate_ref[...]
    acc_lo = None
    acc_hi = None
    for k in range(TOP_K):
        lo, hi = _unpack_halves(buf[slot, k * n:(k + 1) * n, :])
        gk = gates[:, k:k + 1]
        acc_lo = gk * lo if acc_lo is None else acc_lo + gk * lo
        acc_hi = gk * hi if acc_hi is None else acc_hi + gk * hi
    out = h_ref[...] + g2_ref[...] * jnp.concatenate([acc_lo, acc_hi], -1)
    if final:
        out = _rms(out) * fn_ref[...]
    o_ref[...] = out


def _combine(dest, gates, h, mod4, rows, layer, n_rows, yb, final_gain, final):
    d = h.shape[1]
    tm = COMBINE_BLK
    nb = n_rows // tm
    dest3 = dest.reshape(nb, 1, tm * TOP_K)
    return pl.pallas_call(
        functools.partial(_combine_kernel, final=final),
        grid=(nb,),
        in_specs=[
            pl.BlockSpec((1, 1, tm * TOP_K), lambda i: (i, 0, 0), memory_space=pltpu.SMEM),
            pl.BlockSpec((1, 1, tm * TOP_K), lambda i: (jnp.minimum(i + 1, nb - 1), 0, 0), memory_space=pltpu.SMEM),
            pl.BlockSpec((tm, LANES), lambda i: (i, 0)),
            pl.BlockSpec((tm, d), lambda i: (i, 0)),
            _mod_spec(rows, tm, d, layer, 5, d, 1, 0),
            pl.BlockSpec((1, d), lambda i: (0, 0)),
            pl.BlockSpec(memory_space=pl.ANY),
        ],
        out_specs=pl.BlockSpec((tm, d), lambda i: (i, 0)),
        out_shape=jax.ShapeDtypeStruct((n_rows, d), F32),
        scratch_shapes=[pltpu.VMEM((2, TOP_K * tm, d // 2), U32), pltpu.SemaphoreType.DMA((2,))],
        compiler_params=_cparams(("arbitrary",)),
        name="moe_combine",
    )(dest3, dest3, gates, h, mod4, final_gain.reshape(1, d).astype(F32), yb)


def _moe(h, mod4, rows, layer, n_rows, router_w, router_b, w_gu, b_gu, w_down, b_down, final_gain, final):
    u_packed, ti, tg, rk, cnt = _router(h, mod4, rows, layer, n_rows, router_w, router_b)
    counts = cnt[0, :N_EXPERTS].astype(I32)
    padded = (counts + MOE_BLK - 1) // MOE_BLK * MOE_BLK
    pad_end = jnp.cumsum(padded)
    pad_start = pad_end - padded
    top_i = ti[:, :TOP_K].astype(I32)
    dest = pad_start[top_i] + rk[:, :TOP_K].astype(I32)
    n_blocks = n_rows * TOP_K // MOE_BLK + N_EXPERTS
    tok = jnp.repeat(jnp.arange(n_rows, dtype=I32), TOP_K)
    slot_tok = jnp.zeros((n_blocks * MOE_BLK,), I32).at[dest.reshape(-1)].set(tok)
    block_start = jnp.arange(n_blocks, dtype=I32)[:, None] * MOE_BLK
    block_e = jnp.minimum(jnp.sum((pad_end[None, :] <= block_start).astype(I32), 1), N_EXPERTS - 1)
    n_used = (pad_end[-1:] // MOE_BLK).astype(I32)
    yb = _experts(u_packed, slot_tok, block_e, n_used, layer, w_gu, b_gu, w_down, b_down)
    return _combine(dest, tg, h, mod4, rows, layer, n_rows, yb, final_gain, final)


def _rope_tables(seq):
    n_rows = seq // GRID_W
    row = jnp.repeat(jnp.arange(n_rows), GRID_W).astype(F32)
    col = jnp.tile(jnp.arange(GRID_W), n_rows).astype(F32)
    quarter = MLA_ROPE_DIM // 4
    inv = 1.0 / (ROPE_THETA ** (jnp.arange(quarter, dtype=F32) / quarter))
    ang = jnp.concatenate([row[:, None] * inv, col[:, None] * inv], -1)
    cos, sin = jnp.cos(ang), jnp.sin(ang)
    return jnp.tile(cos, (1, 4)), jnp.tile(jnp.concatenate([-sin, sin], -1), (1, 2))


def kernel(x, c, ctx, c_ctx, ada_w, ada_b, mla_in_w, mla_q_norm, mla_kv_norm, mla_w_uq, mla_w_ukv, sc_conv_w, even_out_w, odd_in_w, diff_lambda, diff_subln, hy_conv_w, hy_w1, hy_b1, hy_w2, hy_b2, hy_w3, hy_skip, odd_out_w, router_w, router_b, moe_w_gu, moe_b_gu, moe_w_down, moe_b_down, final_norm):
    batch, seq, d = x.shape
    n_ctx = ctx.shape[1]
    depth = ada_w.shape[0]
    assert depth == 2 and batch + 1 <= MOD_ROWS
    rows = _Rows(batch, seq, n_ctx)
    sc_width = d - MLA_OUT
    hy_width = d - DIFF_V

    cond = jnp.zeros((MOD_ROWS, d), F32).at[:batch].set(c).at[batch].set(c_ctx)
    mod4 = _adaln(cond, ada_w, ada_b)
    cos_t, sin_t = _rope_tables(seq)
    h = jnp.concatenate([x.reshape(rows.lat, d), ctx.reshape(batch * n_ctx, d)], 0)

    kpe_col = MLA_Q_RANK + MLA_KV_RANK
    sc_col = 2048
    w_in = mla_in_w[0]
    w_in = jnp.concatenate([
        w_in[:, :kpe_col + MLA_ROPE_DIM],
        jnp.zeros((d, sc_col - kpe_col - MLA_ROPE_DIM), w_in.dtype),
        w_in[:, kpe_col + MLA_ROPE_DIM:],
    ], 1).astype(BF16)
    wq = (mla_w_uq[0] * (MLA_QK_DIM ** -0.5 * LOG2E)).reshape(MLA_Q_RANK, MLA_HEADS, MLA_QK_DIM)
    wq_pe = jnp.pad(wq[:, :, MLA_NOPE_DIM:], ((0, 0), (0, 0), (0, LANES - MLA_ROPE_DIM)))
    wq = jnp.concatenate([wq[:, :, :MLA_NOPE_DIM].reshape(MLA_Q_RANK, -1), wq_pe.reshape(MLA_Q_RANK, -1)], 1).astype(BF16)

    u = _modulate(h, mod4, rows, 0, rows.all)
    hl = _matmul([u], w_in, n_rows=rows.all, n_cols=w_in.shape[1], name="even_in_proj")
    qup = _norm_matmul(hl, 0, mla_q_norm[0], wq, rows.all, "mla_q_up")
    kvup = _norm_matmul(hl, MLA_Q_RANK, mla_kv_norm[0], mla_w_ukv[0].astype(BF16), rows.all, "mla_kv_up")
    attn = _mla_attention(qup, kvup, hl, cos_t, sin_t, rows, kpe_col)
    sc = _short_conv(hl, sc_col, sc_width, sc_conv_w[0].astype(F32), rows)
    h = _matmul([attn, sc], even_out_w[0].astype(BF16), n_rows=rows.all, n_cols=d, out_dtype=F32,
                resid=(h, mod4, rows, 0, 2), name="even_out_proj")
    w_gu_bf, w_down_bf = moe_w_gu.astype(BF16), moe_w_down.astype(BF16)
    h = _moe(h, mod4, rows, 0, rows.all, router_w[0], router_b[0], w_gu_bf, moe_b_gu, w_down_bf, moe_b_down,
             final_norm, False)

    lam_init = 0.8 - 0.6 * math.exp(-0.3 * 1)
    w_in1 = jnp.concatenate([odd_in_w[0][:, :DIFF_QK] * (DIFF_HEAD_DIM ** -0.5 * LOG2E), odd_in_w[0][:, DIFF_QK:]],
                            1).astype(BF16)
    u = _modulate(h, mod4, rows, 1, rows.all)
    hl = _matmul([u], w_in1, n_rows=rows.lat, n_cols=w_in1.shape[1], name="odd_in_proj")
    hckv = _matmul([u], w_in1, n_rows=batch * n_ctx, n_cols=DIFF_QK + DIFF_V, a_row0=rows.lat, w_col0=DIFF_QK,
                   name="odd_ctx_kv_proj")
    attn = _diff_attention(hl, hckv, cos_t, sin_t, diff_lambda[0], diff_subln[0], rows, lam_init)

    hy_col = 2 * DIFF_QK + DIFF_V
    hc3 = _hyena_conv3(hl, hy_col, 3 * hy_width, hy_conv_w[0].astype(F32), rows)
    filt = _hyena_filters(seq, hy_width, hy_w1[0], hy_b1[0], hy_w2[0], hy_b2[0], hy_w3[0])
    fwd, inv = _dft_tables(seq)
    spec = _filter_spectrum(fwd, filt)
    z, z_col = hc3, 0
    for o in range(HY_ORDER):
        spec_z = _dft_forward(fwd, z, z_col, spec, o, rows)
        z = _dft_inverse(inv, spec_z, hc3, (1 + o) * hy_width, z, z_col, hy_skip[0], o, rows)
        z_col = 0

    h = _matmul([attn, z], odd_out_w[0].astype(BF16), n_rows=rows.lat, n_cols=d, out_dtype=F32,
                resid=(h, mod4, rows, 1, 2), name="odd_out_proj")
    out = _moe(h, mod4, rows, 1, rows.lat, router_w[1], router_b[1], w_gu_bf, moe_b_gu, w_down_bf, moe_b_down,
               final_norm, True)
    return out.reshape(batch, seq, d)
```

```python
import functools
import math

import jax
import jax.numpy as jnp
from jax import lax
from jax.experimental import pallas as pl
from jax.experimental.pallas import tpu as pltpu

F32 = jnp.float32
BF16 = jnp.bfloat16
U32 = jnp.uint32
I32 = jnp.int32

GRID_W = 64
NORM_EPS = 1e-6
ROPE_THETA = 10000.0

MLA_HEADS = 16
MLA_Q_RANK = 1024
MLA_KV_RANK = 512
MLA_NOPE_DIM = 128
MLA_ROPE_DIM = 64
MLA_V_DIM = 128
MLA_QK_DIM = MLA_NOPE_DIM + MLA_ROPE_DIM
MLA_OUT = MLA_HEADS * MLA_V_DIM

DIFF_HEADS = 16
DIFF_HEAD_DIM = 64
DIFF_QK = DIFF_HEADS * 2 * DIFF_HEAD_DIM
DIFF_V = DIFF_HEADS * 2 * DIFF_HEAD_DIM

HY_ORDER = 2
HY_BANDS = 16
HY_EMB = 1 + 2 * HY_BANDS
HY_FFN = 64
HY_SHIFT = 0.05
HY_MIN_DECAY = math.log(1e-2) / 1.5
HY_MAX_DECAY = math.log(1e-2) / 0.3

N_EXPERTS = 32
TOP_K = 4
SWIGLU_LIMIT = 7.0
SWIGLU_ALPHA = 1.702

LOG2E = 1.4426950408889634
LANES = 128
MOD_ROWS = 16
VMEM_LIMIT = 56 * 1024 * 1024
MOE_BLK = 256
EXPERT_CHUNK = 128
COMBINE_BLK = 128
ATTN_CHUNK = 128
ATTN_HEADS = 2
ATTN_AHEAD = 1


def _cparams(sem):
    return pltpu.CompilerParams(dimension_semantics=sem, vmem_limit_bytes=VMEM_LIMIT)


def _rms(x):
    return x * lax.rsqrt(jnp.mean(x * x, -1, keepdims=True) + NORM_EPS)


def _split_bf16(a):
    hi = a.astype(BF16)
    lo = (a - hi.astype(F32)).astype(BF16)
    return hi, lo


def _dot3(a, b):
    ah, al = _split_bf16(a)
    bh, bl = _split_bf16(b)
    d = functools.partial(jnp.dot, preferred_element_type=F32)
    return d(ah, bh) + d(al, bh) + d(ah, bl)


def _pack_halves(y):
    c = y.shape[1] // 2
    lo = lax.bitcast_convert_type(y[:, :c].astype(BF16).astype(F32), U32)
    hi = lax.bitcast_convert_type(y[:, c:].astype(BF16).astype(F32), U32)
    return (lo >> 16) | (hi & jnp.uint32(0xFFFF0000))


def _unpack_halves(w):
    lo = lax.bitcast_convert_type(w << 16, F32)
    hi = lax.bitcast_convert_type(w & jnp.uint32(0xFFFF0000), F32)
    return lo, hi


def _ones_column(shape):
    lane = lax.broadcasted_iota(I32, shape, 1)
    return jnp.where(lane == 0, 1.0, 0.0).astype(BF16)


def _rope_pairs(x, cos_t, sin_t):
    lane = lax.broadcasted_iota(I32, x.shape, 1)
    swapped = jnp.where((lane & 63) < 32, pltpu.roll(x, 96, 1), pltpu.roll(x, 32, 1))
    return x * cos_t + swapped * sin_t


def _adaln_kernel(c_ref, w_ref, b_ref, o_ref):
    x = c_ref[...]
    s = (x * jax.nn.sigmoid(x)).astype(BF16)
    o_ref[...] = jnp.dot(s, w_ref[...].astype(BF16), preferred_element_type=F32) + b_ref[...]


def _adaln(cond, ada_w, ada_b):
    depth, d, d6 = ada_w.shape
    tn = 512
    out = pl.pallas_call(
        _adaln_kernel,
        grid=(depth, d6 // tn),
        in_specs=[
            pl.BlockSpec((MOD_ROWS, d), lambda l, j: (0, 0)),
            pl.BlockSpec((None, d, tn), lambda l, j: (l, 0, j)),
            pl.BlockSpec((None, 1, tn), lambda l, j: (l, 0, j)),
        ],
        out_specs=pl.BlockSpec((None, MOD_ROWS, tn), lambda l, j: (l, 0, j)),
        out_shape=jax.ShapeDtypeStruct((depth, MOD_ROWS, d6), F32),
        compiler_params=_cparams(("arbitrary", "arbitrary")),
        name="adaln",
    )(cond, ada_w, ada_b.reshape(depth, 1, d6))
    return out.reshape(depth, MOD_ROWS, 1, d6)


class _Rows:
    def __init__(self, batch, seq, ctx):
        self.batch, self.seq, self.ctx = batch, seq, ctx
        self.lat = batch * seq
        self.all = batch * (seq + ctx)

    def mod_row(self, i, tm):
        r = i * tm
        return jnp.where(r < self.lat, r // self.seq, self.batch)


def _mod_spec(rows, tm, tn, layer, chunk, d, grid_rank, row_axis, col_axis=None):
    per = d // tn

    def imap(*g):
        j = 0 if col_axis is None else g[col_axis]
        return (layer, rows.mod_row(g[row_axis], tm), 0, chunk * per + j)

    return pl.BlockSpec((None, None, 1, tn), imap)


def _modulate_kernel(x_ref, sh_ref, sc_ref, o_ref):
    u = _rms(x_ref[...]) * (1.0 + sc_ref[...]) + sh_ref[...]
    o_ref[...] = u.astype(o_ref.dtype)


def _modulate(h, mod4, rows, layer, n_rows):
    d = h.shape[1]
    tm = 256
    return pl.pallas_call(
        _modulate_kernel,
        grid=(n_rows // tm,),
        in_specs=[
            pl.BlockSpec((tm, d), lambda i: (i, 0)),
            _mod_spec(rows, tm, d, layer, 0, d, 1, 0),
            _mod_spec(rows, tm, d, layer, 1, d, 1, 0),
        ],
        out_specs=pl.BlockSpec((tm, d), lambda i: (i, 0)),
        out_shape=jax.ShapeDtypeStruct((n_rows, d), BF16),
        compiler_params=_cparams(("arbitrary",)),
        name="modulate",
    )(h, mod4, mod4)


def _mm_kernel(*refs, ks, resid):
    a_refs = refs[:len(ks)]
    w_ref = refs[len(ks)]
    o_ref = refs[-1]
    acc = None
    off = 0
    for a_ref, k in zip(a_refs, ks):
        part = jnp.dot(a_ref[...], w_ref[off:off + k, :], preferred_element_type=F32)
        acc = part if acc is None else acc + part
        off += k
    if resid:
        h_ref, g_ref = refs[len(ks) + 1], refs[len(ks) + 2]
        o_ref[...] = h_ref[...] + g_ref[...] * acc
    else:
        o_ref[...] = acc.astype(o_ref.dtype)


def _matmul(a_list, w, *, n_rows, n_cols, a_row0=0, w_col0=0, tm=512, tn=1024, out_dtype=BF16,
            resid=None, name="matmul"):
    tn = min(tn, n_cols)
    tm = min(tm, n_rows)
    ks = tuple(a.shape[1] for a in a_list)
    k_all = sum(ks)
    r0, c0 = a_row0 // tm, w_col0 // tn
    in_specs = [pl.BlockSpec((tm, k), lambda j, i: (r0 + i, 0)) for k in ks]
    in_specs.append(pl.BlockSpec((k_all, tn), lambda j, i: (0, c0 + j)))
    args = list(a_list) + [w]
    if resid is not None:
        h, mod4, rows, layer, chunk = resid
        in_specs.append(pl.BlockSpec((tm, tn), lambda j, i: (i, j)))
        in_specs.append(_mod_spec(rows, tm, tn, layer, chunk, h.shape[1], 2, 1, 0))
        args += [h, mod4]
    return pl.pallas_call(
        functools.partial(_mm_kernel, ks=ks, resid=resid is not None),
        grid=(n_cols // tn, n_rows // tm),
        in_specs=in_specs,
        out_specs=pl.BlockSpec((tm, tn), lambda j, i: (i, j)),
        out_shape=jax.ShapeDtypeStruct((n_rows, n_cols), out_dtype),
        compiler_params=_cparams(("arbitrary", "arbitrary")),
        name=name,
    )(*args)


def _norm_mm_kernel(a_ref, g_ref, w_ref, o_ref):
    y = _rms(a_ref[...].astype(F32)) * g_ref[...]
    o_ref[...] = jnp.dot(y.astype(BF16), w_ref[...], preferred_element_type=F32).astype(o_ref.dtype)


def _norm_matmul(a, a_col0, gain, w, n_rows, name):
    kw, n = w.shape
    tm = 512
    cb = a_col0 // kw
    return pl.pallas_call(
        _norm_mm_kernel,
        grid=(n_rows // tm,),
        in_specs=[
            pl.BlockSpec((tm, kw), lambda i: (i, cb)),
            pl.BlockSpec((1, kw), lambda i: (0, 0)),
            pl.BlockSpec((kw, n), lambda i: (0, 0)),
        ],
        out_specs=pl.BlockSpec((tm, n), lambda i: (i, 0)),
        out_shape=jax.ShapeDtypeStruct((n_rows, n), BF16),
        compiler_params=_cparams(("arbitrary",)),
        name=name,
    )(a, gain.reshape(1, kw).astype(F32), w)


def _mla_attn_kernel(qn_ref, qp_ref, kvl_ref, kvc_ref, kpl_ref, kpc_ref, cq_ref, sq_ref, ck_ref, sk_ref,
                     o_ref, kf_ref, vf_ref, *, seq, ctx, n_lat_blocks):
    qi = pl.program_id(2)

    @pl.when(qi == 0)
    def _build_keys():
        kpe = _rope_pairs(kpl_ref[...].astype(F32), ck_ref[...], sk_ref[...]).astype(BF16)
        kpc = kpc_ref[...]
        ones = _ones_column((seq + ctx, LANES))
        for hh in range(ATTN_HEADS):
            c = 2 * LANES * hh
            kf_ref[hh, 0:seq, 0:LANES] = kvl_ref[:, c:c + LANES]
            kf_ref[hh, seq:seq + ctx, 0:LANES] = kvc_ref[:, c:c + LANES]
            vf_ref[hh, 0:seq, 0:LANES] = kvl_ref[:, c + LANES:c + 2 * LANES]
            vf_ref[hh, seq:seq + ctx, 0:LANES] = kvc_ref[:, c + LANES:c + 2 * LANES]
            vf_ref[hh, :, LANES:2 * LANES] = ones
            kf_ref[hh, 0:seq, LANES:2 * LANES] = kpe
            kf_ref[hh, seq:seq + ctx, LANES:2 * LANES] = kpc

    def attend(chains):
        def scores(hh, q, k, v):
            return lax.dot_general(q, k, (((1,), (1,)), ((), ())), preferred_element_type=F32)

        def finish(hh, r, s, v):
            e = jnp.exp2(s - jnp.max(s, -1, keepdims=True)).astype(BF16)
            ov = jnp.dot(e, v, preferred_element_type=F32)
            out = (ov[:, :LANES] / ov[:, LANES:LANES + 1]).astype(o_ref.dtype)
            o_ref[r:r + ATTN_CHUNK, hh * LANES:(hh + 1) * LANES] = out

        units = [(hh, r, q[r:r + ATTN_CHUNK], k, v) for hh, q, k, v in chains
                 for r in range(0, q.shape[0], ATTN_CHUNK)]
        pending = []
        for hh, r, q, k, v in units:
            pending.append((hh, r, scores(hh, q, k, v), v))
            if len(pending) > ATTN_AHEAD:
                finish(*pending.pop(0))
        for p in pending:
            finish(*p)

    @pl.when(qi < n_lat_blocks)
    def _latent_queries():
        chains = []
        for hh in range(ATTN_HEADS):
            cols = slice(hh * LANES, (hh + 1) * LANES)
            qpe = _rope_pairs(qp_ref[:, cols].astype(F32), cq_ref[...], sq_ref[...]).astype(BF16)
            chains.append((hh, jnp.concatenate([qn_ref[:, cols], qpe], -1), kf_ref[hh], vf_ref[hh]))
        attend(chains)

    @pl.when(qi >= n_lat_blocks)
    def _context_queries():
        chains = []
        for hh in range(ATTN_HEADS):
            cols = slice(hh * LANES, (hh + 1) * LANES)
            q = jnp.concatenate([qn_ref[:, cols], qp_ref[:, cols]], -1)
            chains.append((hh, q, kf_ref[hh, seq:seq + ctx, :], vf_ref[hh, seq:seq + ctx, :]))
        attend(chains)


def _mla_attention(qup, kvup, hl, cos_t, sin_t, rows, kpe_col):
    batch, seq, ctx = rows.batch, rows.seq, rows.ctx
    tq = 256
    nl, nc = seq // tq, ctx // tq
    lat_blocks = rows.lat // tq

    def qrow(b, qi):
        return jnp.where(qi < nl, b * nl + qi, lat_blocks + b * nc + (qi - nl))

    def qtab(b, h, qi):
        return (jnp.minimum(qi, nl - 1), 0)

    kb = kpe_col // LANES
    ctx0 = rows.lat // ctx
    hw = ATTN_HEADS * LANES
    groups = MLA_HEADS // ATTN_HEADS
    kv_scratch = pltpu.VMEM((ATTN_HEADS, seq + ctx, 2 * LANES), BF16)
    return pl.pallas_call(
        functools.partial(_mla_attn_kernel, seq=seq, ctx=ctx, n_lat_blocks=nl),
        grid=(batch, groups, nl + nc),
        in_specs=[
            pl.BlockSpec((tq, hw), lambda b, h, qi: (qrow(b, qi), h)),
            pl.BlockSpec((tq, hw), lambda b, h, qi: (qrow(b, qi), groups + h)),
            pl.BlockSpec((seq, 2 * hw), lambda b, h, qi: (b, h)),
            pl.BlockSpec((ctx, 2 * hw), lambda b, h, qi: (ctx0 + b, h)),
            pl.BlockSpec((seq, LANES), lambda b, h, qi: (b, kb)),
            pl.BlockSpec((ctx, LANES), lambda b, h, qi: (ctx0 + b, kb)),
            pl.BlockSpec((tq, LANES), qtab),
            pl.BlockSpec((tq, LANES), qtab),
            pl.BlockSpec((seq, LANES), lambda b, h, qi: (0, 0)),
            pl.BlockSpec((seq, LANES), lambda b, h, qi: (0, 0)),
        ],
        out_specs=pl.BlockSpec((tq, hw), lambda b, h, qi: (qrow(b, qi), h)),
        out_shape=jax.ShapeDtypeStruct((rows.all, MLA_OUT), BF16),
        scratch_shapes=[kv_scratch, kv_scratch],
        compiler_params=_cparams(("arbitrary", "arbitrary", "arbitrary")),
        name="mla_attn",
    )(qup, qup, kvup, kvup, hl, hl, cos_t, sin_t, cos_t, sin_t)


def _conv3(p, w):
    n = p.shape[0]
    row = lax.broadcasted_iota(I32, p.shape, 0)
    prev = jnp.where(row == 0, 0.0, pltpu.roll(p, 1, 0))
    nxt = jnp.where(row == n - 1, 0.0, pltpu.roll(p, n - 1, 0))
    return prev * w[0:1] + p * w[1:2] + nxt * w[2:3]


HALO = 16


def _sconv_kernel(gb_ref, gc_ref, hh_ref, gcp_ref, hhp_ref, gcn_ref, hhn_ref, w_ref, o_ref, *, seq, ctx, lat):
    i = pl.program_id(0)
    tr = gc_ref.shape[0]
    r0 = i * tr
    length = jnp.where(r0 < lat, seq, ctx)
    pos = jnp.where(r0 < lat, r0, r0 - lat)
    starts = lax.rem(pos, length) == 0
    ends = lax.rem(pos + tr, length) == 0
    p = gc_ref[...].astype(F32) * hh_ref[...].astype(F32)
    before = gcp_ref[HALO - 1:HALO, :].astype(F32) * hhp_ref[HALO - 1:HALO, :].astype(F32)
    after = gcn_ref[0:1, :].astype(F32) * hhn_ref[0:1, :].astype(F32)
    before = jnp.where(starts, 0.0, before)
    after = jnp.where(ends, 0.0, after)
    row = lax.broadcasted_iota(I32, p.shape, 0)
    prev = jnp.where(row == 0, before, pltpu.roll(p, 1, 0))
    nxt = jnp.where(row == tr - 1, after, pltpu.roll(p, tr - 1, 0))
    w = w_ref[...]
    conv = prev * w[0:1] + p * w[1:2] + nxt * w[2:3]
    o_ref[...] = (gb_ref[...].astype(F32) * conv).astype(o_ref.dtype)


def _short_conv(hl, col0, width, conv_w, rows):
    tr, tc = 256, 1024
    assert rows.seq % tr == 0 and rows.ctx % tr == 0
    nct = width // tc
    c0 = col0 // tc
    per = tr // HALO
    last = rows.all // HALO - 1

    def main(k):
        return pl.BlockSpec((tr, tc), lambda i, j: (i, c0 + k * nct + j))

    def before(k):
        return pl.BlockSpec((HALO, tc), lambda i, j: (jnp.maximum(i * per - 1, 0), c0 + k * nct + j))

    def after(k):
        return pl.BlockSpec((HALO, tc), lambda i, j: (jnp.minimum((i + 1) * per, last), c0 + k * nct + j))

    return pl.pallas_call(
        functools.partial(_sconv_kernel, seq=rows.seq, ctx=rows.ctx, lat=rows.lat),
        grid=(rows.all // tr, nct),
        in_specs=[main(0), main(1), main(2), before(1), before(2), after(1), after(2),
                  pl.BlockSpec((3, tc), lambda i, j: (0, j))],
        out_specs=pl.BlockSpec((tr, tc), lambda i, j: (i, j)),
        out_shape=jax.ShapeDtypeStruct((rows.all, width), BF16),
        compiler_params=_cparams(("arbitrary", "arbitrary")),
        name="short_conv",
    )(hl, hl, hl, hl, hl, hl, hl, conv_w)


def _dwconv_kernel(x_ref, w_ref, o_ref):
    o_ref[...] = _conv3(x_ref[...].astype(F32), w_ref[...]).astype(o_ref.dtype)


def _hyena_conv3(hl, col0, width, conv_w, rows):
    tc = 256
    c0 = col0 // tc
    return pl.pallas_call(
        _dwconv_kernel,
        grid=(rows.batch, width // tc),
        in_specs=[
            pl.BlockSpec((rows.seq, tc), lambda s, j: (s, c0 + j)),
            pl.BlockSpec((3, tc), lambda s, j: (0, j)),
        ],
        out_specs=pl.BlockSpec((rows.seq, tc), lambda s, j: (s, j)),
        out_shape=jax.ShapeDtypeStruct((rows.lat, width), BF16),
        compiler_params=_cparams(("arbitrary", "arbitrary")),
        name="hyena_conv3",
    )(hl, conv_w)


def _diff_attn_kernel(q_ref, kl_ref, kc_ref, vl_ref, vc_ref, cq_ref, sq_ref, ck_ref, sk_ref, lam_ref, sub_ref,
                      o_ref, kf_ref, vf_ref, *, seq, ctx, lam_init):
    qi = pl.program_id(2)

    @pl.when(qi == 0)
    def _build_keys():
        ones = _ones_column((seq + ctx, LANES))
        for hh in range(ATTN_HEADS):
            cols = slice(hh * LANES, (hh + 1) * LANES)
            kf_ref[hh, 0:seq, :] = _rope_pairs(kl_ref[:, cols].astype(F32), ck_ref[...], sk_ref[...]).astype(BF16)
            kf_ref[hh, seq:seq + ctx, :] = kc_ref[:, cols]
            vf_ref[hh, 0:seq, 0:LANES] = vl_ref[:, cols]
            vf_ref[hh, seq:seq + ctx, 0:LANES] = vc_ref[:, cols]
            vf_ref[hh, :, LANES:2 * LANES] = ones

    lp = lam_ref[...]
    lam = (jnp.exp(jnp.sum(lp[0:1] * lp[1:2], -1, keepdims=True))
           - jnp.exp(jnp.sum(lp[2:3] * lp[3:4], -1, keepdims=True)) + lam_init)
    def scores(hh, r):
        cols = slice(hh * LANES, (hh + 1) * LANES)
        q = _rope_pairs(q_ref[r:r + ATTN_CHUNK, cols].astype(F32), cq_ref[r:r + ATTN_CHUNK, :],
                        sq_ref[r:r + ATTN_CHUNK, :])
        first = lax.broadcasted_iota(I32, q.shape, 1) < DIFF_HEAD_DIM
        qq = jnp.concatenate([jnp.where(first, q, 0.0), jnp.where(first, 0.0, q)], 0).astype(BF16)
        return lax.dot_general(qq, kf_ref[hh], (((1,), (1,)), ((), ())), preferred_element_type=F32)

    def finish(hh, r, s):
        e = jnp.exp2(s - jnp.max(s, -1, keepdims=True)).astype(BF16)
        ov = jnp.dot(e, vf_ref[hh], preferred_element_type=F32)
        on = ov[:, :LANES] / ov[:, LANES:LANES + 1]
        o = on[:ATTN_CHUNK] - lam * on[ATTN_CHUNK:]
        out = (_rms(o) * sub_ref[...] * (1.0 - lam_init)).astype(o_ref.dtype)
        o_ref[r:r + ATTN_CHUNK, hh * LANES:(hh + 1) * LANES] = out

    pending = []
    for hh in range(ATTN_HEADS):
        for r in range(0, q_ref.shape[0], ATTN_CHUNK):
            pending.append((hh, r, scores(hh, r)))
            if len(pending) > ATTN_AHEAD:
                finish(*pending.pop(0))
    for p in pending:
        finish(*p)


def _diff_attention(hl, hckv, cos_t, sin_t, lam_p, subln, rows, lam_init):
    batch, seq, ctx = rows.batch, rows.seq, rows.ctx
    tq = 512
    nl = seq // tq
    hw = ATTN_HEADS * LANES
    kc0 = DIFF_QK // hw
    vc0 = 2 * DIFF_QK // hw
    lam_pad = jnp.pad(lam_p.astype(F32), ((0, 0), (0, LANES - lam_p.shape[1])))
    return pl.pallas_call(
        functools.partial(_diff_attn_kernel, seq=seq, ctx=ctx, lam_init=lam_init),
        grid=(batch, DIFF_HEADS // ATTN_HEADS, nl),
        in_specs=[
            pl.BlockSpec((tq, hw), lambda b, h, qi: (b * nl + qi, h)),
            pl.BlockSpec((seq, hw), lambda b, h, qi: (b, kc0 + h)),
            pl.BlockSpec((ctx, hw), lambda b, h, qi: (b, h)),
            pl.BlockSpec((seq, hw), lambda b, h, qi: (b, vc0 + h)),
            pl.BlockSpec((ctx, hw), lambda b, h, qi: (b, kc0 + h)),
            pl.BlockSpec((tq, LANES), lambda b, h, qi: (qi, 0)),
            pl.BlockSpec((tq, LANES), lambda b, h, qi: (qi, 0)),
            pl.BlockSpec((seq, LANES), lambda b, h, qi: (0, 0)),
            pl.BlockSpec((seq, LANES), lambda b, h, qi: (0, 0)),
            pl.BlockSpec((4, LANES), lambda b, h, qi: (0, 0)),
            pl.BlockSpec((1, LANES), lambda b, h, qi: (0, 0)),
        ],
        out_specs=pl.BlockSpec((tq, hw), lambda b, h, qi: (b * nl + qi, h)),
        out_shape=jax.ShapeDtypeStruct((rows.lat, DIFF_V), BF16),
        scratch_shapes=[pltpu.VMEM((ATTN_HEADS, seq + ctx, LANES), BF16),
                        pltpu.VMEM((ATTN_HEADS, seq + ctx, 2 * LANES), BF16)],
        compiler_params=_cparams(("arbitrary", "arbitrary", "arbitrary")),
        name="diff_attn",
    )(hl, hl, hckv, hl, hckv, cos_t, sin_t, cos_t, sin_t, lam_pad, subln.reshape(1, LANES).astype(F32))


def _hid_kernel(z_ref, w1_ref, b1_ref, w2_ref, b2_ref, o_ref):
    h1 = jnp.sin(_dot3(z_ref[...], w1_ref[...]) + b1_ref[...])
    o_ref[...] = jnp.sin(_dot3(h1, w2_ref[...]) + b2_ref[...])


def _filt_kernel(hid_ref, w3f_ref, w3b_ref, dec_ref, o_ref, *, n):
    hf = hid_ref[0:n, :]
    hr = hid_ref[n:2 * n, :]
    w3f, w3b = w3f_ref[...], w3b_ref[...]
    kf = _dot3(hf, w3f)
    kb0 = _dot3(hf[0:8], w3b)[0:1]
    kbr = _dot3(hr, w3b)
    dec = dec_ref[...]
    row = lax.broadcasted_iota(I32, kf.shape, 0)
    inv = 1.0 / (n - 1)
    win_f = jnp.exp(-(row.astype(F32) * inv) * dec) + HY_SHIFT
    win_r = jnp.exp(-((n - row).astype(F32) * inv) * dec) + HY_SHIFT
    head = kf * win_f + jnp.where(row == 0, kb0 * (1.0 + HY_SHIFT), 0.0)
    tail = jnp.where(row == 0, 0.0, kbr * win_r)
    nrm = jnp.sum(jnp.abs(head), 0, keepdims=True) + jnp.sum(jnp.abs(tail), 0, keepdims=True)
    o_ref[0:n, :] = (head / nrm).astype(o_ref.dtype)
    o_ref[n:2 * n, :] = (tail / nrm).astype(o_ref.dtype)


def _hyena_filters(n, width, w1, b1, w2, b2, w3):
    t = jnp.linspace(0.0, 1.0, n, dtype=F32)[:, None]
    ang = (2.0 * math.pi / n) * jnp.arange(n, dtype=F32)[:, None] * jnp.linspace(1e-4, HY_BANDS - 1, HY_BANDS, dtype=F32)[None, :]
    z = jnp.concatenate([t, jnp.cos(ang), -jnp.sin(ang)], -1)
    z_rev = jnp.concatenate([z[:1], z[:0:-1]], 0)
    z2 = jnp.pad(jnp.concatenate([z, z_rev], 0), ((0, 0), (0, LANES - HY_EMB)))
    pad = LANES - HY_FFN
    w1p = jnp.pad(w1.astype(F32), ((0, LANES - HY_EMB), (0, pad)))
    b1p = jnp.pad(b1.astype(F32), (0, pad)).reshape(1, LANES)
    w2p = jnp.pad(w2.astype(F32), ((0, pad), (0, pad)))
    b2p = jnp.pad(b2.astype(F32), (0, pad)).reshape(1, LANES)
    w3p = jnp.pad(w3.astype(F32), ((0, pad), (0, 0)))
    th = min(1024, 2 * n)
    hid = pl.pallas_call(
        _hid_kernel,
        grid=(2 * n // th,),
        in_specs=[
            pl.BlockSpec((th, LANES), lambda i: (i, 0)),
            pl.BlockSpec((LANES, LANES), lambda i: (0, 0)),
            pl.BlockSpec((1, LANES), lambda i: (0, 0)),
            pl.BlockSpec((LANES, LANES), lambda i: (0, 0)),
            pl.BlockSpec((1, LANES), lambda i: (0, 0)),
        ],
        out_specs=pl.BlockSpec((th, LANES), lambda i: (i, 0)),
        out_shape=jax.ShapeDtypeStruct((2 * n, LANES), F32),
        compiler_params=_cparams(("arbitrary",)),
        name="hyena_hidden",
    )(z2, w1p, b1p, w2p, b2p)
    decay = jnp.abs(jnp.linspace(HY_MIN_DECAY, HY_MAX_DECAY, width, dtype=F32)).reshape(1, width)
    tn = 256
    per = width // tn
    return pl.pallas_call(
        functools.partial(_filt_kernel, n=n),
        grid=(HY_ORDER, per),
        in_specs=[
            pl.BlockSpec((2 * n, LANES), lambda o, j: (0, 0)),
            pl.BlockSpec((LANES, tn), lambda o, j: (0, (2 * o) * per + j)),
            pl.BlockSpec((LANES, tn), lambda o, j: (0, (2 * o + 1) * per + j)),
            pl.BlockSpec((1, tn), lambda o, j: (0, j)),
        ],
        out_specs=pl.BlockSpec((None, 2 * n, tn), lambda o, j: (o, 0, j)),
        out_shape=jax.ShapeDtypeStruct((HY_ORDER, 2 * n, width), BF16),
        compiler_params=_cparams(("arbitrary", "arbitrary")),
        name="hyena_filters",
    )(hid, w3p, w3p, decay)


def _dft_tables(n):
    p = 2 * n
    k = jnp.arange(n, dtype=I32)[:, None]
    t = jnp.arange(p, dtype=I32)[None, :]
    ang = ((k * t) % p).astype(F32) * (2.0 * math.pi / p)
    alt = jnp.where(t % 2 == 0, 1.0, -1.0)
    fx = jnp.cos(ang)
    fy = jnp.where(k == 0, alt, -jnp.sin(ang))
    fwd = jnp.stack([fx, fy]).astype(BF16)
    tt = jnp.arange(n, dtype=I32)[:, None]
    kk = jnp.arange(n, dtype=I32)[None, :]
    ang2 = ((tt * kk) % p).astype(F32) * (2.0 * math.pi / p)
    alt2 = jnp.where(tt % 2 == 0, 1.0, -1.0)
    gx = jnp.where(kk == 0, 1.0, 2.0 * jnp.cos(ang2)) / p
    gy = jnp.where(kk == 0, alt2, -2.0 * jnp.sin(ang2)) / p
    inv = jnp.stack([gx, gy]).astype(BF16)
    return fwd, inv


def _spec_kernel(fx_ref, fy_ref, f_ref, o_ref):
    f = f_ref[...]
    o_ref[0] = jnp.dot(fx_ref[...], f, preferred_element_type=F32)
    o_ref[1] = jnp.dot(fy_ref[...], f, preferred_element_type=F32)


def _filter_spectrum(fwd, filt):
    _, n, p = fwd.shape
    width = filt.shape[2]
    tm, tn = min(512, n), 512
    return pl.pallas_call(
        _spec_kernel,
        grid=(HY_ORDER, width // tn, n // tm),
        in_specs=[
            pl.BlockSpec((None, tm, p), lambda o, j, i: (0, i, 0)),
            pl.BlockSpec((None, tm, p), lambda o, j, i: (1, i, 0)),
            pl.BlockSpec((None, p, tn), lambda o, j, i: (o, 0, j)),
        ],
        out_specs=pl.BlockSpec((None, 2, tm, tn), lambda o, j, i: (o, 0, i, j)),
        out_shape=jax.ShapeDtypeStruct((HY_ORDER, 2, n, width), F32),
        compiler_params=_cparams(("arbitrary", "arbitrary", "arbitrary")),
        name="filter_spectrum",
    )(fwd, fwd, filt)


def _dft_fwd_kernel(fx_ref, fy_ref, z_ref, hx_ref, hy_ref, o_ref):
    i = pl.program_id(2)
    z = z_ref[...]
    ux = jnp.dot(fx_ref[...], z, preferred_element_type=F32)
    uy = jnp.dot(fy_ref[...], z, preferred_element_type=F32)
    hx, hy = hx_ref[...], hy_ref[...]
    row = lax.broadcasted_iota(I32, ux.shape, 0) + i * ux.shape[0]
    dc = row == 0
    o_ref[0] = (ux * hx - jnp.where(dc, 0.0, uy * hy)).astype(o_ref.dtype)
    o_ref[1] = jnp.where(dc, uy * hy, ux * hy + uy * hx).astype(o_ref.dtype)


def _dft_forward(fwd, z, z_col0, spec, order, rows):
    n = rows.seq
    width = spec.shape[3]
    tm, tn = min(1024, n), 512
    c0 = z_col0 // tn
    return pl.pallas_call(
        _dft_fwd_kernel,
        grid=(rows.batch, width // tn, n // tm),
        in_specs=[
            pl.BlockSpec((None, tm, n), lambda b, j, i: (0, i, 0)),
            pl.BlockSpec((None, tm, n), lambda b, j, i: (1, i, 0)),
            pl.BlockSpec((n, tn), lambda b, j, i: (b, c0 + j)),
            pl.BlockSpec((None, None, tm, tn), lambda b, j, i: (order, 0, i, j)),
            pl.BlockSpec((None, None, tm, tn), lambda b, j, i: (order, 1, i, j)),
        ],
        out_specs=pl.BlockSpec((None, 2, tm, tn), lambda b, j, i: (b, 0, i, j)),
        out_shape=jax.ShapeDtypeStruct((rows.batch, 2, n, width), BF16),
        compiler_params=_cparams(("arbitrary", "arbitrary", "arbitrary")),
        name="hyena_dft_fwd",
    )(fwd, fwd, z, spec, spec)


def _dft_inv_kernel(gx_ref, gy_ref, s_ref, gate_ref, z_ref, skip_ref, o_ref):
    y = (jnp.dot(gx_ref[...], s_ref[0], preferred_element_type=F32)
         + jnp.dot(gy_ref[...], s_ref[1], preferred_element_type=F32))
    z = z_ref[...].astype(F32)
    o_ref[...] = (gate_ref[...].astype(F32) * (y + z * skip_ref[...])).astype(o_ref.dtype)


def _dft_inverse(inv, spec_z, gates, gate_col0, z, z_col0, skip, order, rows):
    n = rows.seq
    width = spec_z.shape[3]
    tm, tn = min(1024, n), 512
    nb = n // tm
    g0, z0 = gate_col0 // tn, z_col0 // tn
    return pl.pallas_call(
        _dft_inv_kernel,
        grid=(rows.batch, width // tn, nb),
        in_specs=[
            pl.BlockSpec((None, tm, n), lambda b, j, i: (0, i, 0)),
            pl.BlockSpec((None, tm, n), lambda b, j, i: (1, i, 0)),
            pl.BlockSpec((None, 2, n, tn), lambda b, j, i: (b, 0, 0, j)),
            pl.BlockSpec((tm, tn), lambda b, j, i: (b * nb + i, g0 + j)),
            pl.BlockSpec((tm, tn), lambda b, j, i: (b * nb + i, z0 + j)),
            pl.BlockSpec((None, 1, tn), lambda b, j, i: (order, 0, j)),
        ],
        out_specs=pl.BlockSpec((tm, tn), lambda b, j, i: (b * nb + i, j)),
        out_shape=jax.ShapeDtypeStruct((rows.lat, width), BF16),
        compiler_params=_cparams(("arbitrary", "arbitrary", "arbitrary")),
        name="hyena_dft_inv",
    )(inv, inv, spec_z, gates, z, skip.reshape(HY_ORDER, 1, width).astype(F32))


def _router_kernel(x_ref, sh_ref, sc_ref, rw_ref, rb_ref, u_ref, ti_ref, tg_ref, rk_ref, cnt_ref, carry_ref):
    i = pl.program_id(0)

    @pl.when(i == 0)
    def _init():
        carry_ref[...] = jnp.zeros_like(carry_ref)

    u = _rms(x_ref[...]) * (1.0 + sc_ref[...]) + sh_ref[...]
    u_ref[...] = _pack_halves(u)
    logits = _dot3(u, rw_ref[...]) + rb_ref[...]
    tm = logits.shape[0]
    lane = lax.broadcasted_iota(I32, logits.shape, 1).astype(F32)
    neg = jnp.float32(-jnp.inf)
    work = jnp.where(lane < N_EXPERTS, logits, neg)
    vals, idxs = [], []
    for _ in range(TOP_K):
        m = jnp.max(work, -1, keepdims=True)
        idx = jnp.min(jnp.where(work == m, lane, float(LANES)), -1, keepdims=True)
        vals.append(m)
        idxs.append(idx)
        work = jnp.where(lane == idx, neg, work)
    es = [jnp.exp(v - vals[0]) for v in vals]
    den = es[0] + es[1] + es[2] + es[3]
    onehot = jnp.zeros(logits.shape, F32)
    for idx in idxs:
        onehot = onehot + (lane == idx).astype(F32)
    r = lax.broadcasted_iota(I32, (tm, tm), 0)
    c = lax.broadcasted_iota(I32, (tm, tm), 1)
    before = (c < r).astype(BF16)
    carry = carry_ref[0:1, :]
    prefix = jnp.dot(before, onehot.astype(BF16), preferred_element_type=F32) + carry
    ti = jnp.zeros(logits.shape, F32)
    tg = jnp.zeros(logits.shape, F32)
    rk = jnp.zeros(logits.shape, F32)
    for k in range(TOP_K):
        pos = jnp.sum(jnp.where(lane == idxs[k], prefix, 0.0), -1, keepdims=True)
        ti = jnp.where(lane == k, idxs[k], ti)
        tg = jnp.where(lane == k, es[k] / den, tg)
        rk = jnp.where(lane == k, pos, rk)
    ti_ref[...] = ti
    tg_ref[...] = tg
    rk_ref[...] = rk
    total = carry + jnp.sum(onehot, 0, keepdims=True)
    carry_ref[...] = jnp.broadcast_to(total, carry_ref.shape)
    cnt_ref[...] = jnp.broadcast_to(total, cnt_ref.shape)


def _router(h, mod4, rows, layer, n_rows, router_w, router_b):
    d = h.shape[1]
    tm = 256
    rw = jnp.pad(router_w.astype(F32), ((0, 0), (0, LANES - N_EXPERTS)))
    rb = jnp.pad(router_b.astype(F32), (0, LANES - N_EXPERTS)).reshape(1, LANES)
    small = lambda dt: jax.ShapeDtypeStruct((n_rows, LANES), dt)
    return pl.pallas_call(
        _router_kernel,
        grid=(n_rows // tm,),
        in_specs=[
            pl.BlockSpec((tm, d), lambda i: (i, 0)),
            _mod_spec(rows, tm, d, layer, 3, d, 1, 0),
            _mod_spec(rows, tm, d, layer, 4, d, 1, 0),
            pl.BlockSpec((d, LANES), lambda i: (0, 0)),
            pl.BlockSpec((1, LANES), lambda i: (0, 0)),
        ],
        out_specs=[
            pl.BlockSpec((tm, d // 2), lambda i: (i, 0)),
            pl.BlockSpec((tm, LANES), lambda i: (i, 0)),
            pl.BlockSpec((tm, LANES), lambda i: (i, 0)),
            pl.BlockSpec((tm, LANES), lambda i: (i, 0)),
            pl.BlockSpec((8, LANES), lambda i: (0, 0)),
        ],
        out_shape=[jax.ShapeDtypeStruct((n_rows, d // 2), U32), small(F32), small(F32), small(F32),
                   jax.ShapeDtypeStruct((8, LANES), F32)],
        scratch_shapes=[pltpu.VMEM((8, LANES), F32)],
        compiler_params=_cparams(("arbitrary",)),
        name="router",
    )(h, mod4, mod4, rw, rb)


def _row_copy(src_ref, src_row, dst_ref, dst_row, sem):
    return pltpu.make_async_copy(src_ref.at[pl.ds(src_row, 1)], dst_ref.at[pl.ds(dst_row, 1)], sem)


def _block_copy(src_ref, dst_ref, sem):
    return pltpu.make_async_copy(src_ref.at[pl.ds(0, dst_ref.shape[0])], dst_ref, sem)


def _expert_kernel(be_ref, used_ref, tok_ref, tok_next_ref, src_ref, wgu_ref, bgu_ref, wd_ref, bd_ref, o_ref,
                   xbuf, sems):
    i = pl.program_id(0)
    slot = lax.rem(i, 2)
    n_used = used_ref[0]

    def gather(ids_ref, s):
        for r in range(MOE_BLK):
            _row_copy(src_ref, ids_ref[0, 0, r], xbuf.at[s], r, sems.at[s]).start()

    @pl.when(i == 0)
    def _first():
        gather(tok_ref, 0)

    @pl.when(i + 1 < n_used)
    def _ahead():
        gather(tok_next_ref, 1 - slot)

    @pl.when(i < n_used)
    def _compute():
        _block_copy(src_ref, xbuf.at[slot], sems.at[slot]).wait()
        dot = functools.partial(jnp.dot, preferred_element_type=F32)
        def gate_up(r):
            lo, hi = _unpack_halves(xbuf[slot, r:r + EXPERT_CHUNK, :])
            half = lo.shape[1]
            return (dot(lo.astype(BF16), wgu_ref[0:half, :]) + dot(hi.astype(BF16), wgu_ref[half:2 * half, :])
                    + bgu_ref[...])

        def down(r, gu):
            de = gu.shape[1] // 2
            g = jnp.minimum(gu[:, :de], SWIGLU_LIMIT)
            up = jnp.clip(gu[:, de:], -SWIGLU_LIMIT, SWIGLU_LIMIT)
            a = (up + 1.0) * g * jax.nn.sigmoid(SWIGLU_ALPHA * g)
            y = dot(a.astype(BF16), wd_ref[...]) + bd_ref[...]
            o_ref[r:r + EXPERT_CHUNK, :] = _pack_halves(y)

        pending = None
        for r in range(0, MOE_BLK, EXPERT_CHUNK):
            gu = gate_up(r)
            if pending is not None:
                down(*pending)
            pending = (r, gu)
        down(*pending)

    @pl.when(i >= n_used)
    def _empty():
        o_ref[...] = jnp.zeros(o_ref.shape, o_ref.dtype)


def _experts(u_packed, slot_tok, block_e, n_used, layer, w_gu, b_gu, w_down, b_down):
    _, n_exp, d, de2 = w_gu.shape
    de = de2 // 2
    n_blocks = block_e.shape[0]
    tok = slot_tok.reshape(n_blocks, 1, MOE_BLK)
    grid_spec = pltpu.PrefetchScalarGridSpec(
        num_scalar_prefetch=2,
        grid=(n_blocks,),
        in_specs=[
            pl.BlockSpec((1, 1, MOE_BLK), lambda i, be, nu: (i, 0, 0), memory_space=pltpu.SMEM),
            pl.BlockSpec((1, 1, MOE_BLK), lambda i, be, nu: (jnp.minimum(i + 1, n_blocks - 1), 0, 0),
                         memory_space=pltpu.SMEM),
            pl.BlockSpec(memory_space=pl.ANY),
            pl.BlockSpec((None, None, d, de2), lambda i, be, nu: (layer, be[i], 0, 0)),
            pl.BlockSpec((None, None, 1, de2), lambda i, be, nu: (layer, be[i], 0, 0)),
            pl.BlockSpec((None, None, de, d), lambda i, be, nu: (layer, be[i], 0, 0)),
            pl.BlockSpec((None, None, 1, d), lambda i, be, nu: (layer, be[i], 0, 0)),
        ],
        out_specs=pl.BlockSpec((MOE_BLK, d // 2), lambda i, be, nu: (i, 0)),
        scratch_shapes=[pltpu.VMEM((2, MOE_BLK, d // 2), U32), pltpu.SemaphoreType.DMA((2,))],
    )
    return pl.pallas_call(
        _expert_kernel,
        grid_spec=grid_spec,
        out_shape=jax.ShapeDtypeStruct((n_blocks * MOE_BLK, d // 2), U32),
        compiler_params=_cparams(("arbitrary",)),
        name="moe_experts",
    )(block_e, n_used, tok, tok, u_packed, w_gu, b_gu.reshape(-1, n_exp, 1, de2).astype(F32), w_down,
      b_down.reshape(-1, n_exp, 1, d).astype(F32))


def _combine_kernel(dest_ref, dest_next_ref, gate_ref, h_ref, g2_ref, fn_ref, yb_ref, o_ref, buf, sems, *, final):
    n = h_ref.shape[0]
    i = pl.program_id(0)
    slot = lax.rem(i, 2)

    def gather(ids_ref, s):
        for r in range(n):
            for k in range(TOP_K):
                _row_copy(yb_ref, ids_ref[0, 0, r * TOP_K + k], buf.at[s], k * n + r, sems.at[s]).start()

    @pl.when(i == 0)
    def _first():
        gather(dest_ref, 0)

    @pl.when(i + 1 < pl.num_programs(0))
    def _ahead():
        gather(dest_next_ref, 1 - slot)

    _block_copy(yb_ref, buf.at[slot], sems.at[slot]).wait()
    gates = gate_ref[...]
    acc_lo = None
    acc_hi = None
    for k in range(TOP_K):
        lo, hi = _unpack_halves(buf[slot, k * n:(k + 1) * n, :])
        gk = gates[:, k:k + 1]
        acc_lo = gk * lo if acc_lo is None else acc_lo + gk * lo
        acc_hi = gk * hi if acc_hi is None else acc_hi + gk * hi
    out = h_ref[...] + g2_ref[...] * jnp.concatenate([acc_lo, acc_hi], -1)
    if final:
        out = _rms(out) * fn_ref[...]
    o_ref[...] = out


def _combine(dest, gates, h, mod4, rows, layer, n_rows, yb, final_gain, final):
    d = h.shape[1]
    tm = COMBINE_BLK
    nb = n_rows // tm
    dest3 = dest.reshape(nb, 1, tm * TOP_K)
    return pl.pallas_call(
        functools.partial(_combine_kernel, final=final),
        grid=(nb,),
        in_specs=[
            pl.BlockSpec((1, 1, tm * TOP_K), lambda i: (i, 0, 0), memory_space=pltpu.SMEM),
            pl.BlockSpec((1, 1, tm * TOP_K), lambda i: (jnp.minimum(i + 1, nb - 1), 0, 0), memory_space=pltpu.SMEM),
            pl.BlockSpec((tm, LANES), lambda i: (i, 0)),
            pl.BlockSpec((tm, d), lambda i: (i, 0)),
            _mod_spec(rows, tm, d, layer, 5, d, 1, 0),
            pl.BlockSpec((1, d), lambda i: (0, 0)),
            pl.BlockSpec(memory_space=pl.ANY),
        ],
        out_specs=pl.BlockSpec((tm, d), lambda i: (i, 0)),
        out_shape=jax.ShapeDtypeStruct((n_rows, d), F32),
        scratch_shapes=[pltpu.VMEM((2, TOP_K * tm, d // 2), U32), pltpu.SemaphoreType.DMA((2,))],
        compiler_params=_cparams(("arbitrary",)),
        name="moe_combine",
    )(dest3, dest3, gates, h, mod4, final_gain.reshape(1, d).astype(F32), yb)


def _moe(h, mod4, rows, layer, n_rows, router_w, router_b, w_gu, b_gu, w_down, b_down, final_gain, final):
    u_packed, ti, tg, rk, cnt = _router(h, mod4, rows, layer, n_rows, router_w, router_b)
    counts = cnt[0, :N_EXPERTS].astype(I32)
    padded = (counts + MOE_BLK - 1) // MOE_BLK * MOE_BLK
    pad_end = jnp.cumsum(padded)
    pad_start = pad_end - padded
    top_i = ti[:, :TOP_K].astype(I32)
    dest = pad_start[top_i] + rk[:, :TOP_K].astype(I32)
    n_blocks = n_rows * TOP_K // MOE_BLK + N_EXPERTS
    tok = jnp.repeat(jnp.arange(n_rows, dtype=I32), TOP_K)
    slot_tok = jnp.zeros((n_blocks * MOE_BLK,), I32).at[dest.reshape(-1)].set(tok)
    block_start = jnp.arange(n_blocks, dtype=I32)[:, None] * MOE_BLK
    block_e = jnp.minimum(jnp.sum((pad_end[None, :] <= block_start).astype(I32), 1), N_EXPERTS - 1)
    n_used = (pad_end[-1:] // MOE_BLK).astype(I32)
    yb = _experts(u_packed, slot_tok, block_e, n_used, layer, w_gu, b_gu, w_down, b_down)
    return _combine(dest, tg, h, mod4, rows, layer, n_rows, yb, final_gain, final)


def _rope_tables(seq):
    n_rows = seq // GRID_W
    row = jnp.repeat(jnp.arange(n_rows), GRID_W).astype(F32)
    col = jnp.tile(jnp.arange(GRID_W), n_rows).astype(F32)
    quarter = MLA_ROPE_DIM // 4
    inv = 1.0 / (ROPE_THETA ** (jnp.arange(quarter, dtype=F32) / quarter))
    ang = jnp.concatenate([row[:, None] * inv, col[:, None] * inv], -1)
    cos, sin = jnp.cos(ang), jnp.sin(ang)
    return jnp.tile(cos, (1, 4)), jnp.tile(jnp.concatenate([-sin, sin], -1), (1, 2))


def kernel(x, c, ctx, c_ctx, ada_w, ada_b, mla_in_w, mla_q_norm, mla_kv_norm, mla_w_uq, mla_w_ukv, sc_conv_w, even_out_w, odd_in_w, diff_lambda, diff_subln, hy_conv_w, hy_w1, hy_b1, hy_w2, hy_b2, hy_w3, hy_skip, odd_out_w, router_w, router_b, moe_w_gu, moe_b_gu, moe_w_down, moe_b_down, final_norm):
    batch, seq, d = x.shape
    n_ctx = ctx.shape[1]
    depth = ada_w.shape[0]
    assert depth == 2 and batch + 1 <= MOD_ROWS
    rows = _Rows(batch, seq, n_ctx)
    sc_width = d - MLA_OUT
    hy_width = d - DIFF_V

    cond = jnp.zeros((MOD_ROWS, d), F32).at[:batch].set(c).at[batch].set(c_ctx)
    mod4 = _adaln(cond, ada_w, ada_b)
    cos_t, sin_t = _rope_tables(seq)
    h = jnp.concatenate([x.reshape(rows.lat, d), ctx.reshape(batch * n_ctx, d)], 0)

    kpe_col = MLA_Q_RANK + MLA_KV_RANK
    sc_col = 2048
    w_in = mla_in_w[0]
    w_in = jnp.concatenate([
        w_in[:, :kpe_col + MLA_ROPE_DIM],
        jnp.zeros((d, sc_col - kpe_col - MLA_ROPE_DIM), w_in.dtype),
        w_in[:, kpe_col + MLA_ROPE_DIM:],
    ], 1).astype(BF16)
    wq = (mla_w_uq[0] * (MLA_QK_DIM ** -0.5 * LOG2E)).reshape(MLA_Q_RANK, MLA_HEADS, MLA_QK_DIM)
    wq_pe = jnp.pad(wq[:, :, MLA_NOPE_DIM:], ((0, 0), (0, 0), (0, LANES - MLA_ROPE_DIM)))
    wq = jnp.concatenate([wq[:, :, :MLA_NOPE_DIM].reshape(MLA_Q_RANK, -1), wq_pe.reshape(MLA_Q_RANK, -1)], 1).astype(BF16)

    u = _modulate(h, mod4, rows, 0, rows.all)
    hl = _matmul([u], w_in, n_rows=rows.all, n_cols=w_in.shape[1], name="even_in_proj")
    qup = _norm_matmul(hl, 0, mla_q_norm[0], wq, rows.all, "mla_q_up")
    kvup = _norm_matmul(hl, MLA_Q_RANK, mla_kv_norm[0], mla_w_ukv[0].astype(BF16), rows.all, "mla_kv_up")
    attn = _mla_attention(qup, kvup, hl, cos_t, sin_t, rows, kpe_col)
    sc = _short_conv(hl, sc_col, sc_width, sc_conv_w[0].astype(F32), rows)
    h = _matmul([attn, sc], even_out_w[0].astype(BF16), n_rows=rows.all, n_cols=d, out_dtype=F32,
                resid=(h, mod4, rows, 0, 2), name="even_out_proj")
    w_gu_bf, w_down_bf = moe_w_gu.astype(BF16), moe_w_down.astype(BF16)
    h = _moe(h, mod4, rows, 0, rows.all, router_w[0], router_b[0], w_gu_bf, moe_b_gu, w_down_bf, moe_b_down,
             final_norm, False)

    lam_init = 0.8 - 0.6 * math.exp(-0.3 * 1)
    w_in1 = jnp.concatenate([odd_in_w[0][:, :DIFF_QK] * (DIFF_HEAD_DIM ** -0.5 * LOG2E), odd_in_w[0][:, DIFF_QK:]],
                            1).astype(BF16)
    u = _modulate(h, mod4, rows, 1, rows.all)
    hl = _matmul([u], w_in1, n_rows=rows.lat, n_cols=w_in1.shape[1], name="odd_in_proj")
    hckv = _matmul([u], w_in1, n_rows=batch * n_ctx, n_cols=DIFF_QK + DIFF_V, a_row0=rows.lat, w_col0=DIFF_QK,
                   name="odd_ctx_kv_proj")
    attn = _diff_attention(hl, hckv, cos_t, sin_t, diff_lambda[0], diff_subln[0], rows, lam_init)

    hy_col = 2 * DIFF_QK + DIFF_V
    hc3 = _hyena_conv3(hl, hy_col, 3 * hy_width, hy_conv_w[0].astype(F32), rows)
    filt = _hyena_filters(seq, hy_width, hy_w1[0], hy_b1[0], hy_w2[0], hy_b2[0], hy_w3[0])
    fwd, inv = _dft_tables(seq)
    spec = _filter_spectrum(fwd, filt)
    z, z_col = hc3, 0
    for o in range(HY_ORDER):
        spec_z = _dft_forward(fwd, z, z_col, spec, o, rows)
        z = _dft_inverse(inv, spec_z, hc3, (1 + o) * hy_width, z, z_col, hy_skip[0], o, rows)
        z_col = 0

    h = _matmul([attn, z], odd_out_w[0].astype(BF16), n_rows=rows.lat, n_cols=d, out_dtype=F32,
                resid=(h, mod4, rows, 1, 2), name="odd_out_proj")
    out = _moe(h, mod4, rows, 1, rows.lat, router_w[1], router_b[1], w_gu_bf, moe_b_gu, w_down_bf, moe_b_down,
               final_norm, True)
    return out.reshape(batch, seq, d)
```

```python
import functools
import math

import jax
import jax.numpy as jnp
from jax import lax
from jax.experimental import pallas as pl
from jax.experimental.pallas import tpu as pltpu

F32 = jnp.float32
BF16 = jnp.bfloat16
U32 = jnp.uint32
I32 = jnp.int32

GRID_W = 64
NORM_EPS = 1e-6
ROPE_THETA = 10000.0

MLA_HEADS = 16
MLA_Q_RANK = 1024
MLA_KV_RANK = 512
MLA_NOPE_DIM = 128
MLA_ROPE_DIM = 64
MLA_V_DIM = 128
MLA_QK_DIM = MLA_NOPE_DIM + MLA_ROPE_DIM
MLA_OUT = MLA_HEADS * MLA_V_DIM

DIFF_HEADS = 16
DIFF_HEAD_DIM = 64
DIFF_QK = DIFF_HEADS * 2 * DIFF_HEAD_DIM
DIFF_V = DIFF_HEADS * 2 * DIFF_HEAD_DIM

HY_ORDER = 2
HY_BANDS = 16
HY_EMB = 1 + 2 * HY_BANDS
HY_FFN = 64
HY_SHIFT = 0.05
HY_MIN_DECAY = math.log(1e-2) / 1.5
HY_MAX_DECAY = math.log(1e-2) / 0.3

N_EXPERTS = 32
TOP_K = 4
SWIGLU_LIMIT = 7.0
SWIGLU_ALPHA = 1.702

LOG2E = 1.4426950408889634
LANES = 128
MOD_ROWS = 16
VMEM_LIMIT = 56 * 1024 * 1024
MOE_BLK = 512
EXPERT_CHUNK = 128
COMBINE_BLK = 128
ATTN_CHUNK = 128
ATTN_HEADS = 2
ATTN_AHEAD = 1


def _cparams(sem):
    return pltpu.CompilerParams(dimension_semantics=sem, vmem_limit_bytes=VMEM_LIMIT)


def _rms(x):
    return x * lax.rsqrt(jnp.mean(x * x, -1, keepdims=True) + NORM_EPS)


def _split_bf16(a):
    hi = a.astype(BF16)
    lo = (a - hi.astype(F32)).astype(BF16)
    return hi, lo


def _dot3(a, b):
    ah, al = _split_bf16(a)
    bh, bl = _split_bf16(b)
    d = functools.partial(jnp.dot, preferred_element_type=F32)
    return d(ah, bh) + d(al, bh) + d(ah, bl)


def _pack_halves(y):
    c = y.shape[1] // 2
    lo = lax.bitcast_convert_type(y[:, :c].astype(BF16).astype(F32), U32)
    hi = lax.bitcast_convert_type(y[:, c:].astype(BF16).astype(F32), U32)
    return (lo >> 16) | (hi & jnp.uint32(0xFFFF0000))


def _unpack_halves(w):
    lo = lax.bitcast_convert_type(w << 16, F32)
    hi = lax.bitcast_convert_type(w & jnp.uint32(0xFFFF0000), F32)
    return lo, hi


def _ones_column(shape):
    lane = lax.broadcasted_iota(I32, shape, 1)
    return jnp.where(lane == 0, 1.0, 0.0).astype(BF16)


def _rope_pairs(x, cos_t, sin_t):
    lane = lax.broadcasted_iota(I32, x.shape, 1)
    swapped = jnp.where((lane & 63) < 32, pltpu.roll(x, 96, 1), pltpu.roll(x, 32, 1))
    return x * cos_t + swapped * sin_t


def _adaln_kernel(c_ref, w_ref, b_ref, o_ref):
    x = c_ref[...]
    s = (x * jax.nn.sigmoid(x)).astype(BF16)
    o_ref[...] = jnp.dot(s, w_ref[...].astype(BF16), preferred_element_type=F32) + b_ref[...]


def _adaln(cond, ada_w, ada_b):
    depth, d, d6 = ada_w.shape
    tn = 512
    out = pl.pallas_call(
        _adaln_kernel,
        grid=(depth, d6 // tn),
        in_specs=[
            pl.BlockSpec((MOD_ROWS, d), lambda l, j: (0, 0)),
            pl.BlockSpec((None, d, tn), lambda l, j: (l, 0, j)),
            pl.BlockSpec((None, 1, tn), lambda l, j: (l, 0, j)),
        ],
        out_specs=pl.BlockSpec((None, MOD_ROWS, tn), lambda l, j: (l, 0, j)),
        out_shape=jax.ShapeDtypeStruct((depth, MOD_ROWS, d6), F32),
        compiler_params=_cparams(("arbitrary", "arbitrary")),
        name="adaln",
    )(cond, ada_w, ada_b.reshape(depth, 1, d6))
    return out.reshape(depth, MOD_ROWS, 1, d6)


class _Rows:
    def __init__(self, batch, seq, ctx):
        self.batch, self.seq, self.ctx = batch, seq, ctx
        self.lat = batch * seq
        self.all = batch * (seq + ctx)

    def mod_row(self, i, tm):
        r = i * tm
        return jnp.where(r < self.lat, r // self.seq, self.batch)


def _mod_spec(rows, tm, tn, layer, chunk, d, grid_rank, row_axis, col_axis=None):
    per = d // tn

    def imap(*g):
        j = 0 if col_axis is None else g[col_axis]
        return (layer, rows.mod_row(g[row_axis], tm), 0, chunk * per + j)

    return pl.BlockSpec((None, None, 1, tn), imap)


def _modulate_kernel(x_ref, sh_ref, sc_ref, o_ref):
    u = _rms(x_ref[...]) * (1.0 + sc_ref[...]) + sh_ref[...]
    o_ref[...] = u.astype(o_ref.dtype)


def _modulate(h, mod4, rows, layer, n_rows):
    d = h.shape[1]
    tm = 256
    return pl.pallas_call(
        _modulate_kernel,
        grid=(n_rows // tm,),
        in_specs=[
            pl.BlockSpec((tm, d), lambda i: (i, 0)),
            _mod_spec(rows, tm, d, layer, 0, d, 1, 0),
            _mod_spec(rows, tm, d, layer, 1, d, 1, 0),
        ],
        out_specs=pl.BlockSpec((tm, d), lambda i: (i, 0)),
        out_shape=jax.ShapeDtypeStruct((n_rows, d), BF16),
        compiler_params=_cparams(("arbitrary",)),
        name="modulate",
    )(h, mod4, mod4)


def _mm_kernel(*refs, ks, resid):
    a_refs = refs[:len(ks)]
    w_ref = refs[len(ks)]
    o_ref = refs[-1]
    acc = None
    off = 0
    for a_ref, k in zip(a_refs, ks):
        part = jnp.dot(a_ref[...], w_ref[off:off + k, :], preferred_element_type=F32)
        acc = part if acc is None else acc + part
        off += k
    if resid:
        h_ref, g_ref = refs[len(ks) + 1], refs[len(ks) + 2]
        o_ref[...] = h_ref[...] + g_ref[...] * acc
    else:
        o_ref[...] = acc.astype(o_ref.dtype)


def _matmul(a_list, w, *, n_rows, n_cols, a_row0=0, w_col0=0, tm=512, tn=1024, out_dtype=BF16,
            resid=None, name="matmul"):
    tn = min(tn, n_cols)
    tm = min(tm, n_rows)
    ks = tuple(a.shape[1] for a in a_list)
    k_all = sum(ks)
    r0, c0 = a_row0 // tm, w_col0 // tn
    in_specs = [pl.BlockSpec((tm, k), lambda j, i: (r0 + i, 0)) for k in ks]
    in_specs.append(pl.BlockSpec((k_all, tn), lambda j, i: (0, c0 + j)))
    args = list(a_list) + [w]
    if resid is not None:
        h, mod4, rows, layer, chunk = resid
        in_specs.append(pl.BlockSpec((tm, tn), lambda j, i: (i, j)))
        in_specs.append(_mod_spec(rows, tm, tn, layer, chunk, h.shape[1], 2, 1, 0))
        args += [h, mod4]
    return pl.pallas_call(
        functools.partial(_mm_kernel, ks=ks, resid=resid is not None),
        grid=(n_cols // tn, n_rows // tm),
        in_specs=in_specs,
        out_specs=pl.BlockSpec((tm, tn), lambda j, i: (i, j)),
        out_shape=jax.ShapeDtypeStruct((n_rows, n_cols), out_dtype),
        compiler_params=_cparams(("arbitrary", "arbitrary")),
        name=name,
    )(*args)


def _norm_mm_kernel(a_ref, g_ref, w_ref, o_ref):
    y = _rms(a_ref[...].astype(F32)) * g_ref[...]
    o_ref[...] = jnp.dot(y.astype(BF16), w_ref[...], preferred_element_type=F32).astype(o_ref.dtype)


def _norm_matmul(a, a_col0, gain, w, n_rows, name):
    kw, n = w.shape
    tm = 512
    cb = a_col0 // kw
    return pl.pallas_call(
        _norm_mm_kernel,
        grid=(n_rows // tm,),
        in_specs=[
            pl.BlockSpec((tm, kw), lambda i: (i, cb)),
            pl.BlockSpec((1, kw), lambda i: (0, 0)),
            pl.BlockSpec((kw, n), lambda i: (0, 0)),
        ],
        out_specs=pl.BlockSpec((tm, n), lambda i: (i, 0)),
        out_shape=jax.ShapeDtypeStruct((n_rows, n), BF16),
        compiler_params=_cparams(("arbitrary",)),
        name=name,
    )(a, gain.reshape(1, kw).astype(F32), w)


def _mla_attn_kernel(qn_ref, qp_ref, kvl_ref, kvc_ref, kpl_ref, kpc_ref, cq_ref, sq_ref, ck_ref, sk_ref,
                     o_ref, kf_ref, vf_ref, *, seq, ctx, n_lat_blocks):
    qi = pl.program_id(2)

    @pl.when(qi == 0)
    def _build_keys():
        kpe = _rope_pairs(kpl_ref[...].astype(F32), ck_ref[...], sk_ref[...]).astype(BF16)
        kpc = kpc_ref[...]
        ones = _ones_column((seq + ctx, LANES))
        for hh in range(ATTN_HEADS):
            c = 2 * LANES * hh
            kf_ref[hh, 0:seq, 0:LANES] = kvl_ref[:, c:c + LANES]
            kf_ref[hh, seq:seq + ctx, 0:LANES] = kvc_ref[:, c:c + LANES]
            vf_ref[hh, 0:seq, 0:LANES] = kvl_ref[:, c + LANES:c + 2 * LANES]
            vf_ref[hh, seq:seq + ctx, 0:LANES] = kvc_ref[:, c + LANES:c + 2 * LANES]
            vf_ref[hh, :, LANES:2 * LANES] = ones
            kf_ref[hh, 0:seq, LANES:2 * LANES] = kpe
            kf_ref[hh, seq:seq + ctx, LANES:2 * LANES] = kpc

    def attend(chains):
        def scores(hh, q, k, v):
            return lax.dot_general(q, k, (((1,), (1,)), ((), ())), preferred_element_type=F32)

        def finish(hh, r, s, v):
            e = jnp.exp2(s - jnp.max(s, -1, keepdims=True)).astype(BF16)
            ov = jnp.dot(e, v, preferred_element_type=F32)
            out = (ov[:, :LANES] / ov[:, LANES:LANES + 1]).astype(o_ref.dtype)
            o_ref[r:r + ATTN_CHUNK, hh * LANES:(hh + 1) * LANES] = out

        units = [(hh, r, q[r:r + ATTN_CHUNK], k, v) for hh, q, k, v in chains
                 for r in range(0, q.shape[0], ATTN_CHUNK)]
        pending = []
        for hh, r, q, k, v in units:
            pending.append((hh, r, scores(hh, q, k, v), v))
            if len(pending) > ATTN_AHEAD:
                finish(*pending.pop(0))
        for p in pending:
            finish(*p)

    @pl.when(qi < n_lat_blocks)
    def _latent_queries():
        chains = []
        for hh in range(ATTN_HEADS):
            cols = slice(hh * LANES, (hh + 1) * LANES)
            qpe = _rope_pairs(qp_ref[:, cols].astype(F32), cq_ref[...], sq_ref[...]).astype(BF16)
            chains.append((hh, jnp.concatenate([qn_ref[:, cols], qpe], -1), kf_ref[hh], vf_ref[hh]))
        attend(chains)

    @pl.when(qi >= n_lat_blocks)
    def _context_queries():
        chains = []
        for hh in range(ATTN_HEADS):
            cols = slice(hh * LANES, (hh + 1) * LANES)
            q = jnp.concatenate([qn_ref[:, cols], qp_ref[:, cols]], -1)
            chains.append((hh, q, kf_ref[hh, seq:seq + ctx, :], vf_ref[hh, seq:seq + ctx, :]))
        attend(chains)


def _mla_attention(qup, kvup, hl, cos_t, sin_t, rows, kpe_col):
    batch, seq, ctx = rows.batch, rows.seq, rows.ctx
    tq = 256
    nl, nc = seq // tq, ctx // tq
    lat_blocks = rows.lat // tq

    def qrow(b, qi):
        return jnp.where(qi < nl, b * nl + qi, lat_blocks + b * nc + (qi - nl))

    def qtab(b, h, qi):
        return (jnp.minimum(qi, nl - 1), 0)

    kb = kpe_col // LANES
    ctx0 = rows.lat // ctx
    hw = ATTN_HEADS * LANES
    groups = MLA_HEADS // ATTN_HEADS
    kv_scratch = pltpu.VMEM((ATTN_HEADS, seq + ctx, 2 * LANES), BF16)
    return pl.pallas_call(
        functools.partial(_mla_attn_kernel, seq=seq, ctx=ctx, n_lat_blocks=nl),
        grid=(batch, groups, nl + nc),
        in_specs=[
            pl.BlockSpec((tq, hw), lambda b, h, qi: (qrow(b, qi), h)),
            pl.BlockSpec((tq, hw), lambda b, h, qi: (qrow(b, qi), groups + h)),
            pl.BlockSpec((seq, 2 * hw), lambda b, h, qi: (b, h)),
            pl.BlockSpec((ctx, 2 * hw), lambda b, h, qi: (ctx0 + b, h)),
            pl.BlockSpec((seq, LANES), lambda b, h, qi: (b, kb)),
            pl.BlockSpec((ctx, LANES), lambda b, h, qi: (ctx0 + b, kb)),
            pl.BlockSpec((tq, LANES), qtab),
            pl.BlockSpec((tq, LANES), qtab),
            pl.BlockSpec((seq, LANES), lambda b, h, qi: (0, 0)),
            pl.BlockSpec((seq, LANES), lambda b, h, qi: (0, 0)),
        ],
        out_specs=pl.BlockSpec((tq, hw), lambda b, h, qi: (qrow(b, qi), h)),
        out_shape=jax.ShapeDtypeStruct((rows.all, MLA_OUT), BF16),
        scratch_shapes=[kv_scratch, kv_scratch],
        compiler_params=_cparams(("arbitrary", "arbitrary", "arbitrary")),
        name="mla_attn",
    )(qup, qup, kvup, kvup, hl, hl, cos_t, sin_t, cos_t, sin_t)


def _conv3(p, w):
    n = p.shape[0]
    row = lax.broadcasted_iota(I32, p.shape, 0)
    prev = jnp.where(row == 0, 0.0, pltpu.roll(p, 1, 0))
    nxt = jnp.where(row == n - 1, 0.0, pltpu.roll(p, n - 1, 0))
    return prev * w[0:1] + p * w[1:2] + nxt * w[2:3]


HALO = 16


def _sconv_kernel(gb_ref, gc_ref, hh_ref, gcp_ref, hhp_ref, gcn_ref, hhn_ref, w_ref, o_ref, *, seq, ctx, lat):
    i = pl.program_id(0)
    tr = gc_ref.shape[0]
    r0 = i * tr
    length = jnp.where(r0 < lat, seq, ctx)
    pos = jnp.where(r0 < lat, r0, r0 - lat)
    starts = lax.rem(pos, length) == 0
    ends = lax.rem(pos + tr, length) == 0
    p = gc_ref[...].astype(F32) * hh_ref[...].astype(F32)
    before = gcp_ref[HALO - 1:HALO, :].astype(F32) * hhp_ref[HALO - 1:HALO, :].astype(F32)
    after = gcn_ref[0:1, :].astype(F32) * hhn_ref[0:1, :].astype(F32)
    before = jnp.where(starts, 0.0, before)
    after = jnp.where(ends, 0.0, after)
    row = lax.broadcasted_iota(I32, p.shape, 0)
    prev = jnp.where(row == 0, before, pltpu.roll(p, 1, 0))
    nxt = jnp.where(row == tr - 1, after, pltpu.roll(p, tr - 1, 0))
    w = w_ref[...]
    conv = prev * w[0:1] + p * w[1:2] + nxt * w[2:3]
    o_ref[...] = (gb_ref[...].astype(F32) * conv).astype(o_ref.dtype)


def _short_conv(hl, col0, width, conv_w, rows):
    tr, tc = 256, 1024
    assert rows.seq % tr == 0 and rows.ctx % tr == 0
    nct = width // tc
    c0 = col0 // tc
    per = tr // HALO
    last = rows.all // HALO - 1

    def main(k):
        return pl.BlockSpec((tr, tc), lambda i, j: (i, c0 + k * nct + j))

    def before(k):
        return pl.BlockSpec((HALO, tc), lambda i, j: (jnp.maximum(i * per - 1, 0), c0 + k * nct + j))

    def after(k):
        return pl.BlockSpec((HALO, tc), lambda i, j: (jnp.minimum((i + 1) * per, last), c0 + k * nct + j))

    return pl.pallas_call(
        functools.partial(_sconv_kernel, seq=rows.seq, ctx=rows.ctx, lat=rows.lat),
        grid=(rows.all // tr, nct),
        in_specs=[main(0), main(1), main(2), before(1), before(2), after(1), after(2),
                  pl.BlockSpec((3, tc), lambda i, j: (0, j))],
        out_specs=pl.BlockSpec((tr, tc), lambda i, j: (i, j)),
        out_shape=jax.ShapeDtypeStruct((rows.all, width), BF16),
        compiler_params=_cparams(("arbitrary", "arbitrary")),
        name="short_conv",
    )(hl, hl, hl, hl, hl, hl, hl, conv_w)


def _dwconv_kernel(x_ref, w_ref, o_ref):
    o_ref[...] = _conv3(x_ref[...].astype(F32), w_ref[...]).astype(o_ref.dtype)


def _hyena_conv3(hl, col0, width, conv_w, rows):
    tc = 256
    c0 = col0 // tc
    return pl.pallas_call(
        _dwconv_kernel,
        grid=(rows.batch, width // tc),
        in_specs=[
            pl.BlockSpec((rows.seq, tc), lambda s, j: (s, c0 + j)),
            pl.BlockSpec((3, tc), lambda s, j: (0, j)),
        ],
        out_specs=pl.BlockSpec((rows.seq, tc), lambda s, j: (s, j)),
        out_shape=jax.ShapeDtypeStruct((rows.lat, width), BF16),
        compiler_params=_cparams(("arbitrary", "arbitrary")),
        name="hyena_conv3",
    )(hl, conv_w)


def _diff_attn_kernel(q_ref, kl_ref, kc_ref, vl_ref, vc_ref, cq_ref, sq_ref, ck_ref, sk_ref, lam_ref, sub_ref,
                      o_ref, kf_ref, vf_ref, *, seq, ctx, lam_init):
    qi = pl.program_id(2)

    @pl.when(qi == 0)
    def _build_keys():
        ones = _ones_column((seq + ctx, LANES))
        for hh in range(ATTN_HEADS):
            cols = slice(hh * LANES, (hh + 1) * LANES)
            kf_ref[hh, 0:seq, :] = _rope_pairs(kl_ref[:, cols].astype(F32), ck_ref[...], sk_ref[...]).astype(BF16)
            kf_ref[hh, seq:seq + ctx, :] = kc_ref[:, cols]
            vf_ref[hh, 0:seq, 0:LANES] = vl_ref[:, cols]
            vf_ref[hh, seq:seq + ctx, 0:LANES] = vc_ref[:, cols]
            vf_ref[hh, :, LANES:2 * LANES] = ones

    lp = lam_ref[...]
    lam = (jnp.exp(jnp.sum(lp[0:1] * lp[1:2], -1, keepdims=True))
           - jnp.exp(jnp.sum(lp[2:3] * lp[3:4], -1, keepdims=True)) + lam_init)
    def scores(hh, r):
        cols = slice(hh * LANES, (hh + 1) * LANES)
        q = _rope_pairs(q_ref[r:r + ATTN_CHUNK, cols].astype(F32), cq_ref[r:r + ATTN_CHUNK, :],
                        sq_ref[r:r + ATTN_CHUNK, :])
        first = lax.broadcasted_iota(I32, q.shape, 1) < DIFF_HEAD_DIM
        qq = jnp.concatenate([jnp.where(first, q, 0.0), jnp.where(first, 0.0, q)], 0).astype(BF16)
        return lax.dot_general(qq, kf_ref[hh], (((1,), (1,)), ((), ())), preferred_element_type=F32)

    def finish(hh, r, s):
        e = jnp.exp2(s - jnp.max(s, -1, keepdims=True)).astype(BF16)
        ov = jnp.dot(e, vf_ref[hh], preferred_element_type=F32)
        on = ov[:, :LANES] / ov[:, LANES:LANES + 1]
        o = on[:ATTN_CHUNK] - lam * on[ATTN_CHUNK:]
        out = (_rms(o) * sub_ref[...] * (1.0 - lam_init)).astype(o_ref.dtype)
        o_ref[r:r + ATTN_CHUNK, hh * LANES:(hh + 1) * LANES] = out

    pending = []
    for hh in range(ATTN_HEADS):
        for r in range(0, q_ref.shape[0], ATTN_CHUNK):
            pending.append((hh, r, scores(hh, r)))
            if len(pending) > ATTN_AHEAD:
                finish(*pending.pop(0))
    for p in pending:
        finish(*p)


def _diff_attention(hl, hckv, cos_t, sin_t, lam_p, subln, rows, lam_init):
    batch, seq, ctx = rows.batch, rows.seq, rows.ctx
    tq = 512
    nl = seq // tq
    hw = ATTN_HEADS * LANES
    kc0 = DIFF_QK // hw
    vc0 = 2 * DIFF_QK // hw
    lam_pad = jnp.pad(lam_p.astype(F32), ((0, 0), (0, LANES - lam_p.shape[1])))
    return pl.pallas_call(
        functools.partial(_diff_attn_kernel, seq=seq, ctx=ctx, lam_init=lam_init),
        grid=(batch, DIFF_HEADS // ATTN_HEADS, nl),
        in_specs=[
            pl.BlockSpec((tq, hw), lambda b, h, qi: (b * nl + qi, h)),
            pl.BlockSpec((seq, hw), lambda b, h, qi: (b, kc0 + h)),
            pl.BlockSpec((ctx, hw), lambda b, h, qi: (b, h)),
            pl.BlockSpec((seq, hw), lambda b, h, qi: (b, vc0 + h)),
            pl.BlockSpec((ctx, hw), lambda b, h, qi: (b, kc0 + h)),
            pl.BlockSpec((tq, LANES), lambda b, h, qi: (qi, 0)),
            pl.BlockSpec((tq, LANES), lambda b, h, qi: (qi, 0)),
            pl.BlockSpec((seq, LANES), lambda b, h, qi: (0, 0)),
            pl.BlockSpec((seq, LANES), lambda b, h, qi: (0, 0)),
            pl.BlockSpec((4, LANES), lambda b, h, qi: (0, 0)),
            pl.BlockSpec((1, LANES), lambda b, h, qi: (0, 0)),
        ],
        out_specs=pl.BlockSpec((tq, hw), lambda b, h, qi: (b * nl + qi, h)),
        out_shape=jax.ShapeDtypeStruct((rows.lat, DIFF_V), BF16),
        scratch_shapes=[pltpu.VMEM((ATTN_HEADS, seq + ctx, LANES), BF16),
                        pltpu.VMEM((ATTN_HEADS, seq + ctx, 2 * LANES), BF16)],
        compiler_params=_cparams(("arbitrary", "arbitrary", "arbitrary")),
        name="diff_attn",
    )(hl, hl, hckv, hl, hckv, cos_t, sin_t, cos_t, sin_t, lam_pad, subln.reshape(1, LANES).astype(F32))


def _hid_kernel(z_ref, w1_ref, b1_ref, w2_ref, b2_ref, o_ref):
    h1 = jnp.sin(_dot3(z_ref[...], w1_ref[...]) + b1_ref[...])
    o_ref[...] = jnp.sin(_dot3(h1, w2_ref[...]) + b2_ref[...])


def _filt_kernel(hid_ref, w3f_ref, w3b_ref, dec_ref, o_ref, *, n):
    hf = hid_ref[0:n, :]
    hr = hid_ref[n:2 * n, :]
    w3f, w3b = w3f_ref[...], w3b_ref[...]
    kf = _dot3(hf, w3f)
    kb0 = _dot3(hf[0:8], w3b)[0:1]
    kbr = _dot3(hr, w3b)
    dec = dec_ref[...]
    row = lax.broadcasted_iota(I32, kf.shape, 0)
    inv = 1.0 / (n - 1)
    win_f = jnp.exp(-(row.astype(F32) * inv) * dec) + HY_SHIFT
    win_r = jnp.exp(-((n - row).astype(F32) * inv) * dec) + HY_SHIFT
    head = kf * win_f + jnp.where(row == 0, kb0 * (1.0 + HY_SHIFT), 0.0)
    tail = jnp.where(row == 0, 0.0, kbr * win_r)
    nrm = jnp.sum(jnp.abs(head), 0, keepdims=True) + jnp.sum(jnp.abs(tail), 0, keepdims=True)
    o_ref[0:n, :] = (head / nrm).astype(o_ref.dtype)
    o_ref[n:2 * n, :] = (tail / nrm).astype(o_ref.dtype)


def _hyena_filters(n, width, w1, b1, w2, b2, w3):
    t = jnp.linspace(0.0, 1.0, n, dtype=F32)[:, None]
    ang = (2.0 * math.pi / n) * jnp.arange(n, dtype=F32)[:, None] * jnp.linspace(1e-4, HY_BANDS - 1, HY_BANDS, dtype=F32)[None, :]
    z = jnp.concatenate([t, jnp.cos(ang), -jnp.sin(ang)], -1)
    z_rev = jnp.concatenate([z[:1], z[:0:-1]], 0)
    z2 = jnp.pad(jnp.concatenate([z, z_rev], 0), ((0, 0), (0, LANES - HY_EMB)))
    pad = LANES - HY_FFN
    w1p = jnp.pad(w1.astype(F32), ((0, LANES - HY_EMB), (0, pad)))
    b1p = jnp.pad(b1.astype(F32), (0, pad)).reshape(1, LANES)
    w2p = jnp.pad(w2.astype(F32), ((0, pad), (0, pad)))
    b2p = jnp.pad(b2.astype(F32), (0, pad)).reshape(1, LANES)
    w3p = jnp.pad(w3.astype(F32), ((0, pad), (0, 0)))
    th = min(1024, 2 * n)
    hid = pl.pallas_call(
        _hid_kernel,
        grid=(2 * n // th,),
        in_specs=[
            pl.BlockSpec((th, LANES), lambda i: (i, 0)),
            pl.BlockSpec((LANES, LANES), lambda i: (0, 0)),
            pl.BlockSpec((1, LANES), lambda i: (0, 0)),
            pl.BlockSpec((LANES, LANES), lambda i: (0, 0)),
            pl.BlockSpec((1, LANES), lambda i: (0, 0)),
        ],
        out_specs=pl.BlockSpec((th, LANES), lambda i: (i, 0)),
        out_shape=jax.ShapeDtypeStruct((2 * n, LANES), F32),
        compiler_params=_cparams(("arbitrary",)),
        name="hyena_hidden",
    )(z2, w1p, b1p, w2p, b2p)
    decay = jnp.abs(jnp.linspace(HY_MIN_DECAY, HY_MAX_DECAY, width, dtype=F32)).reshape(1, width)
    tn = 256
    per = width // tn
    return pl.pallas_call(
        functools.partial(_filt_kernel, n=n),
        grid=(HY_ORDER, per),
        in_specs=[
            pl.BlockSpec((2 * n, LANES), lambda o, j: (0, 0)),
            pl.BlockSpec((LANES, tn), lambda o, j: (0, (2 * o) * per + j)),
            pl.BlockSpec((LANES, tn), lambda o, j: (0, (2 * o + 1) * per + j)),
            pl.BlockSpec((1, tn), lambda o, j: (0, j)),
        ],
        out_specs=pl.BlockSpec((None, 2 * n, tn), lambda o, j: (o, 0, j)),
        out_shape=jax.ShapeDtypeStruct((HY_ORDER, 2 * n, width), BF16),
        compiler_params=_cparams(("arbitrary", "arbitrary")),
        name="hyena_filters",
    )(hid, w3p, w3p, decay)


def _dft_tables(n):
    p = 2 * n
    k = jnp.arange(n, dtype=I32)[:, None]
    t = jnp.arange(p, dtype=I32)[None, :]
    ang = ((k * t) % p).astype(F32) * (2.0 * math.pi / p)
    alt = jnp.where(t % 2 == 0, 1.0, -1.0)
    fx = jnp.cos(ang)
    fy = jnp.where(k == 0, alt, -jnp.sin(ang))
    fwd = jnp.stack([fx, fy]).astype(BF16)
    tt = jnp.arange(n, dtype=I32)[:, None]
    kk = jnp.arange(n, dtype=I32)[None, :]
    ang2 = ((tt * kk) % p).astype(F32) * (2.0 * math.pi / p)
    alt2 = jnp.where(tt % 2 == 0, 1.0, -1.0)
    gx = jnp.where(kk == 0, 1.0, 2.0 * jnp.cos(ang2)) / p
    gy = jnp.where(kk == 0, alt2, -2.0 * jnp.sin(ang2)) / p
    inv = jnp.stack([gx, gy]).astype(BF16)
    return fwd, inv


def _spec_kernel(fx_ref, fy_ref, f_ref, o_ref):
    f = f_ref[...]
    o_ref[0] = jnp.dot(fx_ref[...], f, preferred_element_type=F32)
    o_ref[1] = jnp.dot(fy_ref[...], f, preferred_element_type=F32)


def _filter_spectrum(fwd, filt):
    _, n, p = fwd.shape
    width = filt.shape[2]
    tm, tn = min(512, n), 512
    return pl.pallas_call(
        _spec_kernel,
        grid=(HY_ORDER, width // tn, n // tm),
        in_specs=[
            pl.BlockSpec((None, tm, p), lambda o, j, i: (0, i, 0)),
            pl.BlockSpec((None, tm, p), lambda o, j, i: (1, i, 0)),
            pl.BlockSpec((None, p, tn), lambda o, j, i: (o, 0, j)),
        ],
        out_specs=pl.BlockSpec((None, 2, tm, tn), lambda o, j, i: (o, 0, i, j)),
        out_shape=jax.ShapeDtypeStruct((HY_ORDER, 2, n, width), F32),
        compiler_params=_cparams(("arbitrary", "arbitrary", "arbitrary")),
        name="filter_spectrum",
    )(fwd, fwd, filt)


def _dft_fwd_kernel(fx_ref, fy_ref, z_ref, hx_ref, hy_ref, o_ref):
    i = pl.program_id(2)
    z = z_ref[...]
    ux = jnp.dot(fx_ref[...], z, preferred_element_type=F32)
    uy = jnp.dot(fy_ref[...], z, preferred_element_type=F32)
    hx, hy = hx_ref[...], hy_ref[...]
    row = lax.broadcasted_iota(I32, ux.shape, 0) + i * ux.shape[0]
    dc = row == 0
    o_ref[0] = (ux * hx - jnp.where(dc, 0.0, uy * hy)).astype(o_ref.dtype)
    o_ref[1] = jnp.where(dc, uy * hy, ux * hy + uy * hx).astype(o_ref.dtype)


def _dft_forward(fwd, z, z_col0, spec, order, rows):
    n = rows.seq
    width = spec.shape[3]
    tm, tn = min(1024, n), 512
    c0 = z_col0 // tn
    return pl.pallas_call(
        _dft_fwd_kernel,
        grid=(rows.batch, width // tn, n // tm),
        in_specs=[
            pl.BlockSpec((None, tm, n), lambda b, j, i: (0, i, 0)),
            pl.BlockSpec((None, tm, n), lambda b, j, i: (1, i, 0)),
            pl.BlockSpec((n, tn), lambda b, j, i: (b, c0 + j)),
            pl.BlockSpec((None, None, tm, tn), lambda b, j, i: (order, 0, i, j)),
            pl.BlockSpec((None, None, tm, tn), lambda b, j, i: (order, 1, i, j)),
        ],
        out_specs=pl.BlockSpec((None, 2, tm, tn), lambda b, j, i: (b, 0, i, j)),
        out_shape=jax.ShapeDtypeStruct((rows.batch, 2, n, width), BF16),
        compiler_params=_cparams(("arbitrary", "arbitrary", "arbitrary")),
        name="hyena_dft_fwd",
    )(fwd, fwd, z, spec, spec)


def _dft_inv_kernel(gx_ref, gy_ref, s_ref, gate_ref, z_ref, skip_ref, o_ref):
    y = (jnp.dot(gx_ref[...], s_ref[0], preferred_element_type=F32)
         + jnp.dot(gy_ref[...], s_ref[1], preferred_element_type=F32))
    z = z_ref[...].astype(F32)
    o_ref[...] = (gate_ref[...].astype(F32) * (y + z * skip_ref[...])).astype(o_ref.dtype)


def _dft_inverse(inv, spec_z, gates, gate_col0, z, z_col0, skip, order, rows):
    n = rows.seq
    width = spec_z.shape[3]
    tm, tn = min(1024, n), 512
    nb = n // tm
    g0, z0 = gate_col0 // tn, z_col0 // tn
    return pl.pallas_call(
        _dft_inv_kernel,
        grid=(rows.batch, width // tn, nb),
        in_specs=[
            pl.BlockSpec((None, tm, n), lambda b, j, i: (0, i, 0)),
            pl.BlockSpec((None, tm, n), lambda b, j, i: (1, i, 0)),
            pl.BlockSpec((None, 2, n, tn), lambda b, j, i: (b, 0, 0, j)),
            pl.BlockSpec((tm, tn), lambda b, j, i: (b * nb + i, g0 + j)),
            pl.BlockSpec((tm, tn), lambda b, j, i: (b * nb + i, z0 + j)),
            pl.BlockSpec((None, 1, tn), lambda b, j, i: (order, 0, j)),
        ],
        out_specs=pl.BlockSpec((tm, tn), lambda b, j, i: (b * nb + i, j)),
        out_shape=jax.ShapeDtypeStruct((rows.lat, width), BF16),
        compiler_params=_cparams(("arbitrary", "arbitrary", "arbitrary")),
        name="hyena_dft_inv",
    )(inv, inv, spec_z, gates, z, skip.reshape(HY_ORDER, 1, width).astype(F32))


def _router_kernel(x_ref, sh_ref, sc_ref, rw_ref, rb_ref, u_ref, ti_ref, tg_ref, rk_ref, cnt_ref, carry_ref,
                   w_ref):
    i = pl.program_id(0)

    @pl.when(i == 0)
    def _init():
        carry_ref[...] = jnp.zeros_like(carry_ref)
        w_ref[0], w_ref[1] = _split_bf16(rw_ref[...])

    u = _rms(x_ref[...]) * (1.0 + sc_ref[...]) + sh_ref[...]
    u_ref[...] = _pack_halves(u)
    uh, ul = _split_bf16(u)
    dot = functools.partial(jnp.dot, preferred_element_type=F32)
    logits = dot(uh, w_ref[0]) + dot(ul, w_ref[0]) + dot(uh, w_ref[1]) + rb_ref[...]
    tm = logits.shape[0]
    lane = lax.broadcasted_iota(I32, logits.shape, 1).astype(F32)
    neg = jnp.float32(-jnp.inf)
    work = jnp.where(lane < N_EXPERTS, logits, neg)
    vals, idxs = [], []
    for _ in range(TOP_K):
        m = jnp.max(work, -1, keepdims=True)
        idx = jnp.min(jnp.where(work == m, lane, float(LANES)), -1, keepdims=True)
        vals.append(m)
        idxs.append(idx)
        work = jnp.where(lane == idx, neg, work)
    es = [jnp.exp(v - vals[0]) for v in vals]
    den = es[0] + es[1] + es[2] + es[3]
    onehot = jnp.zeros(logits.shape, F32)
    for idx in idxs:
        onehot = onehot + (lane == idx).astype(F32)
    r = lax.broadcasted_iota(I32, (tm, tm), 0)
    c = lax.broadcasted_iota(I32, (tm, tm), 1)
    before = (c < r).astype(BF16)
    carry = carry_ref[0:1, :]
    prefix = jnp.dot(before, onehot.astype(BF16), preferred_element_type=F32) + carry
    ti = jnp.zeros(logits.shape, F32)
    tg = jnp.zeros(logits.shape, F32)
    rk = jnp.zeros(logits.shape, F32)
    for k in range(TOP_K):
        pos = jnp.sum(jnp.where(lane == idxs[k], prefix, 0.0), -1, keepdims=True)
        ti = jnp.where(lane == k, idxs[k], ti)
        tg = jnp.where(lane == k, es[k] / den, tg)
        rk = jnp.where(lane == k, pos, rk)
    ti_ref[...] = ti
    tg_ref[...] = tg
    rk_ref[...] = rk
    total = carry + jnp.sum(onehot, 0, keepdims=True)
    carry_ref[...] = jnp.broadcast_to(total, carry_ref.shape)
    cnt_ref[...] = jnp.broadcast_to(total, cnt_ref.shape)


def _router(h, mod4, rows, layer, n_rows, router_w, router_b):
    d = h.shape[1]
    tm = 256
    rw = jnp.pad(router_w.astype(F32), ((0, 0), (0, LANES - N_EXPERTS)))
    rb = jnp.pad(router_b.astype(F32), (0, LANES - N_EXPERTS)).reshape(1, LANES)
    small = lambda dt: jax.ShapeDtypeStruct((n_rows, LANES), dt)
    return pl.pallas_call(
        _router_kernel,
        grid=(n_rows // tm,),
        in_specs=[
            pl.BlockSpec((tm, d), lambda i: (i, 0)),
            _mod_spec(rows, tm, d, layer, 3, d, 1, 0),
            _mod_spec(rows, tm, d, layer, 4, d, 1, 0),
            pl.BlockSpec((d, LANES), lambda i: (0, 0)),
            pl.BlockSpec((1, LANES), lambda i: (0, 0)),
        ],
        out_specs=[
            pl.BlockSpec((tm, d // 2), lambda i: (i, 0)),
            pl.BlockSpec((tm, LANES), lambda i: (i, 0)),
            pl.BlockSpec((tm, LANES), lambda i: (i, 0)),
            pl.BlockSpec((tm, LANES), lambda i: (i, 0)),
            pl.BlockSpec((8, LANES), lambda i: (0, 0)),
        ],
        out_shape=[jax.ShapeDtypeStruct((n_rows, d // 2), U32), small(F32), small(F32), small(F32),
                   jax.ShapeDtypeStruct((8, LANES), F32)],
        scratch_shapes=[pltpu.VMEM((8, LANES), F32), pltpu.VMEM((2, d, LANES), BF16)],
        compiler_params=_cparams(("arbitrary",)),
        name="router",
    )(h, mod4, mod4, rw, rb)


def _row_copy(src_ref, src_row, dst_ref, dst_row, sem):
    return pltpu.make_async_copy(src_ref.at[pl.ds(src_row, 1)], dst_ref.at[pl.ds(dst_row, 1)], sem)


def _block_copy(src_ref, dst_ref, sem):
    return pltpu.make_async_copy(src_ref.at[pl.ds(0, dst_ref.shape[0])], dst_ref, sem)


def _expert_kernel(be_ref, used_ref, tok_ref, tok_next_ref, src_ref, wgu_ref, bgu_ref, wd_ref, bd_ref, o_ref,
                   xbuf, sems):
    i = pl.program_id(0)
    slot = lax.rem(i, 2)
    n_used = used_ref[0]

    def gather(ids_ref, s):
        for r in range(MOE_BLK):
            _row_copy(src_ref, ids_ref[0, 0, r], xbuf.at[s], r, sems.at[s]).start()

    @pl.when(i == 0)
    def _first():
        gather(tok_ref, 0)

    @pl.when(i + 1 < n_used)
    def _ahead():
        gather(tok_next_ref, 1 - slot)

    @pl.when(i < n_used)
    def _compute():
        _block_copy(src_ref, xbuf.at[slot], sems.at[slot]).wait()
        dot = functools.partial(jnp.dot, preferred_element_type=F32)
        def gate_up(r):
            lo, hi = _unpack_halves(xbuf[slot, r:r + EXPERT_CHUNK, :])
            half = lo.shape[1]
            return (dot(lo.astype(BF16), wgu_ref[0:half, :]) + dot(hi.astype(BF16), wgu_ref[half:2 * half, :])
                    + bgu_ref[...])

        def down(r, gu):
            de = gu.shape[1] // 2
            g = jnp.minimum(gu[:, :de], SWIGLU_LIMIT)
            up = jnp.clip(gu[:, de:], -SWIGLU_LIMIT, SWIGLU_LIMIT)
            a = (up + 1.0) * g * jax.nn.sigmoid(SWIGLU_ALPHA * g)
            y = dot(a.astype(BF16), wd_ref[...]) + bd_ref[...]
            o_ref[r:r + EXPERT_CHUNK, :] = _pack_halves(y)

        pending = None
        for r in range(0, MOE_BLK, EXPERT_CHUNK):
            gu = gate_up(r)
            if pending is not None:
                down(*pending)
            pending = (r, gu)
        down(*pending)

    @pl.when(i >= n_used)
    def _empty():
        o_ref[...] = jnp.zeros(o_ref.shape, o_ref.dtype)


def _experts(u_packed, slot_tok, block_e, n_used, layer, w_gu, b_gu, w_down, b_down):
    _, n_exp, d, de2 = w_gu.shape
    de = de2 // 2
    n_blocks = block_e.shape[0]
    tok = slot_tok.reshape(n_blocks, 1, MOE_BLK)
    grid_spec = pltpu.PrefetchScalarGridSpec(
        num_scalar_prefetch=2,
        grid=(n_blocks,),
        in_specs=[
            pl.BlockSpec((1, 1, MOE_BLK), lambda i, be, nu: (i, 0, 0), memory_space=pltpu.SMEM),
            pl.BlockSpec((1, 1, MOE_BLK), lambda i, be, nu: (jnp.minimum(i + 1, n_blocks - 1), 0, 0),
                         memory_space=pltpu.SMEM),
            pl.BlockSpec(memory_space=pl.ANY),
            pl.BlockSpec((None, None, d, de2), lambda i, be, nu: (layer, be[i], 0, 0)),
            pl.BlockSpec((None, None, 1, de2), lambda i, be, nu: (layer, be[i], 0, 0)),
            pl.BlockSpec((None, None, de, d), lambda i, be, nu: (layer, be[i], 0, 0)),
            pl.BlockSpec((None, None, 1, d), lambda i, be, nu: (layer, be[i], 0, 0)),
        ],
        out_specs=pl.BlockSpec((MOE_BLK, d // 2), lambda i, be, nu: (i, 0)),
        scratch_shapes=[pltpu.VMEM((2, MOE_BLK, d // 2), U32), pltpu.SemaphoreType.DMA((2,))],
    )
    return pl.pallas_call(
        _expert_kernel,
        grid_spec=grid_spec,
        out_shape=jax.ShapeDtypeStruct((n_blocks * MOE_BLK, d // 2), U32),
        compiler_params=_cparams(("arbitrary",)),
        name="moe_experts",
    )(block_e, n_used, tok, tok, u_packed, w_gu, b_gu.reshape(-1, n_exp, 1, de2).astype(F32), w_down,
      b_down.reshape(-1, n_exp, 1, d).astype(F32))


def _combine_kernel(dest_ref, dest_next_ref, gate_ref, h_ref, g2_ref, sh_ref, sc_ref, yb_ref, *rest, final):
    if final:
        o_ref, buf, sems = rest
    else:
        o_ref, u_ref, buf, sems = rest
    n = h_ref.shape[0]
    i = pl.program_id(0)
    slot = lax.rem(i, 2)

    def gather(ids_ref, s):
        for r in range(n):
            for k in range(TOP_K):
                _row_copy(yb_ref, ids_ref[0, 0, r * TOP_K + k], buf.at[s], k * n + r, sems.at[s]).start()

    @pl.when(i == 0)
    def _first():
        gather(dest_ref, 0)

    @pl.when(i + 1 < pl.num_programs(0))
    def _ahead():
        gather(dest_next_ref, 1 - slot)

    _block_copy(yb_ref, buf.at[slot], sems.at[slot]).wait()
    gates = gate_ref[...]
    acc_lo = None
    acc_hi = None
    for k in range(TOP_K):
        lo, hi = _unpack_halves(buf[slot, k * n:(k + 1) * n, :])
        gk = gates[:, k:k + 1]
        acc_lo = gk * lo if acc_lo is None else acc_lo + gk * lo
        acc_hi = gk * hi if acc_hi is None else acc_hi + gk * hi
    out = h_ref[...] + g2_ref[...] * jnp.concatenate([acc_lo, acc_hi], -1)
    if final:
        o_ref[...] = _rms(out) * sc_ref[...]
    else:
        o_ref[...] = out
        u_ref[...] = (_rms(out) * (1.0 + sc_ref[...]) + sh_ref[...]).astype(u_ref.dtype)


def _combine(dest, gates, h, mod4, rows, layer, n_rows, yb, final_gain, final):
    d = h.shape[1]
    tm = COMBINE_BLK
    nb = n_rows // tm
    dest3 = dest.reshape(nb, 1, tm * TOP_K)
    row_spec = pl.BlockSpec((tm, d), lambda i: (i, 0))
    if final:
        gain = final_gain.reshape(1, d).astype(F32)
        mod_specs = [pl.BlockSpec((1, d), lambda i: (0, 0))] * 2
        mod_args = [gain, gain]
        out_specs, out_shape = row_spec, jax.ShapeDtypeStruct((n_rows, d), F32)
    else:
        mod_specs = [_mod_spec(rows, tm, d, layer + 1, 0, d, 1, 0), _mod_spec(rows, tm, d, layer + 1, 1, d, 1, 0)]
        mod_args = [mod4, mod4]
        out_specs = [row_spec, row_spec]
        out_shape = [jax.ShapeDtypeStruct((n_rows, d), F32), jax.ShapeDtypeStruct((n_rows, d), BF16)]
    return pl.pallas_call(
        functools.partial(_combine_kernel, final=final),
        grid=(nb,),
        in_specs=[
            pl.BlockSpec((1, 1, tm * TOP_K), lambda i: (i, 0, 0), memory_space=pltpu.SMEM),
            pl.BlockSpec((1, 1, tm * TOP_K), lambda i: (jnp.minimum(i + 1, nb - 1), 0, 0), memory_space=pltpu.SMEM),
            pl.BlockSpec((tm, LANES), lambda i: (i, 0)),
            row_spec,
            _mod_spec(rows, tm, d, layer, 5, d, 1, 0),
            *mod_specs,
            pl.BlockSpec(memory_space=pl.ANY),
        ],
        out_specs=out_specs,
        out_shape=out_shape,
        scratch_shapes=[pltpu.VMEM((2, TOP_K * tm, d // 2), U32), pltpu.SemaphoreType.DMA((2,))],
        compiler_params=_cparams(("arbitrary",)),
        name="moe_combine",
    )(dest3, dest3, gates, h, mod4, *mod_args, yb)


def _moe(h, mod4, rows, layer, n_rows, router_w, router_b, w_gu, b_gu, w_down, b_down, final_gain, final):
    u_packed, ti, tg, rk, cnt = _router(h, mod4, rows, layer, n_rows, router_w, router_b)
    counts = cnt[0, :N_EXPERTS].astype(I32)
    padded = (counts + MOE_BLK - 1) // MOE_BLK * MOE_BLK
    pad_end = jnp.cumsum(padded)
    pad_start = pad_end - padded
    top_i = ti[:, :TOP_K].astype(I32)
    dest = pad_start[top_i] + rk[:, :TOP_K].astype(I32)
    n_blocks = n_rows * TOP_K // MOE_BLK + N_EXPERTS
    tok = jnp.repeat(jnp.arange(n_rows, dtype=I32), TOP_K)
    slot_tok = jnp.zeros((n_blocks * MOE_BLK,), I32).at[dest.reshape(-1)].set(tok)
    block_start = jnp.arange(n_blocks, dtype=I32)[:, None] * MOE_BLK
    block_e = jnp.minimum(jnp.sum((pad_end[None, :] <= block_start).astype(I32), 1), N_EXPERTS - 1)
    n_used = (pad_end[-1:] // MOE_BLK).astype(I32)
    yb = _experts(u_packed, slot_tok, block_e, n_used, layer, w_gu, b_gu, w_down, b_down)
    return _combine(dest, tg, h, mod4, rows, layer, n_rows, yb, final_gain, final)


def _rope_tables(seq):
    n_rows = seq // GRID_W
    row = jnp.repeat(jnp.arange(n_rows), GRID_W).astype(F32)
    col = jnp.tile(jnp.arange(GRID_W), n_rows).astype(F32)
    quarter = MLA_ROPE_DIM // 4
    inv = 1.0 / (ROPE_THETA ** (jnp.arange(quarter, dtype=F32) / quarter))
    ang = jnp.concatenate([row[:, None] * inv, col[:, None] * inv], -1)
    cos, sin = jnp.cos(ang), jnp.sin(ang)
    return jnp.tile(cos, (1, 4)), jnp.tile(jnp.concatenate([-sin, sin], -1), (1, 2))


def kernel(x, c, ctx, c_ctx, ada_w, ada_b, mla_in_w, mla_q_norm, mla_kv_norm, mla_w_uq, mla_w_ukv, sc_conv_w, even_out_w, odd_in_w, diff_lambda, diff_subln, hy_conv_w, hy_w1, hy_b1, hy_w2, hy_b2, hy_w3, hy_skip, odd_out_w, router_w, router_b, moe_w_gu, moe_b_gu, moe_w_down, moe_b_down, final_norm):
    batch, seq, d = x.shape
    n_ctx = ctx.shape[1]
    depth = ada_w.shape[0]
    assert depth == 2 and batch + 1 <= MOD_ROWS
    rows = _Rows(batch, seq, n_ctx)
    sc_width = d - MLA_OUT
    hy_width = d - DIFF_V

    cond = jnp.zeros((MOD_ROWS, d), F32).at[:batch].set(c).at[batch].set(c_ctx)
    mod4 = _adaln(cond, ada_w, ada_b)
    cos_t, sin_t = _rope_tables(seq)
    h = jnp.concatenate([x.reshape(rows.lat, d), ctx.reshape(batch * n_ctx, d)], 0)

    kpe_col = MLA_Q_RANK + MLA_KV_RANK
    sc_col = 2048
    w_in = mla_in_w[0]
    w_in = jnp.concatenate([
        w_in[:, :kpe_col + MLA_ROPE_DIM],
        jnp.zeros((d, sc_col - kpe_col - MLA_ROPE_DIM), w_in.dtype),
        w_in[:, kpe_col + MLA_ROPE_DIM:],
    ], 1).astype(BF16)
    wq = (mla_w_uq[0] * (MLA_QK_DIM ** -0.5 * LOG2E)).reshape(MLA_Q_RANK, MLA_HEADS, MLA_QK_DIM)
    wq_pe = jnp.pad(wq[:, :, MLA_NOPE_DIM:], ((0, 0), (0, 0), (0, LANES - MLA_ROPE_DIM)))
    wq = jnp.concatenate([wq[:, :, :MLA_NOPE_DIM].reshape(MLA_Q_RANK, -1), wq_pe.reshape(MLA_Q_RANK, -1)], 1).astype(BF16)

    u = _modulate(h, mod4, rows, 0, rows.all)
    hl = _matmul([u], w_in, n_rows=rows.all, n_cols=w_in.shape[1], name="even_in_proj")
    qup = _norm_matmul(hl, 0, mla_q_norm[0], wq, rows.all, "mla_q_up")
    kvup = _norm_matmul(hl, MLA_Q_RANK, mla_kv_norm[0], mla_w_ukv[0].astype(BF16), rows.all, "mla_kv_up")
    attn = _mla_attention(qup, kvup, hl, cos_t, sin_t, rows, kpe_col)
    sc = _short_conv(hl, sc_col, sc_width, sc_conv_w[0].astype(F32), rows)
    h = _matmul([attn, sc], even_out_w[0].astype(BF16), n_rows=rows.all, n_cols=d, out_dtype=F32,
                resid=(h, mod4, rows, 0, 2), name="even_out_proj")
    w_gu_bf, w_down_bf = moe_w_gu.astype(BF16), moe_w_down.astype(BF16)
    h, u = _moe(h, mod4, rows, 0, rows.all, router_w[0], router_b[0], w_gu_bf, moe_b_gu, w_down_bf, moe_b_down,
                final_norm, False)

    lam_init = 0.8 - 0.6 * math.exp(-0.3 * 1)
    w_in1 = jnp.concatenate([odd_in_w[0][:, :DIFF_QK] * (DIFF_HEAD_DIM ** -0.5 * LOG2E), odd_in_w[0][:, DIFF_QK:]],
                            1).astype(BF16)
    hl = _matmul([u], w_in1, n_rows=rows.lat, n_cols=w_in1.shape[1], name="odd_in_proj")
    hckv = _matmul([u], w_in1, n_rows=batch * n_ctx, n_cols=DIFF_QK + DIFF_V, a_row0=rows.lat, w_col0=DIFF_QK,
                   name="odd_ctx_kv_proj")
    attn = _diff_attention(hl, hckv, cos_t, sin_t, diff_lambda[0], diff_subln[0], rows, lam_init)

    hy_col = 2 * DIFF_QK + DIFF_V
    hc3 = _hyena_conv3(hl, hy_col, 3 * hy_width, hy_conv_w[0].astype(F32), rows)
    filt = _hyena_filters(seq, hy_width, hy_w1[0], hy_b1[0], hy_w2[0], hy_b2[0], hy_w3[0])
    fwd, inv = _dft_tables(seq)
    spec = _filter_spectrum(fwd, filt)
    z, z_col = hc3, 0
    for o in range(HY_ORDER):
        spec_z = _dft_forward(fwd, z, z_col, spec, o, rows)
        z = _dft_inverse(inv, spec_z, hc3, (1 + o) * hy_width, z, z_col, hy_skip[0], o, rows)
        z_col = 0

    h = _matmul([attn, z], odd_out_w[0].astype(BF16), n_rows=rows.lat, n_cols=d, out_dtype=F32,
                resid=(h, mod4, rows, 1, 2), name="odd_out_proj")
    out = _moe(h, mod4, rows, 1, rows.lat, router_w[1], router_b[1], w_gu_bf, moe_b_gu, w_down_bf, moe_b_down,
               final_norm, True)
    return out.reshape(batch, seq, d)
```

```python
import functools
import math

import jax
import jax.numpy as jnp
from jax import lax
from jax.experimental import pallas as pl
from jax.experimental.pallas import tpu as pltpu

F32 = jnp.float32
BF16 = jnp.bfloat16
U32 = jnp.uint32
I32 = jnp.int32

GRID_W = 64
NORM_EPS = 1e-6
ROPE_THETA = 10000.0

MLA_HEADS = 16
MLA_Q_RANK = 1024
MLA_KV_RANK = 512
MLA_NOPE_DIM = 128
MLA_ROPE_DIM = 64
MLA_V_DIM = 128
MLA_QK_DIM = MLA_NOPE_DIM + MLA_ROPE_DIM
MLA_OUT = MLA_HEADS * MLA_V_DIM

DIFF_HEADS = 16
DIFF_HEAD_DIM = 64
DIFF_QK = DIFF_HEADS * 2 * DIFF_HEAD_DIM
DIFF_V = DIFF_HEADS * 2 * DIFF_HEAD_DIM

HY_ORDER = 2
HY_BANDS = 16
HY_EMB = 1 + 2 * HY_BANDS
HY_FFN = 64
HY_SHIFT = 0.05
HY_MIN_DECAY = math.log(1e-2) / 1.5
HY_MAX_DECAY = math.log(1e-2) / 0.3

N_EXPERTS = 32
TOP_K = 4
SWIGLU_LIMIT = 7.0
SWIGLU_ALPHA = 1.702

LOG2E = 1.4426950408889634
LANES = 128
MOD_ROWS = 16
VMEM_LIMIT = 56 * 1024 * 1024
MOE_BLK = 256
EXPERT_CHUNK = 128
COMBINE_BLK = 128
ATTN_CHUNK = 128
ATTN_HEADS = 2
MLA_STEP_HEADS = 4
ATTN_AHEAD = 1


def _cparams(sem):
    return pltpu.CompilerParams(dimension_semantics=sem, vmem_limit_bytes=VMEM_LIMIT)


def _rms(x):
    return x * lax.rsqrt(jnp.mean(x * x, -1, keepdims=True) + NORM_EPS)


def _split_bf16(a):
    hi = a.astype(BF16)
    lo = (a - hi.astype(F32)).astype(BF16)
    return hi, lo


def _dot3(a, b):
    ah, al = _split_bf16(a)
    bh, bl = _split_bf16(b)
    d = functools.partial(jnp.dot, preferred_element_type=F32)
    return d(ah, bh) + d(al, bh) + d(ah, bl)


def _pack_halves(y):
    c = y.shape[1] // 2
    lo = lax.bitcast_convert_type(y[:, :c].astype(BF16).astype(F32), U32)
    hi = lax.bitcast_convert_type(y[:, c:].astype(BF16).astype(F32), U32)
    return (lo >> 16) | (hi & jnp.uint32(0xFFFF0000))


def _unpack_halves(w):
    lo = lax.bitcast_convert_type(w << 16, F32)
    hi = lax.bitcast_convert_type(w & jnp.uint32(0xFFFF0000), F32)
    return lo, hi


def _ones_column(shape):
    lane = lax.broadcasted_iota(I32, shape, 1)
    return jnp.where(lane == 0, 1.0, 0.0).astype(BF16)


def _rope_pairs(x, cos_t, sin_t):
    lane = lax.broadcasted_iota(I32, x.shape, 1)
    swapped = jnp.where((lane & 63) < 32, pltpu.roll(x, 96, 1), pltpu.roll(x, 32, 1))
    return x * cos_t + swapped * sin_t


def _adaln_kernel(c_ref, w_ref, b_ref, o_ref):
    x = c_ref[...]
    s = (x * jax.nn.sigmoid(x)).astype(BF16)
    o_ref[...] = jnp.dot(s, w_ref[...].astype(BF16), preferred_element_type=F32) + b_ref[...]


def _adaln(cond, ada_w, ada_b):
    depth, d, d6 = ada_w.shape
    tn = 512
    out = pl.pallas_call(
        _adaln_kernel,
        grid=(depth, d6 // tn),
        in_specs=[
            pl.BlockSpec((MOD_ROWS, d), lambda l, j: (0, 0)),
            pl.BlockSpec((None, d, tn), lambda l, j: (l, 0, j)),
            pl.BlockSpec((None, 1, tn), lambda l, j: (l, 0, j)),
        ],
        out_specs=pl.BlockSpec((None, MOD_ROWS, tn), lambda l, j: (l, 0, j)),
        out_shape=jax.ShapeDtypeStruct((depth, MOD_ROWS, d6), F32),
        compiler_params=_cparams(("arbitrary", "arbitrary")),
        name="adaln",
    )(cond, ada_w, ada_b.reshape(depth, 1, d6))
    return out.reshape(depth, MOD_ROWS, 1, d6)


class _Rows:
    def __init__(self, batch, seq, ctx):
        self.batch, self.seq, self.ctx = batch, seq, ctx
        self.lat = batch * seq
        self.all = batch * (seq + ctx)

    def mod_row(self, i, tm):
        r = i * tm
        return jnp.where(r < self.lat, r // self.seq, self.batch)


def _mod_spec(rows, tm, tn, layer, chunk, d, grid_rank, row_axis, col_axis=None):
    per = d // tn

    def imap(*g):
        j = 0 if col_axis is None else g[col_axis]
        return (layer, rows.mod_row(g[row_axis], tm), 0, chunk * per + j)

    return pl.BlockSpec((None, None, 1, tn), imap)


def _part_specs(parts, tm, tn, row_axis, col_axis):
    specs, starts, first = [], [], 0
    for p in parts:
        nb = p.shape[0] // tm

        def imap(*g, first=first, nb=nb):
            j = 0 if col_axis is None else g[col_axis]
            return (jnp.clip(g[row_axis] - first, 0, nb - 1), j)

        specs.append(pl.BlockSpec((tm, tn), imap))
        starts.append(first)
        first += nb
    return specs, tuple(starts)


def _pick_part(refs, starts, i):
    val = refs[0][...]
    for ref, start in zip(refs[1:], starts[1:]):
        val = jnp.where(i >= start, ref[...], val)
    return val


def _modulate_kernel(*refs, starts):
    x_refs, (sh_ref, sc_ref, o_ref) = refs[:len(starts)], refs[len(starts):]
    x = _pick_part(x_refs, starts, pl.program_id(0))
    o_ref[...] = (_rms(x) * (1.0 + sc_ref[...]) + sh_ref[...]).astype(o_ref.dtype)


def _modulate(parts, mod4, rows, layer, n_rows):
    d = parts[0].shape[1]
    tm = 256
    specs, starts = _part_specs(parts, tm, d, 0, None)
    return pl.pallas_call(
        functools.partial(_modulate_kernel, starts=starts),
        grid=(n_rows // tm,),
        in_specs=specs + [
            _mod_spec(rows, tm, d, layer, 0, d, 1, 0),
            _mod_spec(rows, tm, d, layer, 1, d, 1, 0),
        ],
        out_specs=pl.BlockSpec((tm, d), lambda i: (i, 0)),
        out_shape=jax.ShapeDtypeStruct((n_rows, d), BF16),
        compiler_params=_cparams(("arbitrary",)),
        name="modulate",
    )(*parts, mod4, mod4)


def _mm_kernel(*refs, ks, h_starts):
    a_refs = refs[:len(ks)]
    w_ref = refs[len(ks)]
    o_ref = refs[-1]
    acc = None
    off = 0
    for a_ref, k in zip(a_refs, ks):
        part = jnp.dot(a_ref[...], w_ref[off:off + k, :], preferred_element_type=F32)
        acc = part if acc is None else acc + part
        off += k
    if h_starts:
        h_refs = refs[len(ks) + 1:len(ks) + 1 + len(h_starts)]
        g_ref = refs[len(ks) + 1 + len(h_starts)]
        o_ref[...] = _pick_part(h_refs, h_starts, pl.program_id(1)) + g_ref[...] * acc
    else:
        o_ref[...] = acc.astype(o_ref.dtype)


def _matmul(a_list, w, *, n_rows, n_cols, a_row0=0, w_col0=0, tm=512, tn=1024, out_dtype=BF16,
            resid=None, name="matmul"):
    tn = min(tn, n_cols)
    tm = min(tm, n_rows)
    ks = tuple(a.shape[1] for a in a_list)
    k_all = sum(ks)
    r0, c0 = a_row0 // tm, w_col0 // tn
    in_specs = [pl.BlockSpec((tm, k), lambda j, i: (r0 + i, 0)) for k in ks]
    in_specs.append(pl.BlockSpec((k_all, tn), lambda j, i: (0, c0 + j)))
    args = list(a_list) + [w]
    h_starts = ()
    if resid is not None:
        h_parts, mod4, rows, layer, chunk = resid
        h_specs, h_starts = _part_specs(h_parts, tm, tn, 1, 0)
        in_specs += h_specs
        in_specs.append(_mod_spec(rows, tm, tn, layer, chunk, h_parts[0].shape[1], 2, 1, 0))
        args += [*h_parts, mod4]
    return pl.pallas_call(
        functools.partial(_mm_kernel, ks=ks, h_starts=h_starts),
        grid=(n_cols // tn, n_rows // tm),
        in_specs=in_specs,
        out_specs=pl.BlockSpec((tm, tn), lambda j, i: (i, j)),
        out_shape=jax.ShapeDtypeStruct((n_rows, n_cols), out_dtype),
        compiler_params=_cparams(("arbitrary", "arbitrary")),
        name=name,
    )(*args)


def _norm_mm_kernel(a_ref, g_ref, w_ref, o_ref):
    y = _rms(a_ref[...].astype(F32)) * g_ref[...]
    o_ref[...] = jnp.dot(y.astype(BF16), w_ref[...], preferred_element_type=F32).astype(o_ref.dtype)


def _norm_matmul(a, a_col0, gain, w, n_rows, name):
    kw, n = w.shape
    tm = 512
    cb = a_col0 // kw
    return pl.pallas_call(
        _norm_mm_kernel,
        grid=(n_rows // tm,),
        in_specs=[
            pl.BlockSpec((tm, kw), lambda i: (i, cb)),
            pl.BlockSpec((1, kw), lambda i: (0, 0)),
            pl.BlockSpec((kw, n), lambda i: (0, 0)),
        ],
        out_specs=pl.BlockSpec((tm, n), lambda i: (i, 0)),
        out_shape=jax.ShapeDtypeStruct((n_rows, n), BF16),
        compiler_params=_cparams(("arbitrary",)),
        name=name,
    )(a, gain.reshape(1, kw).astype(F32), w)


def _mla_attn_kernel(qn_ref, qp_ref, kvl_ref, kvc_ref, kpl_ref, kpc_ref, cq_ref, sq_ref, ck_ref, sk_ref,
                     o_ref, kf_ref, vf_ref, *, seq, ctx, n_lat_blocks):
    qi = pl.program_id(2)

    @pl.when(qi == 0)
    def _build_keys():
        kpe = _rope_pairs(kpl_ref[...].astype(F32), ck_ref[...], sk_ref[...]).astype(BF16)
        kpc = kpc_ref[...]
        ones = _ones_column((seq + ctx, LANES))
        for hh in range(MLA_STEP_HEADS):
            c = 2 * LANES * hh
            kf_ref[hh, 0:seq, 0:LANES] = kvl_ref[:, c:c + LANES]
            kf_ref[hh, seq:seq + ctx, 0:LANES] = kvc_ref[:, c:c + LANES]
            vf_ref[hh, 0:seq, 0:LANES] = kvl_ref[:, c + LANES:c + 2 * LANES]
            vf_ref[hh, seq:seq + ctx, 0:LANES] = kvc_ref[:, c + LANES:c + 2 * LANES]
            vf_ref[hh, :, LANES:2 * LANES] = ones
            kf_ref[hh, 0:seq, LANES:2 * LANES] = kpe
            kf_ref[hh, seq:seq + ctx, LANES:2 * LANES] = kpc

    def attend(chains):
        def scores(hh, q, k, v):
            return lax.dot_general(q, k, (((1,), (1,)), ((), ())), preferred_element_type=F32)

        def finish(hh, r, s, v):
            e = jnp.exp2(s - jnp.max(s, -1, keepdims=True)).astype(BF16)
            ov = jnp.dot(e, v, preferred_element_type=F32)
            out = (ov[:, :LANES] / ov[:, LANES:LANES + 1]).astype(o_ref.dtype)
            o_ref[r:r + ATTN_CHUNK, hh * LANES:(hh + 1) * LANES] = out

        units = [(hh, r, q[r:r + ATTN_CHUNK], k, v) for hh, q, k, v in chains
                 for r in range(0, q.shape[0], ATTN_CHUNK)]
        pending = []
        for hh, r, q, k, v in units:
            pending.append((hh, r, scores(hh, q, k, v), v))
            if len(pending) > ATTN_AHEAD:
                finish(*pending.pop(0))
        for p in pending:
            finish(*p)

    @pl.when(qi < n_lat_blocks)
    def _latent_queries():
        chains = []
        for hh in range(MLA_STEP_HEADS):
            cols = slice(hh * LANES, (hh + 1) * LANES)
            qpe = _rope_pairs(qp_ref[:, cols].astype(F32), cq_ref[...], sq_ref[...]).astype(BF16)
            chains.append((hh, jnp.concatenate([qn_ref[:, cols], qpe], -1), kf_ref[hh], vf_ref[hh]))
        attend(chains)

    @pl.when(qi >= n_lat_blocks)
    def _context_queries():
        chains = []
        for hh in range(MLA_STEP_HEADS):
            cols = slice(hh * LANES, (hh + 1) * LANES)
            q = jnp.concatenate([qn_ref[:, cols], qp_ref[:, cols]], -1)
            chains.append((hh, q, kf_ref[hh, seq:seq + ctx, :], vf_ref[hh, seq:seq + ctx, :]))
        attend(chains)


def _mla_attention(qup, kvup, hl, cos_t, sin_t, rows, kpe_col):
    batch, seq, ctx = rows.batch, rows.seq, rows.ctx
    tq = 256
    nl, nc = seq // tq, ctx // tq
    lat_blocks = rows.lat // tq

    def qrow(b, qi):
        return jnp.where(qi < nl, b * nl + qi, lat_blocks + b * nc + (qi - nl))

    def qtab(b, h, qi):
        return (jnp.minimum(qi, nl - 1), 0)

    kb = kpe_col // LANES
    ctx0 = rows.lat // ctx
    hw = MLA_STEP_HEADS * LANES
    groups = MLA_HEADS // MLA_STEP_HEADS
    kv_scratch = pltpu.VMEM((MLA_STEP_HEADS, seq + ctx, 2 * LANES), BF16)
    return pl.pallas_call(
        functools.partial(_mla_attn_kernel, seq=seq, ctx=ctx, n_lat_blocks=nl),
        grid=(batch, groups, nl + nc),
        in_specs=[
            pl.BlockSpec((tq, hw), lambda b, h, qi: (qrow(b, qi), h)),
            pl.BlockSpec((tq, hw), lambda b, h, qi: (qrow(b, qi), groups + h)),
            pl.BlockSpec((seq, 2 * hw), lambda b, h, qi: (b, h)),
            pl.BlockSpec((ctx, 2 * hw), lambda b, h, qi: (ctx0 + b, h)),
            pl.BlockSpec((seq, LANES), lambda b, h, qi: (b, kb)),
            pl.BlockSpec((ctx, LANES), lambda b, h, qi: (ctx0 + b, kb)),
            pl.BlockSpec((tq, LANES), qtab),
            pl.BlockSpec((tq, LANES), qtab),
            pl.BlockSpec((seq, LANES), lambda b, h, qi: (0, 0)),
            pl.BlockSpec((seq, LANES), lambda b, h, qi: (0, 0)),
        ],
        out_specs=pl.BlockSpec((tq, hw), lambda b, h, qi: (qrow(b, qi), h)),
        out_shape=jax.ShapeDtypeStruct((rows.all, MLA_OUT), BF16),
        scratch_shapes=[kv_scratch, kv_scratch],
        compiler_params=_cparams(("arbitrary", "arbitrary", "arbitrary")),
        name="mla_attn",
    )(qup, qup, kvup, kvup, hl, hl, cos_t, sin_t, cos_t, sin_t)


def _conv3(p, w):
    n = p.shape[0]
    row = lax.broadcasted_iota(I32, p.shape, 0)
    prev = jnp.where(row == 0, 0.0, pltpu.roll(p, 1, 0))
    nxt = jnp.where(row == n - 1, 0.0, pltpu.roll(p, n - 1, 0))
    return prev * w[0:1] + p * w[1:2] + nxt * w[2:3]


HALO = 16


def _sconv_kernel(gb_ref, gc_ref, hh_ref, gcp_ref, hhp_ref, gcn_ref, hhn_ref, w_ref, o_ref, *, seq, ctx, lat):
    i = pl.program_id(0)
    tr = gc_ref.shape[0]
    r0 = i * tr
    length = jnp.where(r0 < lat, seq, ctx)
    pos = jnp.where(r0 < lat, r0, r0 - lat)
    starts = lax.rem(pos, length) == 0
    ends = lax.rem(pos + tr, length) == 0
    p = gc_ref[...].astype(F32) * hh_ref[...].astype(F32)
    before = gcp_ref[HALO - 1:HALO, :].astype(F32) * hhp_ref[HALO - 1:HALO, :].astype(F32)
    after = gcn_ref[0:1, :].astype(F32) * hhn_ref[0:1, :].astype(F32)
    before = jnp.where(starts, 0.0, before)
    after = jnp.where(ends, 0.0, after)
    row = lax.broadcasted_iota(I32, p.shape, 0)
    prev = jnp.where(row == 0, before, pltpu.roll(p, 1, 0))
    nxt = jnp.where(row == tr - 1, after, pltpu.roll(p, tr - 1, 0))
    w = w_ref[...]
    conv = prev * w[0:1] + p * w[1:2] + nxt * w[2:3]
    o_ref[...] = (gb_ref[...].astype(F32) * conv).astype(o_ref.dtype)


def _short_conv(hl, col0, width, conv_w, rows):
    tr, tc = 256, 1024
    assert rows.seq % tr == 0 and rows.ctx % tr == 0
    nct = width // tc
    c0 = col0 // tc
    per = tr // HALO
    last = rows.all // HALO - 1

    def main(k):
        return pl.BlockSpec((tr, tc), lambda i, j: (i, c0 + k * nct + j))

    def before(k):
        return pl.BlockSpec((HALO, tc), lambda i, j: (jnp.maximum(i * per - 1, 0), c0 + k * nct + j))

    def after(k):
        return pl.BlockSpec((HALO, tc), lambda i, j: (jnp.minimum((i + 1) * per, last), c0 + k * nct + j))

    return pl.pallas_call(
        functools.partial(_sconv_kernel, seq=rows.seq, ctx=rows.ctx, lat=rows.lat),
        grid=(rows.all // tr, nct),
        in_specs=[main(0), main(1), main(2), before(1), before(2), after(1), after(2),
                  pl.BlockSpec((3, tc), lambda i, j: (0, j))],
        out_specs=pl.BlockSpec((tr, tc), lambda i, j: (i, j)),
        out_shape=jax.ShapeDtypeStruct((rows.all, width), BF16),
        compiler_params=_cparams(("arbitrary", "arbitrary")),
        name="short_conv",
    )(hl, hl, hl, hl, hl, hl, hl, conv_w)


def _dwconv_kernel(x_ref, w_ref, o_ref):
    o_ref[...] = _conv3(x_ref[...].astype(F32), w_ref[...]).astype(o_ref.dtype)


def _hyena_conv3(hl, col0, width, conv_w, rows):
    tc = 256
    c0 = col0 // tc
    return pl.pallas_call(
        _dwconv_kernel,
        grid=(rows.batch, width // tc),
        in_specs=[
            pl.BlockSpec((rows.seq, tc), lambda s, j: (s, c0 + j)),
            pl.BlockSpec((3, tc), lambda s, j: (0, j)),
        ],
        out_specs=pl.BlockSpec((rows.seq, tc), lambda s, j: (s, j)),
        out_shape=jax.ShapeDtypeStruct((rows.lat, width), BF16),
        compiler_params=_cparams(("arbitrary", "arbitrary")),
        name="hyena_conv3",
    )(hl, conv_w)


def _diff_attn_kernel(q_ref, kl_ref, kc_ref, vl_ref, vc_ref, cq_ref, sq_ref, ck_ref, sk_ref, lam_ref, sub_ref,
                      o_ref, kf_ref, vf_ref, *, seq, ctx, lam_init):
    qi = pl.program_id(2)

    @pl.when(qi == 0)
    def _build_keys():
        ones = _ones_column((seq + ctx, LANES))
        for hh in range(ATTN_HEADS):
            cols = slice(hh * LANES, (hh + 1) * LANES)
            kf_ref[hh, 0:seq, :] = _rope_pairs(kl_ref[:, cols].astype(F32), ck_ref[...], sk_ref[...]).astype(BF16)
            kf_ref[hh, seq:seq + ctx, :] = kc_ref[:, cols]
            vf_ref[hh, 0:seq, 0:LANES] = vl_ref[:, cols]
            vf_ref[hh, seq:seq + ctx, 0:LANES] = vc_ref[:, cols]
            vf_ref[hh, :, LANES:2 * LANES] = ones

    lp = lam_ref[...]
    lam = (jnp.exp(jnp.sum(lp[0:1] * lp[1:2], -1, keepdims=True))
           - jnp.exp(jnp.sum(lp[2:3] * lp[3:4], -1, keepdims=True)) + lam_init)
    def scores(hh, r):
        cols = slice(hh * LANES, (hh + 1) * LANES)
        q = _rope_pairs(q_ref[r:r + ATTN_CHUNK, cols].astype(F32), cq_ref[r:r + ATTN_CHUNK, :],
                        sq_ref[r:r + ATTN_CHUNK, :])
        first = lax.broadcasted_iota(I32, q.shape, 1) < DIFF_HEAD_DIM
        qq = jnp.concatenate([jnp.where(first, q, 0.0), jnp.where(first, 0.0, q)], 0).astype(BF16)
        return lax.dot_general(qq, kf_ref[hh], (((1,), (1,)), ((), ())), preferred_element_type=F32)

    def finish(hh, r, s):
        e = jnp.exp2(s - jnp.max(s, -1, keepdims=True)).astype(BF16)
        ov = jnp.dot(e, vf_ref[hh], preferred_element_type=F32)
        on = ov[:, :LANES] / ov[:, LANES:LANES + 1]
        o = on[:ATTN_CHUNK] - lam * on[ATTN_CHUNK:]
        out = (_rms(o) * sub_ref[...] * (1.0 - lam_init)).astype(o_ref.dtype)
        o_ref[r:r + ATTN_CHUNK, hh * LANES:(hh + 1) * LANES] = out

    pending = []
    for hh in range(ATTN_HEADS):
        for r in range(0, q_ref.shape[0], ATTN_CHUNK):
            pending.append((hh, r, scores(hh, r)))
            if len(pending) > ATTN_AHEAD:
                finish(*pending.pop(0))
    for p in pending:
        finish(*p)


def _diff_attention(hl, hckv, cos_t, sin_t, lam_p, subln, rows, lam_init):
    batch, seq, ctx = rows.batch, rows.seq, rows.ctx
    tq = 512
    nl = seq // tq
    hw = ATTN_HEADS * LANES
    kc0 = DIFF_QK // hw
    vc0 = 2 * DIFF_QK // hw
    lam_pad = jnp.pad(lam_p.astype(F32), ((0, 0), (0, LANES - lam_p.shape[1])))
    return pl.pallas_call(
        functools.partial(_diff_attn_kernel, seq=seq, ctx=ctx, lam_init=lam_init),
        grid=(batch, DIFF_HEADS // ATTN_HEADS, nl),
        in_specs=[
            pl.BlockSpec((tq, hw), lambda b, h, qi: (b * nl + qi, h)),
            pl.BlockSpec((seq, hw), lambda b, h, qi: (b, kc0 + h)),
            pl.BlockSpec((ctx, hw), lambda b, h, qi: (b, h)),
            pl.BlockSpec((seq, hw), lambda b, h, qi: (b, vc0 + h)),
            pl.BlockSpec((ctx, hw), lambda b, h, qi: (b, kc0 + h)),
            pl.BlockSpec((tq, LANES), lambda b, h, qi: (qi, 0)),
            pl.BlockSpec((tq, LANES), lambda b, h, qi: (qi, 0)),
            pl.BlockSpec((seq, LANES), lambda b, h, qi: (0, 0)),
            pl.BlockSpec((seq, LANES), lambda b, h, qi: (0, 0)),
            pl.BlockSpec((4, LANES), lambda b, h, qi: (0, 0)),
            pl.BlockSpec((1, LANES), lambda b, h, qi: (0, 0)),
        ],
        out_specs=pl.BlockSpec((tq, hw), lambda b, h, qi: (b * nl + qi, h)),
        out_shape=jax.ShapeDtypeStruct((rows.lat, DIFF_V), BF16),
        scratch_shapes=[pltpu.VMEM((ATTN_HEADS, seq + ctx, LANES), BF16),
                        pltpu.VMEM((ATTN_HEADS, seq + ctx, 2 * LANES), BF16)],
        compiler_params=_cparams(("arbitrary", "arbitrary", "arbitrary")),
        name="diff_attn",
    )(hl, hl, hckv, hl, hckv, cos_t, sin_t, cos_t, sin_t, lam_pad, subln.reshape(1, LANES).astype(F32))


def _hid_kernel(z_ref, w1_ref, b1_ref, w2_ref, b2_ref, o_ref):
    h1 = jnp.sin(_dot3(z_ref[...], w1_ref[...]) + b1_ref[...])
    o_ref[...] = jnp.sin(_dot3(h1, w2_ref[...]) + b2_ref[...])


def _filt_kernel(hid_ref, w3f_ref, w3b_ref, dec_ref, o_ref, *, n):
    hf = hid_ref[0:n, :]
    hr = hid_ref[n:2 * n, :]
    w3f, w3b = w3f_ref[...], w3b_ref[...]
    kf = _dot3(hf, w3f)
    kb0 = _dot3(hf[0:8], w3b)[0:1]
    kbr = _dot3(hr, w3b)
    dec = dec_ref[...]
    row = lax.broadcasted_iota(I32, kf.shape, 0)
    inv = 1.0 / (n - 1)
    win_f = jnp.exp(-(row.astype(F32) * inv) * dec) + HY_SHIFT
    win_r = jnp.exp(-((n - row).astype(F32) * inv) * dec) + HY_SHIFT
    head = kf * win_f + jnp.where(row == 0, kb0 * (1.0 + HY_SHIFT), 0.0)
    tail = jnp.where(row == 0, 0.0, kbr * win_r)
    nrm = jnp.sum(jnp.abs(head), 0, keepdims=True) + jnp.sum(jnp.abs(tail), 0, keepdims=True)
    o_ref[0:n, :] = (head / nrm).astype(o_ref.dtype)
    o_ref[n:2 * n, :] = (tail / nrm).astype(o_ref.dtype)


def _hyena_filters(n, width, w1, b1, w2, b2, w3):
    t = jnp.linspace(0.0, 1.0, n, dtype=F32)[:, None]
    ang = (2.0 * math.pi / n) * jnp.arange(n, dtype=F32)[:, None] * jnp.linspace(1e-4, HY_BANDS - 1, HY_BANDS, dtype=F32)[None, :]
    z = jnp.concatenate([t, jnp.cos(ang), -jnp.sin(ang)], -1)
    z_rev = jnp.concatenate([z[:1], z[:0:-1]], 0)
    z2 = jnp.pad(jnp.concatenate([z, z_rev], 0), ((0, 0), (0, LANES - HY_EMB)))
    pad = LANES - HY_FFN
    w1p = jnp.pad(w1.astype(F32), ((0, LANES - HY_EMB), (0, pad)))
    b1p = jnp.pad(b1.astype(F32), (0, pad)).reshape(1, LANES)
    w2p = jnp.pad(w2.astype(F32), ((0, pad), (0, pad)))
    b2p = jnp.pad(b2.astype(F32), (0, pad)).reshape(1, LANES)
    w3p = jnp.pad(w3.astype(F32), ((0, pad), (0, 0)))
    th = min(1024, 2 * n)
    hid = pl.pallas_call(
        _hid_kernel,
        grid=(2 * n // th,),
        in_specs=[
            pl.BlockSpec((th, LANES), lambda i: (i, 0)),
            pl.BlockSpec((LANES, LANES), lambda i: (0, 0)),
            pl.BlockSpec((1, LANES), lambda i: (0, 0)),
            pl.BlockSpec((LANES, LANES), lambda i: (0, 0)),
            pl.BlockSpec((1, LANES), lambda i: (0, 0)),
        ],
        out_specs=pl.BlockSpec((th, LANES), lambda i: (i, 0)),
        out_shape=jax.ShapeDtypeStruct((2 * n, LANES), F32),
        compiler_params=_cparams(("arbitrary",)),
        name="hyena_hidden",
    )(z2, w1p, b1p, w2p, b2p)
    decay = jnp.abs(jnp.linspace(HY_MIN_DECAY, HY_MAX_DECAY, width, dtype=F32)).reshape(1, width)
    tn = 256
    per = width // tn
    return pl.pallas_call(
        functools.partial(_filt_kernel, n=n),
        grid=(HY_ORDER, per),
        in_specs=[
            pl.BlockSpec((2 * n, LANES), lambda o, j: (0, 0)),
            pl.BlockSpec((LANES, tn), lambda o, j: (0, (2 * o) * per + j)),
            pl.BlockSpec((LANES, tn), lambda o, j: (0, (2 * o + 1) * per + j)),
            pl.BlockSpec((1, tn), lambda o, j: (0, j)),
        ],
        out_specs=pl.BlockSpec((None, 2 * n, tn), lambda o, j: (o, 0, j)),
        out_shape=jax.ShapeDtypeStruct((HY_ORDER, 2 * n, width), BF16),
        compiler_params=_cparams(("arbitrary", "arbitrary")),
        name="hyena_filters",
    )(hid, w3p, w3p, decay)


def _dft_tables(n):
    p = 2 * n
    lo = 64
    k = jnp.arange(n, dtype=I32)[:, None]
    t1 = jnp.arange(p // lo, dtype=I32)[None, :]
    t2 = jnp.arange(lo, dtype=I32)[None, :]
    ang_a = ((k * t1 * lo) % p).astype(F32) * (2.0 * math.pi / p)
    ang_b = ((k * t2) % p).astype(F32) * (2.0 * math.pi / p)
    ca, sa = jnp.cos(ang_a)[:, :, None], jnp.sin(ang_a)[:, :, None]
    cb, sb = jnp.cos(ang_b)[:, None, :], jnp.sin(ang_b)[:, None, :]
    cos_kt = (ca * cb - sa * sb).reshape(n, p)
    sin_kt = (sa * cb + ca * sb).reshape(n, p)
    t = jnp.arange(p, dtype=I32)[None, :]
    alt = jnp.where(t % 2 == 0, 1.0, -1.0)
    fwd = jnp.stack([cos_kt, jnp.where(k == 0, alt, -sin_kt)]).astype(BF16)
    kk = t[:, :n]
    gx = jnp.where(kk == 0, 1.0, 2.0 * cos_kt[:, :n]) / p
    gy = jnp.where(kk == 0, jnp.where(k % 2 == 0, 1.0, -1.0), -2.0 * sin_kt[:, :n]) / p
    inv = jnp.stack([gx, gy]).astype(BF16)
    return fwd, inv


def _spec_kernel(fx_ref, fy_ref, f_ref, o_ref):
    f = f_ref[...]
    o_ref[0] = jnp.dot(fx_ref[...], f, preferred_element_type=F32)
    o_ref[1] = jnp.dot(fy_ref[...], f, preferred_element_type=F32)


def _filter_spectrum(fwd, filt):
    _, n, p = fwd.shape
    width = filt.shape[2]
    tm, tn = min(512, n), 512
    return pl.pallas_call(
        _spec_kernel,
        grid=(HY_ORDER, width // tn, n // tm),
        in_specs=[
            pl.BlockSpec((None, tm, p), lambda o, j, i: (0, i, 0)),
            pl.BlockSpec((None, tm, p), lambda o, j, i: (1, i, 0)),
            pl.BlockSpec((None, p, tn), lambda o, j, i: (o, 0, j)),
        ],
        out_specs=pl.BlockSpec((None, 2, tm, tn), lambda o, j, i: (o, 0, i, j)),
        out_shape=jax.ShapeDtypeStruct((HY_ORDER, 2, n, width), F32),
        compiler_params=_cparams(("arbitrary", "arbitrary", "arbitrary")),
        name="filter_spectrum",
    )(fwd, fwd, filt)


def _dft_fwd_kernel(fx_ref, fy_ref, z_ref, hx_ref, hy_ref, o_ref):
    i = pl.program_id(2)
    z = z_ref[...]
    ux = jnp.dot(fx_ref[...], z, preferred_element_type=F32)
    uy = jnp.dot(fy_ref[...], z, preferred_element_type=F32)
    hx, hy = hx_ref[...], hy_ref[...]
    row = lax.broadcasted_iota(I32, ux.shape, 0) + i * ux.shape[0]
    dc = row == 0
    o_ref[0] = (ux * hx - jnp.where(dc, 0.0, uy * hy)).astype(o_ref.dtype)
    o_ref[1] = jnp.where(dc, uy * hy, ux * hy + uy * hx).astype(o_ref.dtype)


def _dft_forward(fwd, z, z_col0, spec, order, rows):
    n = rows.seq
    width = spec.shape[3]
    tm, tn = min(1024, n), 512
    c0 = z_col0 // tn
    return pl.pallas_call(
        _dft_fwd_kernel,
        grid=(rows.batch, width // tn, n // tm),
        in_specs=[
            pl.BlockSpec((None, tm, n), lambda b, j, i: (0, i, 0)),
            pl.BlockSpec((None, tm, n), lambda b, j, i: (1, i, 0)),
            pl.BlockSpec((n, tn), lambda b, j, i: (b, c0 + j)),
            pl.BlockSpec((None, None, tm, tn), lambda b, j, i: (order, 0, i, j)),
            pl.BlockSpec((None, None, tm, tn), lambda b, j, i: (order, 1, i, j)),
        ],
        out_specs=pl.BlockSpec((None, 2, tm, tn), lambda b, j, i: (b, 0, i, j)),
        out_shape=jax.ShapeDtypeStruct((rows.batch, 2, n, width), BF16),
        compiler_params=_cparams(("arbitrary", "arbitrary", "arbitrary")),
        name="hyena_dft_fwd",
    )(fwd, fwd, z, spec, spec)


def _dft_inv_kernel(gx_ref, gy_ref, s_ref, gate_ref, z_ref, skip_ref, o_ref):
    y = (jnp.dot(gx_ref[...], s_ref[0], preferred_element_type=F32)
         + jnp.dot(gy_ref[...], s_ref[1], preferred_element_type=F32))
    z = z_ref[...].astype(F32)
    o_ref[...] = (gate_ref[...].astype(F32) * (y + z * skip_ref[...])).astype(o_ref.dtype)


def _dft_inverse(inv, spec_z, gates, gate_col0, z, z_col0, skip, order, rows):
    n = rows.seq
    width = spec_z.shape[3]
    tm, tn = min(1024, n), 512
    nb = n // tm
    g0, z0 = gate_col0 // tn, z_col0 // tn
    return pl.pallas_call(
        _dft_inv_kernel,
        grid=(rows.batch, width // tn, nb),
        in_specs=[
            pl.BlockSpec((None, tm, n), lambda b, j, i: (0, i, 0)),
            pl.BlockSpec((None, tm, n), lambda b, j, i: (1, i, 0)),
            pl.BlockSpec((None, 2, n, tn), lambda b, j, i: (b, 0, 0, j)),
            pl.BlockSpec((tm, tn), lambda b, j, i: (b * nb + i, g0 + j)),
            pl.BlockSpec((tm, tn), lambda b, j, i: (b * nb + i, z0 + j)),
            pl.BlockSpec((None, 1, tn), lambda b, j, i: (order, 0, j)),
        ],
        out_specs=pl.BlockSpec((tm, tn), lambda b, j, i: (b * nb + i, j)),
        out_shape=jax.ShapeDtypeStruct((rows.lat, width), BF16),
        compiler_params=_cparams(("arbitrary", "arbitrary", "arbitrary")),
        name="hyena_dft_inv",
    )(inv, inv, spec_z, gates, z, skip.reshape(HY_ORDER, 1, width).astype(F32))


def _router_kernel(x_ref, sh_ref, sc_ref, rw_ref, rb_ref, u_ref, ti_ref, tg_ref, rk_ref, cnt_ref, carry_ref,
                   w_ref):
    i = pl.program_id(0)

    @pl.when(i == 0)
    def _init():
        carry_ref[...] = jnp.zeros_like(carry_ref)
        w_ref[0], w_ref[1] = _split_bf16(rw_ref[...])

    u = _rms(x_ref[...]) * (1.0 + sc_ref[...]) + sh_ref[...]
    u_ref[...] = _pack_halves(u)
    uh, ul = _split_bf16(u)
    dot = functools.partial(jnp.dot, preferred_element_type=F32)
    logits = dot(uh, w_ref[0]) + dot(ul, w_ref[0]) + dot(uh, w_ref[1]) + rb_ref[...]
    tm = logits.shape[0]
    lane = lax.broadcasted_iota(I32, logits.shape, 1).astype(F32)
    neg = jnp.float32(-jnp.inf)
    work = jnp.where(lane < N_EXPERTS, logits, neg)
    vals, idxs = [], []
    for _ in range(TOP_K):
        m = jnp.max(work, -1, keepdims=True)
        idx = jnp.min(jnp.where(work == m, lane, float(LANES)), -1, keepdims=True)
        vals.append(m)
        idxs.append(idx)
        work = jnp.where(lane == idx, neg, work)
    es = [jnp.exp(v - vals[0]) for v in vals]
    den = es[0] + es[1] + es[2] + es[3]
    onehot = jnp.zeros(logits.shape, F32)
    for idx in idxs:
        onehot = onehot + (lane == idx).astype(F32)
    r = lax.broadcasted_iota(I32, (tm, tm), 0)
    c = lax.broadcasted_iota(I32, (tm, tm), 1)
    before = (c < r).astype(BF16)
    carry = carry_ref[0:1, :]
    prefix = jnp.dot(before, onehot.astype(BF16), preferred_element_type=F32) + carry
    ti = jnp.zeros(logits.shape, F32)
    tg = jnp.zeros(logits.shape, F32)
    rk = jnp.zeros(logits.shape, F32)
    for k in range(TOP_K):
        pos = jnp.sum(jnp.where(lane == idxs[k], prefix, 0.0), -1, keepdims=True)
        ti = jnp.where(lane == k, idxs[k], ti)
        tg = jnp.where(lane == k, es[k] / den, tg)
        rk = jnp.where(lane == k, pos, rk)
    ti_ref[...] = ti
    tg_ref[...] = tg
    rk_ref[...] = rk
    total = carry + jnp.sum(onehot, 0, keepdims=True)
    carry_ref[...] = jnp.broadcast_to(total, carry_ref.shape)
    cnt_ref[...] = jnp.broadcast_to(total, cnt_ref.shape)


def _router(h, mod4, rows, layer, n_rows, router_w, router_b):
    d = h.shape[1]
    tm = 256
    rw = jnp.pad(router_w.astype(F32), ((0, 0), (0, LANES - N_EXPERTS)))
    rb = jnp.pad(router_b.astype(F32), (0, LANES - N_EXPERTS)).reshape(1, LANES)
    small = lambda dt: jax.ShapeDtypeStruct((n_rows, LANES), dt)
    return pl.pallas_call(
        _router_kernel,
        grid=(n_rows // tm,),
        in_specs=[
            pl.BlockSpec((tm, d), lambda i: (i, 0)),
            _mod_spec(rows, tm, d, layer, 3, d, 1, 0),
            _mod_spec(rows, tm, d, layer, 4, d, 1, 0),
            pl.BlockSpec((d, LANES), lambda i: (0, 0)),
            pl.BlockSpec((1, LANES), lambda i: (0, 0)),
        ],
        out_specs=[
            pl.BlockSpec((tm, d // 2), lambda i: (i, 0)),
            pl.BlockSpec((tm, LANES), lambda i: (i, 0)),
            pl.BlockSpec((tm, LANES), lambda i: (i, 0)),
            pl.BlockSpec((tm, LANES), lambda i: (i, 0)),
            pl.BlockSpec((8, LANES), lambda i: (0, 0)),
        ],
        out_shape=[jax.ShapeDtypeStruct((n_rows, d // 2), U32), small(F32), small(F32), small(F32),
                   jax.ShapeDtypeStruct((8, LANES), F32)],
        scratch_shapes=[pltpu.VMEM((8, LANES), F32), pltpu.VMEM((2, d, LANES), BF16)],
        compiler_params=_cparams(("arbitrary",)),
        name="router",
    )(h, mod4, mod4, rw, rb)


def _row_copy(src_ref, src_row, dst_ref, dst_row, sem):
    return pltpu.make_async_copy(src_ref.at[pl.ds(src_row, 1)], dst_ref.at[pl.ds(dst_row, 1)], sem)


def _block_copy(src_ref, dst_ref, sem):
    return pltpu.make_async_copy(src_ref.at[pl.ds(0, dst_ref.shape[0])], dst_ref, sem)


def _expert_kernel(be_ref, used_ref, tok_ref, tok_next_ref, src_ref, wgu_ref, bgu_ref, wd_ref, bd_ref, o_ref,
                   xbuf, sems):
    i = pl.program_id(0)
    slot = lax.rem(i, 2)
    n_used = used_ref[0]

    def gather(ids_ref, s):
        for r in range(MOE_BLK):
            _row_copy(src_ref, ids_ref[0, 0, r], xbuf.at[s], r, sems.at[s]).start()

    @pl.when(i == 0)
    def _first():
        gather(tok_ref, 0)

    @pl.when(i + 1 < n_used)
    def _ahead():
        gather(tok_next_ref, 1 - slot)

    @pl.when(i < n_used)
    def _compute():
        _block_copy(src_ref, xbuf.at[slot], sems.at[slot]).wait()
        dot = functools.partial(jnp.dot, preferred_element_type=F32)
        def gate_up(r):
            lo, hi = _unpack_halves(xbuf[slot, r:r + EXPERT_CHUNK, :])
            half = lo.shape[1]
            return (dot(lo.astype(BF16), wgu_ref[0:half, :]) + dot(hi.astype(BF16), wgu_ref[half:2 * half, :])
                    + bgu_ref[...])

        def down(r, gu):
            de = gu.shape[1] // 2
            g = jnp.minimum(gu[:, :de], SWIGLU_LIMIT)
            up = jnp.clip(gu[:, de:], -SWIGLU_LIMIT, SWIGLU_LIMIT)
            a = (up + 1.0) * g * jax.nn.sigmoid(SWIGLU_ALPHA * g)
            y = dot(a.astype(BF16), wd_ref[...]) + bd_ref[...]
            o_ref[r:r + EXPERT_CHUNK, :] = _pack_halves(y)

        pending = None
        for r in range(0, MOE_BLK, EXPERT_CHUNK):
            gu = gate_up(r)
            if pending is not None:
                down(*pending)
            pending = (r, gu)
        down(*pending)

    @pl.when(i >= n_used)
    def _empty():
        o_ref[...] = jnp.zeros(o_ref.shape, o_ref.dtype)


def _experts(u_packed, slot_tok, block_e, n_used, layer, w_gu, b_gu, w_down, b_down):
    _, n_exp, d, de2 = w_gu.shape
    de = de2 // 2
    n_blocks = block_e.shape[0]
    tok = slot_tok.reshape(n_blocks, 1, MOE_BLK)
    grid_spec = pltpu.PrefetchScalarGridSpec(
        num_scalar_prefetch=2,
        grid=(n_blocks,),
        in_specs=[
            pl.BlockSpec((1, 1, MOE_BLK), lambda i, be, nu: (i, 0, 0), memory_space=pltpu.SMEM),
            pl.BlockSpec((1, 1, MOE_BLK), lambda i, be, nu: (jnp.minimum(i + 1, n_blocks - 1), 0, 0),
                         memory_space=pltpu.SMEM),
            pl.BlockSpec(memory_space=pl.ANY),
            pl.BlockSpec((None, None, d, de2), lambda i, be, nu: (layer, be[i], 0, 0)),
            pl.BlockSpec((None, None, 1, de2), lambda i, be, nu: (layer, be[i], 0, 0)),
            pl.BlockSpec((None, None, de, d), lambda i, be, nu: (layer, be[i], 0, 0)),
            pl.BlockSpec((None, None, 1, d), lambda i, be, nu: (layer, be[i], 0, 0)),
        ],
        out_specs=pl.BlockSpec((MOE_BLK, d // 2), lambda i, be, nu: (i, 0)),
        scratch_shapes=[pltpu.VMEM((2, MOE_BLK, d // 2), U32), pltpu.SemaphoreType.DMA((2,))],
    )
    return pl.pallas_call(
        _expert_kernel,
        grid_spec=grid_spec,
        out_shape=jax.ShapeDtypeStruct((n_blocks * MOE_BLK, d // 2), U32),
        compiler_params=_cparams(("arbitrary",)),
        name="moe_experts",
    )(block_e, n_used, tok, tok, u_packed, w_gu, b_gu.reshape(-1, n_exp, 1, de2).astype(F32), w_down,
      b_down.reshape(-1, n_exp, 1, d).astype(F32))


def _combine_kernel(dest_ref, dest_next_ref, gate_ref, h_ref, g2_ref, sh_ref, sc_ref, yb_ref, *rest, final):
    if final:
        o_ref, buf, sems = rest
    else:
        o_ref, u_ref, buf, sems = rest
    n = h_ref.shape[0]
    i = pl.program_id(0)
    slot = lax.rem(i, 2)

    def gather(ids_ref, s):
        for r in range(n):
            for k in range(TOP_K):
                _row_copy(yb_ref, ids_ref[0, 0, r * TOP_K + k], buf.at[s], k * n + r, sems.at[s]).start()

    @pl.when(i == 0)
    def _first():
        gather(dest_ref, 0)

    @pl.when(i + 1 < pl.num_programs(0))
    def _ahead():
        gather(dest_next_ref, 1 - slot)

    _block_copy(yb_ref, buf.at[slot], sems.at[slot]).wait()
    gates = gate_ref[...]
    acc_lo = None
    acc_hi = None
    for k in range(TOP_K):
        lo, hi = _unpack_halves(buf[slot, k * n:(k + 1) * n, :])
        gk = gates[:, k:k + 1]
        acc_lo = gk * lo if acc_lo is None else acc_lo + gk * lo
        acc_hi = gk * hi if acc_hi is None else acc_hi + gk * hi
    out = h_ref[...] + g2_ref[...] * jnp.concatenate([acc_lo, acc_hi], -1)
    if final:
        o_ref[...] = _rms(out) * sc_ref[...]
    else:
        o_ref[...] = out
        u_ref[...] = (_rms(out) * (1.0 + sc_ref[...]) + sh_ref[...]).astype(u_ref.dtype)


def _combine(dest, gates, h, mod4, rows, layer, n_rows, yb, final_gain, final):
    d = h.shape[1]
    tm = COMBINE_BLK
    nb = n_rows // tm
    dest3 = dest.reshape(nb, 1, tm * TOP_K)
    row_spec = pl.BlockSpec((tm, d), lambda i: (i, 0))
    if final:
        gain = final_gain.reshape(1, d).astype(F32)
        mod_specs = [pl.BlockSpec((1, d), lambda i: (0, 0))] * 2
        mod_args = [gain, gain]
        out_specs, out_shape = row_spec, jax.ShapeDtypeStruct((n_rows, d), F32)
    else:
        mod_specs = [_mod_spec(rows, tm, d, layer + 1, 0, d, 1, 0), _mod_spec(rows, tm, d, layer + 1, 1, d, 1, 0)]
        mod_args = [mod4, mod4]
        out_specs = [row_spec, row_spec]
        out_shape = [jax.ShapeDtypeStruct((n_rows, d), F32), jax.ShapeDtypeStruct((n_rows, d), BF16)]
    return pl.pallas_call(
        functools.partial(_combine_kernel, final=final),
        grid=(nb,),
        in_specs=[
            pl.BlockSpec((1, 1, tm * TOP_K), lambda i: (i, 0, 0), memory_space=pltpu.SMEM),
            pl.BlockSpec((1, 1, tm * TOP_K), lambda i: (jnp.minimum(i + 1, nb - 1), 0, 0), memory_space=pltpu.SMEM),
            pl.BlockSpec((tm, LANES), lambda i: (i, 0)),
            row_spec,
            _mod_spec(rows, tm, d, layer, 5, d, 1, 0),
            *mod_specs,
            pl.BlockSpec(memory_space=pl.ANY),
        ],
        out_specs=out_specs,
        out_shape=out_shape,
        scratch_shapes=[pltpu.VMEM((2, TOP_K * tm, d // 2), U32), pltpu.SemaphoreType.DMA((2,))],
        compiler_params=_cparams(("arbitrary",)),
        name="moe_combine",
    )(dest3, dest3, gates, h, mod4, *mod_args, yb)


def _moe(h, mod4, rows, layer, n_rows, router_w, router_b, w_gu, b_gu, w_down, b_down, final_gain, final):
    u_packed, ti, tg, rk, cnt = _router(h, mod4, rows, layer, n_rows, router_w, router_b)
    counts = cnt[0, :N_EXPERTS].astype(I32)
    padded = (counts + MOE_BLK - 1) // MOE_BLK * MOE_BLK
    pad_end = jnp.cumsum(padded)
    pad_start = pad_end - padded
    top_i = ti[:, :TOP_K].astype(I32)
    dest = pad_start[top_i] + rk[:, :TOP_K].astype(I32)
    n_blocks = n_rows * TOP_K // MOE_BLK + N_EXPERTS
    tok = jnp.repeat(jnp.arange(n_rows, dtype=I32), TOP_K)
    slot_tok = jnp.zeros((n_blocks * MOE_BLK,), I32).at[dest.reshape(-1)].set(tok)
    block_start = jnp.arange(n_blocks, dtype=I32)[:, None] * MOE_BLK
    block_e = jnp.minimum(jnp.sum((pad_end[None, :] <= block_start).astype(I32), 1), N_EXPERTS - 1)
    n_used = (pad_end[-1:] // MOE_BLK).astype(I32)
    yb = _experts(u_packed, slot_tok, block_e, n_used, layer, w_gu, b_gu, w_down, b_down)
    return _combine(dest, tg, h, mod4, rows, layer, n_rows, yb, final_gain, final)


def _rope_tables(seq):
    n_rows = seq // GRID_W
    row = jnp.repeat(jnp.arange(n_rows), GRID_W).astype(F32)
    col = jnp.tile(jnp.arange(GRID_W), n_rows).astype(F32)
    quarter = MLA_ROPE_DIM // 4
    inv = 1.0 / (ROPE_THETA ** (jnp.arange(quarter, dtype=F32) / quarter))
    ang = jnp.concatenate([row[:, None] * inv, col[:, None] * inv], -1)
    cos, sin = jnp.cos(ang), jnp.sin(ang)
    return jnp.tile(cos, (1, 4)), jnp.tile(jnp.concatenate([-sin, sin], -1), (1, 2))


def kernel(x, c, ctx, c_ctx, ada_w, ada_b, mla_in_w, mla_q_norm, mla_kv_norm, mla_w_uq, mla_w_ukv, sc_conv_w, even_out_w, odd_in_w, diff_lambda, diff_subln, hy_conv_w, hy_w1, hy_b1, hy_w2, hy_b2, hy_w3, hy_skip, odd_out_w, router_w, router_b, moe_w_gu, moe_b_gu, moe_w_down, moe_b_down, final_norm):
    batch, seq, d = x.shape
    n_ctx = ctx.shape[1]
    depth = ada_w.shape[0]
    assert depth == 2 and batch + 1 <= MOD_ROWS
    rows = _Rows(batch, seq, n_ctx)
    sc_width = d - MLA_OUT
    hy_width = d - DIFF_V

    cond = jnp.zeros((MOD_ROWS, d), F32).at[:batch].set(c).at[batch].set(c_ctx)
    mod4 = _adaln(cond, ada_w, ada_b)
    cos_t, sin_t = _rope_tables(seq)
    h0 = [x.reshape(rows.lat, d), ctx.reshape(batch * n_ctx, d)]

    kpe_col = MLA_Q_RANK + MLA_KV_RANK
    lat_cols = kpe_col + MLA_ROPE_DIM
    lat_pad = -lat_cols % LANES
    w_lat = jnp.pad(mla_in_w[0][:, :lat_cols], ((0, 0), (0, lat_pad))).astype(BF16)
    w_sc = mla_in_w[0][:, lat_cols:].astype(BF16)
    wq = (mla_w_uq[0] * (MLA_QK_DIM ** -0.5 * LOG2E)).reshape(MLA_Q_RANK, MLA_HEADS, MLA_QK_DIM)
    wq_pe = jnp.pad(wq[:, :, MLA_NOPE_DIM:], ((0, 0), (0, 0), (0, LANES - MLA_ROPE_DIM)))
    wq = jnp.concatenate([wq[:, :, :MLA_NOPE_DIM].reshape(MLA_Q_RANK, -1), wq_pe.reshape(MLA_Q_RANK, -1)], 1).astype(BF16)

    u = _modulate(h0, mod4, rows, 0, rows.all)
    hl = _matmul([u], w_lat, n_rows=rows.all, n_cols=w_lat.shape[1], tn=w_lat.shape[1], name="even_in_proj_latent")
    hs = _matmul([u], w_sc, n_rows=rows.all, n_cols=w_sc.shape[1], name="even_in_proj_conv")
    qup = _norm_matmul(hl, 0, mla_q_norm[0], wq, rows.all, "mla_q_up")
    kvup = _norm_matmul(hl, MLA_Q_RANK, mla_kv_norm[0], mla_w_ukv[0].astype(BF16), rows.all, "mla_kv_up")
    attn = _mla_attention(qup, kvup, hl, cos_t, sin_t, rows, kpe_col)
    sc = _short_conv(hs, 0, sc_width, sc_conv_w[0].astype(F32), rows)
    h = _matmul([attn, sc], even_out_w[0].astype(BF16), n_rows=rows.all, n_cols=d, out_dtype=F32,
                resid=(h0, mod4, rows, 0, 2), name="even_out_proj")
    w_gu_bf, w_down_bf = moe_w_gu.astype(BF16), moe_w_down.astype(BF16)
    h, u = _moe(h, mod4, rows, 0, rows.all, router_w[0], router_b[0], w_gu_bf, moe_b_gu, w_down_bf, moe_b_down,
                final_norm, False)

    lam_init = 0.8 - 0.6 * math.exp(-0.3 * 1)
    q_scale = jnp.where(jnp.arange(odd_in_w.shape[2]) < DIFF_QK, DIFF_HEAD_DIM ** -0.5 * LOG2E, 1.0).astype(F32)
    w_in1 = (odd_in_w[0] * q_scale[None, :]).astype(BF16)
    hl = _matmul([u], w_in1, n_rows=rows.lat, n_cols=w_in1.shape[1], name="odd_in_proj")
    hckv = _matmul([u], w_in1, n_rows=batch * n_ctx, n_cols=DIFF_QK + DIFF_V, a_row0=rows.lat, w_col0=DIFF_QK,
                   name="odd_ctx_kv_proj")
    attn = _diff_attention(hl, hckv, cos_t, sin_t, diff_lambda[0], diff_subln[0], rows, lam_init)

    hy_col = 2 * DIFF_QK + DIFF_V
    hc3 = _hyena_conv3(hl, hy_col, 3 * hy_width, hy_conv_w[0].astype(F32), rows)
    filt = _hyena_filters(seq, hy_width, hy_w1[0], hy_b1[0], hy_w2[0], hy_b2[0], hy_w3[0])
    fwd, inv = _dft_tables(seq)
    spec = _filter_spectrum(fwd, filt)
    z, z_col = hc3, 0
    for o in range(HY_ORDER):
        spec_z = _dft_forward(fwd, z, z_col, spec, o, rows)
        z = _dft_inverse(inv, spec_z, hc3, (1 + o) * hy_width, z, z_col, hy_skip[0], o, rows)
        z_col = 0

    h = _matmul([attn, z], odd_out_w[0].astype(BF16), n_rows=rows.lat, n_cols=d, out_dtype=F32,
                resid=([h], mod4, rows, 1, 2), name="odd_out_proj")
    out = _moe(h, mod4, rows, 1, rows.lat, router_w[1], router_b[1], w_gu_bf, moe_b_gu, w_down_bf, moe_b_down,
               final_norm, True)
    return out.reshape(batch, seq, d)
```

```python
import functools
import math

import jax
import jax.numpy as jnp
from jax import lax
from jax.experimental import pallas as pl
from jax.experimental.pallas import tpu as pltpu

F32 = jnp.float32
BF16 = jnp.bfloat16
U32 = jnp.uint32
I32 = jnp.int32

GRID_W = 64
NORM_EPS = 1e-6
ROPE_THETA = 10000.0

MLA_HEADS = 16
MLA_Q_RANK = 1024
MLA_KV_RANK = 512
MLA_NOPE_DIM = 128
MLA_ROPE_DIM = 64
MLA_V_DIM = 128
MLA_QK_DIM = MLA_NOPE_DIM + MLA_ROPE_DIM
MLA_OUT = MLA_HEADS * MLA_V_DIM

DIFF_HEADS = 16
DIFF_HEAD_DIM = 64
DIFF_QK = DIFF_HEADS * 2 * DIFF_HEAD_DIM
DIFF_V = DIFF_HEADS * 2 * DIFF_HEAD_DIM

HY_ORDER = 2
HY_BANDS = 16
HY_EMB = 1 + 2 * HY_BANDS
HY_FFN = 64
HY_SHIFT = 0.05
HY_MIN_DECAY = math.log(1e-2) / 1.5
HY_MAX_DECAY = math.log(1e-2) / 0.3

N_EXPERTS = 32
TOP_K = 4
SWIGLU_LIMIT = 7.0
SWIGLU_ALPHA = 1.702

LOG2E = 1.4426950408889634
LANES = 128
MOD_ROWS = 16
VMEM_LIMIT = 56 * 1024 * 1024
MOE_BLK = 256
EXPERT_CHUNK = 128
COMBINE_BLK = 128
ATTN_CHUNK = 128
ATTN_HEADS = 4
MLA_STEP_HEADS = 8
ATTN_AHEAD = 1


def _cparams(sem):
    return pltpu.CompilerParams(dimension_semantics=sem, vmem_limit_bytes=VMEM_LIMIT)


def _rms(x):
    return x * lax.rsqrt(jnp.mean(x * x, -1, keepdims=True) + NORM_EPS)


def _split_bf16(a):
    hi = a.astype(BF16)
    lo = (a - hi.astype(F32)).astype(BF16)
    return hi, lo


def _dot3(a, b):
    ah, al = _split_bf16(a)
    bh, bl = _split_bf16(b)
    d = functools.partial(jnp.dot, preferred_element_type=F32)
    return d(ah, bh) + d(al, bh) + d(ah, bl)


def _pack_halves(y):
    c = y.shape[1] // 2
    lo = lax.bitcast_convert_type(y[:, :c].astype(BF16).astype(F32), U32)
    hi = lax.bitcast_convert_type(y[:, c:].astype(BF16).astype(F32), U32)
    return (lo >> 16) | (hi & jnp.uint32(0xFFFF0000))


def _unpack_halves(w):
    lo = lax.bitcast_convert_type(w << 16, F32)
    hi = lax.bitcast_convert_type(w & jnp.uint32(0xFFFF0000), F32)
    return lo, hi


def _ones_column(shape):
    lane = lax.broadcasted_iota(I32, shape, 1)
    return jnp.where(lane == 0, 1.0, 0.0).astype(BF16)


def _rope_pairs(x, cos_t, sin_t):
    lane = lax.broadcasted_iota(I32, x.shape, 1)
    swapped = jnp.where((lane & 63) < 32, pltpu.roll(x, 96, 1), pltpu.roll(x, 32, 1))
    return x * cos_t + swapped * sin_t


def _adaln_kernel(c_ref, w_ref, b_ref, o_ref):
    x = c_ref[...]
    s = (x * jax.nn.sigmoid(x)).astype(BF16)
    o_ref[...] = jnp.dot(s, w_ref[...].astype(BF16), preferred_element_type=F32) + b_ref[...]


def _adaln(cond, ada_w, ada_b):
    depth, d, d6 = ada_w.shape
    tn = 512
    out = pl.pallas_call(
        _adaln_kernel,
        grid=(depth, d6 // tn),
        in_specs=[
            pl.BlockSpec((MOD_ROWS, d), lambda l, j: (0, 0)),
            pl.BlockSpec((None, d, tn), lambda l, j: (l, 0, j)),
            pl.BlockSpec((None, 1, tn), lambda l, j: (l, 0, j)),
        ],
        out_specs=pl.BlockSpec((None, MOD_ROWS, tn), lambda l, j: (l, 0, j)),
        out_shape=jax.ShapeDtypeStruct((depth, MOD_ROWS, d6), F32),
        compiler_params=_cparams(("arbitrary", "arbitrary")),
        name="adaln",
    )(cond, ada_w, ada_b.reshape(depth, 1, d6))
    return out.reshape(depth, MOD_ROWS, 1, d6)


class _Rows:
    def __init__(self, batch, seq, ctx):
        self.batch, self.seq, self.ctx = batch, seq, ctx
        self.lat = batch * seq
        self.all = batch * (seq + ctx)

    def mod_row(self, i, tm):
        r = i * tm
        return jnp.where(r < self.lat, r // self.seq, self.batch)


def _mod_spec(rows, tm, tn, layer, chunk, d, grid_rank, row_axis, col_axis=None):
    per = d // tn

    def imap(*g):
        j = 0 if col_axis is None else g[col_axis]
        return (layer, rows.mod_row(g[row_axis], tm), 0, chunk * per + j)

    return pl.BlockSpec((None, None, 1, tn), imap)


def _part_specs(parts, tm, tn, row_axis, col_axis):
    specs, starts, first = [], [], 0
    for p in parts:
        nb = p.shape[0] // tm

        def imap(*g, first=first, nb=nb):
            j = 0 if col_axis is None else g[col_axis]
            return (jnp.clip(g[row_axis] - first, 0, nb - 1), j)

        specs.append(pl.BlockSpec((tm, tn), imap))
        starts.append(first)
        first += nb
    return specs, tuple(starts)


def _pick_part(refs, starts, i):
    val = refs[0][...]
    for ref, start in zip(refs[1:], starts[1:]):
        val = jnp.where(i >= start, ref[...], val)
    return val


def _modulate_kernel(*refs, starts):
    x_refs, (sh_ref, sc_ref, o_ref) = refs[:len(starts)], refs[len(starts):]
    x = _pick_part(x_refs, starts, pl.program_id(0))
    o_ref[...] = (_rms(x) * (1.0 + sc_ref[...]) + sh_ref[...]).astype(o_ref.dtype)


def _modulate(parts, mod4, rows, layer, n_rows):
    d = parts[0].shape[1]
    tm = 256
    specs, starts = _part_specs(parts, tm, d, 0, None)
    return pl.pallas_call(
        functools.partial(_modulate_kernel, starts=starts),
        grid=(n_rows // tm,),
        in_specs=specs + [
            _mod_spec(rows, tm, d, layer, 0, d, 1, 0),
            _mod_spec(rows, tm, d, layer, 1, d, 1, 0),
        ],
        out_specs=pl.BlockSpec((tm, d), lambda i: (i, 0)),
        out_shape=jax.ShapeDtypeStruct((n_rows, d), BF16),
        compiler_params=_cparams(("arbitrary",)),
        name="modulate",
    )(*parts, mod4, mod4)


def _mm_kernel(*refs, ks, h_starts):
    a_refs = refs[:len(ks)]
    w_ref = refs[len(ks)]
    o_ref = refs[-1]
    acc = None
    off = 0
    for a_ref, k in zip(a_refs, ks):
        part = jnp.dot(a_ref[...], w_ref[off:off + k, :], preferred_element_type=F32)
        acc = part if acc is None else acc + part
        off += k
    if h_starts:
        h_refs = refs[len(ks) + 1:len(ks) + 1 + len(h_starts)]
        g_ref = refs[len(ks) + 1 + len(h_starts)]
        o_ref[...] = _pick_part(h_refs, h_starts, pl.program_id(1)) + g_ref[...] * acc
    else:
        o_ref[...] = acc.astype(o_ref.dtype)


def _matmul(a_list, w, *, n_rows, n_cols, a_row0=0, w_col0=0, tm=512, tn=1024, out_dtype=BF16,
            resid=None, name="matmul"):
    tn = min(tn, n_cols)
    tm = min(tm, n_rows)
    ks = tuple(a.shape[1] for a in a_list)
    k_all = sum(ks)
    r0, c0 = a_row0 // tm, w_col0 // tn
    in_specs = [pl.BlockSpec((tm, k), lambda j, i: (r0 + i, 0)) for k in ks]
    in_specs.append(pl.BlockSpec((k_all, tn), lambda j, i: (0, c0 + j)))
    args = list(a_list) + [w]
    h_starts = ()
    if resid is not None:
        h_parts, mod4, rows, layer, chunk = resid
        h_specs, h_starts = _part_specs(h_parts, tm, tn, 1, 0)
        in_specs += h_specs
        in_specs.append(_mod_spec(rows, tm, tn, layer, chunk, h_parts[0].shape[1], 2, 1, 0))
        args += [*h_parts, mod4]
    return pl.pallas_call(
        functools.partial(_mm_kernel, ks=ks, h_starts=h_starts),
        grid=(n_cols // tn, n_rows // tm),
        in_specs=in_specs,
        out_specs=pl.BlockSpec((tm, tn), lambda j, i: (i, j)),
        out_shape=jax.ShapeDtypeStruct((n_rows, n_cols), out_dtype),
        compiler_params=_cparams(("arbitrary", "arbitrary")),
        name=name,
    )(*args)


def _norm_mm_kernel(a_ref, g_ref, w_ref, o_ref):
    y = _rms(a_ref[...].astype(F32)) * g_ref[...]
    o_ref[...] = jnp.dot(y.astype(BF16), w_ref[...], preferred_element_type=F32).astype(o_ref.dtype)


def _norm_matmul(a, a_col0, gain, w, n_rows, name):
    kw, n = w.shape
    tm = 512
    cb = a_col0 // kw
    return pl.pallas_call(
        _norm_mm_kernel,
        grid=(n_rows // tm,),
        in_specs=[
            pl.BlockSpec((tm, kw), lambda i: (i, cb)),
            pl.BlockSpec((1, kw), lambda i: (0, 0)),
            pl.BlockSpec((kw, n), lambda i: (0, 0)),
        ],
        out_specs=pl.BlockSpec((tm, n), lambda i: (i, 0)),
        out_shape=jax.ShapeDtypeStruct((n_rows, n), BF16),
        compiler_params=_cparams(("arbitrary",)),
        name=name,
    )(a, gain.reshape(1, kw).astype(F32), w)


def _mla_attn_kernel(qn_ref, qp_ref, kvl_ref, kvc_ref, kpl_ref, kpc_ref, cq_ref, sq_ref, ck_ref, sk_ref,
                     o_ref, kf_ref, vf_ref, *, seq, ctx, n_lat_blocks):
    qi = pl.program_id(2)

    @pl.when(qi == 0)
    def _build_keys():
        kpe = _rope_pairs(kpl_ref[...].astype(F32), ck_ref[...], sk_ref[...]).astype(BF16)
        kpc = kpc_ref[...]
        ones = _ones_column((seq + ctx, LANES))
        for hh in range(MLA_STEP_HEADS):
            c = 2 * LANES * hh
            kf_ref[hh, 0:seq, 0:LANES] = kvl_ref[:, c:c + LANES]
            kf_ref[hh, seq:seq + ctx, 0:LANES] = kvc_ref[:, c:c + LANES]
            vf_ref[hh, 0:seq, 0:LANES] = kvl_ref[:, c + LANES:c + 2 * LANES]
            vf_ref[hh, seq:seq + ctx, 0:LANES] = kvc_ref[:, c + LANES:c + 2 * LANES]
            vf_ref[hh, :, LANES:2 * LANES] = ones
            kf_ref[hh, 0:seq, LANES:2 * LANES] = kpe
            kf_ref[hh, seq:seq + ctx, LANES:2 * LANES] = kpc

    def attend(chains):
        def scores(hh, q, k, v):
            return lax.dot_general(q, k, (((1,), (1,)), ((), ())), preferred_element_type=F32)

        def finish(hh, r, s, v):
            e = jnp.exp2(s - jnp.max(s, -1, keepdims=True)).astype(BF16)
            ov = jnp.dot(e, v, preferred_element_type=F32)
            out = (ov[:, :LANES] / ov[:, LANES:LANES + 1]).astype(o_ref.dtype)
            o_ref[r:r + ATTN_CHUNK, hh * LANES:(hh + 1) * LANES] = out

        units = [(hh, r, q[r:r + ATTN_CHUNK], k, v) for hh, q, k, v in chains
                 for r in range(0, q.shape[0], ATTN_CHUNK)]
        pending = []
        for hh, r, q, k, v in units:
            pending.append((hh, r, scores(hh, q, k, v), v))
            if len(pending) > ATTN_AHEAD:
                finish(*pending.pop(0))
        for p in pending:
            finish(*p)

    @pl.when(qi < n_lat_blocks)
    def _latent_queries():
        chains = []
        for hh in range(MLA_STEP_HEADS):
            cols = slice(hh * LANES, (hh + 1) * LANES)
            qpe = _rope_pairs(qp_ref[:, cols].astype(F32), cq_ref[...], sq_ref[...]).astype(BF16)
            chains.append((hh, jnp.concatenate([qn_ref[:, cols], qpe], -1), kf_ref[hh], vf_ref[hh]))
        attend(chains)

    @pl.when(qi >= n_lat_blocks)
    def _context_queries():
        chains = []
        for hh in range(MLA_STEP_HEADS):
            cols = slice(hh * LANES, (hh + 1) * LANES)
            q = jnp.concatenate([qn_ref[:, cols], qp_ref[:, cols]], -1)
            chains.append((hh, q, kf_ref[hh, seq:seq + ctx, :], vf_ref[hh, seq:seq + ctx, :]))
        attend(chains)


def _mla_attention(qup, kvup, hl, cos_t, sin_t, rows, kpe_col):
    batch, seq, ctx = rows.batch, rows.seq, rows.ctx
    tq = 256
    nl, nc = seq // tq, ctx // tq
    lat_blocks = rows.lat // tq

    def qrow(b, qi):
        return jnp.where(qi < nl, b * nl + qi, lat_blocks + b * nc + (qi - nl))

    def qtab(b, h, qi):
        return (jnp.minimum(qi, nl - 1), 0)

    kb = kpe_col // LANES
    ctx0 = rows.lat // ctx
    hw = MLA_STEP_HEADS * LANES
    groups = MLA_HEADS // MLA_STEP_HEADS
    kv_scratch = pltpu.VMEM((MLA_STEP_HEADS, seq + ctx, 2 * LANES), BF16)
    return pl.pallas_call(
        functools.partial(_mla_attn_kernel, seq=seq, ctx=ctx, n_lat_blocks=nl),
        grid=(batch, groups, nl + nc),
        in_specs=[
            pl.BlockSpec((tq, hw), lambda b, h, qi: (qrow(b, qi), h)),
            pl.BlockSpec((tq, hw), lambda b, h, qi: (qrow(b, qi), groups + h)),
            pl.BlockSpec((seq, 2 * hw), lambda b, h, qi: (b, h)),
            pl.BlockSpec((ctx, 2 * hw), lambda b, h, qi: (ctx0 + b, h)),
            pl.BlockSpec((seq, LANES), lambda b, h, qi: (b, kb)),
            pl.BlockSpec((ctx, LANES), lambda b, h, qi: (ctx0 + b, kb)),
            pl.BlockSpec((tq, LANES), qtab),
            pl.BlockSpec((tq, LANES), qtab),
            pl.BlockSpec((seq, LANES), lambda b, h, qi: (0, 0)),
            pl.BlockSpec((seq, LANES), lambda b, h, qi: (0, 0)),
        ],
        out_specs=pl.BlockSpec((tq, hw), lambda b, h, qi: (qrow(b, qi), h)),
        out_shape=jax.ShapeDtypeStruct((rows.all, MLA_OUT), BF16),
        scratch_shapes=[kv_scratch, kv_scratch],
        compiler_params=_cparams(("arbitrary", "arbitrary", "arbitrary")),
        name="mla_attn",
    )(qup, qup, kvup, kvup, hl, hl, cos_t, sin_t, cos_t, sin_t)


def _conv3(p, w):
    n = p.shape[0]
    row = lax.broadcasted_iota(I32, p.shape, 0)
    prev = jnp.where(row == 0, 0.0, pltpu.roll(p, 1, 0))
    nxt = jnp.where(row == n - 1, 0.0, pltpu.roll(p, n - 1, 0))
    return prev * w[0:1] + p * w[1:2] + nxt * w[2:3]


HALO = 16


def _sconv_kernel(gb_ref, gc_ref, hh_ref, gcp_ref, hhp_ref, gcn_ref, hhn_ref, w_ref, o_ref, *, seq, ctx, lat):
    i = pl.program_id(0)
    tr = gc_ref.shape[0]
    r0 = i * tr
    length = jnp.where(r0 < lat, seq, ctx)
    pos = jnp.where(r0 < lat, r0, r0 - lat)
    starts = lax.rem(pos, length) == 0
    ends = lax.rem(pos + tr, length) == 0
    p = gc_ref[...].astype(F32) * hh_ref[...].astype(F32)
    before = gcp_ref[HALO - 1:HALO, :].astype(F32) * hhp_ref[HALO - 1:HALO, :].astype(F32)
    after = gcn_ref[0:1, :].astype(F32) * hhn_ref[0:1, :].astype(F32)
    before = jnp.where(starts, 0.0, before)
    after = jnp.where(ends, 0.0, after)
    row = lax.broadcasted_iota(I32, p.shape, 0)
    prev = jnp.where(row == 0, before, pltpu.roll(p, 1, 0))
    nxt = jnp.where(row == tr - 1, after, pltpu.roll(p, tr - 1, 0))
    w = w_ref[...]
    conv = prev * w[0:1] + p * w[1:2] + nxt * w[2:3]
    o_ref[...] = (gb_ref[...].astype(F32) * conv).astype(o_ref.dtype)


def _short_conv(hl, col0, width, conv_w, rows):
    tr, tc = 256, 1024
    assert rows.seq % tr == 0 and rows.ctx % tr == 0
    nct = width // tc
    c0 = col0 // tc
    per = tr // HALO
    last = rows.all // HALO - 1

    def main(k):
        return pl.BlockSpec((tr, tc), lambda i, j: (i, c0 + k * nct + j))

    def before(k):
        return pl.BlockSpec((HALO, tc), lambda i, j: (jnp.maximum(i * per - 1, 0), c0 + k * nct + j))

    def after(k):
        return pl.BlockSpec((HALO, tc), lambda i, j: (jnp.minimum((i + 1) * per, last), c0 + k * nct + j))

    return pl.pallas_call(
        functools.partial(_sconv_kernel, seq=rows.seq, ctx=rows.ctx, lat=rows.lat),
        grid=(rows.all // tr, nct),
        in_specs=[main(0), main(1), main(2), before(1), before(2), after(1), after(2),
                  pl.BlockSpec((3, tc), lambda i, j: (0, j))],
        out_specs=pl.BlockSpec((tr, tc), lambda i, j: (i, j)),
        out_shape=jax.ShapeDtypeStruct((rows.all, width), BF16),
        compiler_params=_cparams(("arbitrary", "arbitrary")),
        name="short_conv",
    )(hl, hl, hl, hl, hl, hl, hl, conv_w)


def _dwconv_kernel(x_ref, w_ref, o_ref):
    o_ref[...] = _conv3(x_ref[...].astype(F32), w_ref[...]).astype(o_ref.dtype)


def _hyena_conv3(hl, col0, width, conv_w, rows):
    tc = 512
    c0 = col0 // tc
    return pl.pallas_call(
        _dwconv_kernel,
        grid=(rows.batch, width // tc),
        in_specs=[
            pl.BlockSpec((rows.seq, tc), lambda s, j: (s, c0 + j)),
            pl.BlockSpec((3, tc), lambda s, j: (0, j)),
        ],
        out_specs=pl.BlockSpec((rows.seq, tc), lambda s, j: (s, j)),
        out_shape=jax.ShapeDtypeStruct((rows.lat, width), BF16),
        compiler_params=_cparams(("arbitrary", "arbitrary")),
        name="hyena_conv3",
    )(hl, conv_w)


def _diff_attn_kernel(q_ref, kl_ref, kc_ref, vl_ref, vc_ref, cq_ref, sq_ref, ck_ref, sk_ref, lam_ref, sub_ref,
                      o_ref, kf_ref, vf_ref, *, seq, ctx, lam_init):
    qi = pl.program_id(2)

    @pl.when(qi == 0)
    def _build_keys():
        ones = _ones_column((seq + ctx, LANES))
        for hh in range(ATTN_HEADS):
            cols = slice(hh * LANES, (hh + 1) * LANES)
            kf_ref[hh, 0:seq, :] = _rope_pairs(kl_ref[:, cols].astype(F32), ck_ref[...], sk_ref[...]).astype(BF16)
            kf_ref[hh, seq:seq + ctx, :] = kc_ref[:, cols]
            vf_ref[hh, 0:seq, 0:LANES] = vl_ref[:, cols]
            vf_ref[hh, seq:seq + ctx, 0:LANES] = vc_ref[:, cols]
            vf_ref[hh, :, LANES:2 * LANES] = ones

    lp = lam_ref[...]
    lam = (jnp.exp(jnp.sum(lp[0:1] * lp[1:2], -1, keepdims=True))
           - jnp.exp(jnp.sum(lp[2:3] * lp[3:4], -1, keepdims=True)) + lam_init)
    def scores(hh, r):
        cols = slice(hh * LANES, (hh + 1) * LANES)
        q = _rope_pairs(q_ref[r:r + ATTN_CHUNK, cols].astype(F32), cq_ref[r:r + ATTN_CHUNK, :],
                        sq_ref[r:r + ATTN_CHUNK, :])
        first = lax.broadcasted_iota(I32, q.shape, 1) < DIFF_HEAD_DIM
        qq = jnp.concatenate([jnp.where(first, q, 0.0), jnp.where(first, 0.0, q)], 0).astype(BF16)
        return lax.dot_general(qq, kf_ref[hh], (((1,), (1,)), ((), ())), preferred_element_type=F32)

    def finish(hh, r, s):
        e = jnp.exp2(s - jnp.max(s, -1, keepdims=True)).astype(BF16)
        ov = jnp.dot(e, vf_ref[hh], preferred_element_type=F32)
        on = ov[:, :LANES] / ov[:, LANES:LANES + 1]
        o = on[:ATTN_CHUNK] - lam * on[ATTN_CHUNK:]
        out = (_rms(o) * sub_ref[...] * (1.0 - lam_init)).astype(o_ref.dtype)
        o_ref[r:r + ATTN_CHUNK, hh * LANES:(hh + 1) * LANES] = out

    pending = []
    for hh in range(ATTN_HEADS):
        for r in range(0, q_ref.shape[0], ATTN_CHUNK):
            pending.append((hh, r, scores(hh, r)))
            if len(pending) > ATTN_AHEAD:
                finish(*pending.pop(0))
    for p in pending:
        finish(*p)


def _diff_attention(hl, hckv, cos_t, sin_t, lam_p, subln, rows, lam_init):
    batch, seq, ctx = rows.batch, rows.seq, rows.ctx
    tq = 512
    nl = seq // tq
    hw = ATTN_HEADS * LANES
    kc0 = DIFF_QK // hw
    vc0 = 2 * DIFF_QK // hw
    lam_pad = jnp.pad(lam_p.astype(F32), ((0, 0), (0, LANES - lam_p.shape[1])))
    return pl.pallas_call(
        functools.partial(_diff_attn_kernel, seq=seq, ctx=ctx, lam_init=lam_init),
        grid=(batch, DIFF_HEADS // ATTN_HEADS, nl),
        in_specs=[
            pl.BlockSpec((tq, hw), lambda b, h, qi: (b * nl + qi, h)),
            pl.BlockSpec((seq, hw), lambda b, h, qi: (b, kc0 + h)),
            pl.BlockSpec((ctx, hw), lambda b, h, qi: (b, h)),
            pl.BlockSpec((seq, hw), lambda b, h, qi: (b, vc0 + h)),
            pl.BlockSpec((ctx, hw), lambda b, h, qi: (b, kc0 + h)),
            pl.BlockSpec((tq, LANES), lambda b, h, qi: (qi, 0)),
            pl.BlockSpec((tq, LANES), lambda b, h, qi: (qi, 0)),
            pl.BlockSpec((seq, LANES), lambda b, h, qi: (0, 0)),
            pl.BlockSpec((seq, LANES), lambda b, h, qi: (0, 0)),
            pl.BlockSpec((4, LANES), lambda b, h, qi: (0, 0)),
            pl.BlockSpec((1, LANES), lambda b, h, qi: (0, 0)),
        ],
        out_specs=pl.BlockSpec((tq, hw), lambda b, h, qi: (b * nl + qi, h)),
        out_shape=jax.ShapeDtypeStruct((rows.lat, DIFF_V), BF16),
        scratch_shapes=[pltpu.VMEM((ATTN_HEADS, seq + ctx, LANES), BF16),
                        pltpu.VMEM((ATTN_HEADS, seq + ctx, 2 * LANES), BF16)],
        compiler_params=_cparams(("arbitrary", "arbitrary", "arbitrary")),
        name="diff_attn",
    )(hl, hl, hckv, hl, hckv, cos_t, sin_t, cos_t, sin_t, lam_pad, subln.reshape(1, LANES).astype(F32))


def _hid_kernel(z_ref, w1_ref, b1_ref, w2_ref, b2_ref, o_ref):
    h1 = jnp.sin(_dot3(z_ref[...], w1_ref[...]) + b1_ref[...])
    o_ref[...] = jnp.sin(_dot3(h1, w2_ref[...]) + b2_ref[...])


def _filt_kernel(hid_ref, w3f_ref, w3b_ref, dec_ref, o_ref, *, n):
    hf = hid_ref[0:n, :]
    hr = hid_ref[n:2 * n, :]
    w3f, w3b = w3f_ref[...], w3b_ref[...]
    kf = _dot3(hf, w3f)
    kb0 = _dot3(hf[0:8], w3b)[0:1]
    kbr = _dot3(hr, w3b)
    dec = dec_ref[...]
    row = lax.broadcasted_iota(I32, kf.shape, 0)
    inv = 1.0 / (n - 1)
    win_f = jnp.exp(-(row.astype(F32) * inv) * dec) + HY_SHIFT
    win_r = jnp.exp(-((n - row).astype(F32) * inv) * dec) + HY_SHIFT
    head = kf * win_f + jnp.where(row == 0, kb0 * (1.0 + HY_SHIFT), 0.0)
    tail = jnp.where(row == 0, 0.0, kbr * win_r)
    nrm = jnp.sum(jnp.abs(head), 0, keepdims=True) + jnp.sum(jnp.abs(tail), 0, keepdims=True)
    o_ref[0:n, :] = (head / nrm).astype(o_ref.dtype)
    o_ref[n:2 * n, :] = (tail / nrm).astype(o_ref.dtype)


def _hyena_filters(n, width, w1, b1, w2, b2, w3):
    t = jnp.linspace(0.0, 1.0, n, dtype=F32)[:, None]
    ang = (2.0 * math.pi / n) * jnp.arange(n, dtype=F32)[:, None] * jnp.linspace(1e-4, HY_BANDS - 1, HY_BANDS, dtype=F32)[None, :]
    z = jnp.concatenate([t, jnp.cos(ang), -jnp.sin(ang)], -1)
    z_rev = jnp.concatenate([z[:1], z[:0:-1]], 0)
    z2 = jnp.pad(jnp.concatenate([z, z_rev], 0), ((0, 0), (0, LANES - HY_EMB)))
    pad = LANES - HY_FFN
    w1p = jnp.pad(w1.astype(F32), ((0, LANES - HY_EMB), (0, pad)))
    b1p = jnp.pad(b1.astype(F32), (0, pad)).reshape(1, LANES)
    w2p = jnp.pad(w2.astype(F32), ((0, pad), (0, pad)))
    b2p = jnp.pad(b2.astype(F32), (0, pad)).reshape(1, LANES)
    w3p = jnp.pad(w3.astype(F32), ((0, pad), (0, 0)))
    th = min(1024, 2 * n)
    hid = pl.pallas_call(
        _hid_kernel,
        grid=(2 * n // th,),
        in_specs=[
            pl.BlockSpec((th, LANES), lambda i: (i, 0)),
            pl.BlockSpec((LANES, LANES), lambda i: (0, 0)),
            pl.BlockSpec((1, LANES), lambda i: (0, 0)),
            pl.BlockSpec((LANES, LANES), lambda i: (0, 0)),
            pl.BlockSpec((1, LANES), lambda i: (0, 0)),
        ],
        out_specs=pl.BlockSpec((th, LANES), lambda i: (i, 0)),
        out_shape=jax.ShapeDtypeStruct((2 * n, LANES), F32),
        compiler_params=_cparams(("arbitrary",)),
        name="hyena_hidden",
    )(z2, w1p, b1p, w2p, b2p)
    decay = jnp.abs(jnp.linspace(HY_MIN_DECAY, HY_MAX_DECAY, width, dtype=F32)).reshape(1, width)
    tn = 256
    per = width // tn
    return pl.pallas_call(
        functools.partial(_filt_kernel, n=n),
        grid=(HY_ORDER, per),
        in_specs=[
            pl.BlockSpec((2 * n, LANES), lambda o, j: (0, 0)),
            pl.BlockSpec((LANES, tn), lambda o, j: (0, (2 * o) * per + j)),
            pl.BlockSpec((LANES, tn), lambda o, j: (0, (2 * o + 1) * per + j)),
            pl.BlockSpec((1, tn), lambda o, j: (0, j)),
        ],
        out_specs=pl.BlockSpec((None, 2 * n, tn), lambda o, j: (o, 0, j)),
        out_shape=jax.ShapeDtypeStruct((HY_ORDER, 2 * n, width), BF16),
        compiler_params=_cparams(("arbitrary", "arbitrary")),
        name="hyena_filters",
    )(hid, w3p, w3p, decay)


def _dft_tables(n):
    p = 2 * n
    lo = 64
    k = jnp.arange(n, dtype=I32)[:, None]
    t1 = jnp.arange(p // lo, dtype=I32)[None, :]
    t2 = jnp.arange(lo, dtype=I32)[None, :]
    ang_a = ((k * t1 * lo) % p).astype(F32) * (2.0 * math.pi / p)
    ang_b = ((k * t2) % p).astype(F32) * (2.0 * math.pi / p)
    ca, sa = jnp.cos(ang_a)[:, :, None], jnp.sin(ang_a)[:, :, None]
    cb, sb = jnp.cos(ang_b)[:, None, :], jnp.sin(ang_b)[:, None, :]
    cos_kt = (ca * cb - sa * sb).reshape(n, p)
    sin_kt = (sa * cb + ca * sb).reshape(n, p)
    t = jnp.arange(p, dtype=I32)[None, :]
    alt = jnp.where(t % 2 == 0, 1.0, -1.0)
    fwd = jnp.stack([cos_kt, jnp.where(k == 0, alt, -sin_kt)]).astype(BF16)
    kk = t[:, :n]
    gx = jnp.where(kk == 0, 1.0, 2.0 * cos_kt[:, :n]) / p
    gy = jnp.where(kk == 0, jnp.where(k % 2 == 0, 1.0, -1.0), -2.0 * sin_kt[:, :n]) / p
    inv = jnp.stack([gx, gy]).astype(BF16)
    return fwd, inv


def _spec_kernel(fx_ref, fy_ref, f_ref, o_ref):
    f = f_ref[...]
    o_ref[0] = jnp.dot(fx_ref[...], f, preferred_element_type=F32)
    o_ref[1] = jnp.dot(fy_ref[...], f, preferred_element_type=F32)


def _filter_spectrum(fwd, filt):
    _, n, p = fwd.shape
    width = filt.shape[2]
    tm, tn = min(512, n), 512
    return pl.pallas_call(
        _spec_kernel,
        grid=(HY_ORDER, width // tn, n // tm),
        in_specs=[
            pl.BlockSpec((None, tm, p), lambda o, j, i: (0, i, 0)),
            pl.BlockSpec((None, tm, p), lambda o, j, i: (1, i, 0)),
            pl.BlockSpec((None, p, tn), lambda o, j, i: (o, 0, j)),
        ],
        out_specs=pl.BlockSpec((None, 2, tm, tn), lambda o, j, i: (o, 0, i, j)),
        out_shape=jax.ShapeDtypeStruct((HY_ORDER, 2, n, width), F32),
        compiler_params=_cparams(("arbitrary", "arbitrary", "arbitrary")),
        name="filter_spectrum",
    )(fwd, fwd, filt)


def _dft_fwd_kernel(fx_ref, fy_ref, z_ref, hx_ref, hy_ref, o_ref):
    i = pl.program_id(2)
    z = z_ref[...]
    ux = jnp.dot(fx_ref[...], z, preferred_element_type=F32)
    uy = jnp.dot(fy_ref[...], z, preferred_element_type=F32)
    hx, hy = hx_ref[...], hy_ref[...]
    row = lax.broadcasted_iota(I32, ux.shape, 0) + i * ux.shape[0]
    dc = row == 0
    o_ref[0] = (ux * hx - jnp.where(dc, 0.0, uy * hy)).astype(o_ref.dtype)
    o_ref[1] = jnp.where(dc, uy * hy, ux * hy + uy * hx).astype(o_ref.dtype)


def _dft_forward(fwd, z, z_col0, spec, order, rows):
    n = rows.seq
    width = spec.shape[3]
    tm, tn = min(1024, n), 512
    c0 = z_col0 // tn
    return pl.pallas_call(
        _dft_fwd_kernel,
        grid=(rows.batch, width // tn, n // tm),
        in_specs=[
            pl.BlockSpec((None, tm, n), lambda b, j, i: (0, i, 0)),
            pl.BlockSpec((None, tm, n), lambda b, j, i: (1, i, 0)),
            pl.BlockSpec((n, tn), lambda b, j, i: (b, c0 + j)),
            pl.BlockSpec((None, None, tm, tn), lambda b, j, i: (order, 0, i, j)),
            pl.BlockSpec((None, None, tm, tn), lambda b, j, i: (order, 1, i, j)),
        ],
        out_specs=pl.BlockSpec((None, 2, tm, tn), lambda b, j, i: (b, 0, i, j)),
        out_shape=jax.ShapeDtypeStruct((rows.batch, 2, n, width), BF16),
        compiler_params=_cparams(("arbitrary", "arbitrary", "arbitrary")),
        name="hyena_dft_fwd",
    )(fwd, fwd, z, spec, spec)


def _dft_inv_kernel(gx_ref, gy_ref, s_ref, gate_ref, z_ref, skip_ref, o_ref):
    y = (jnp.dot(gx_ref[...], s_ref[0], preferred_element_type=F32)
         + jnp.dot(gy_ref[...], s_ref[1], preferred_element_type=F32))
    z = z_ref[...].astype(F32)
    o_ref[...] = (gate_ref[...].astype(F32) * (y + z * skip_ref[...])).astype(o_ref.dtype)


def _dft_inverse(inv, spec_z, gates, gate_col0, z, z_col0, skip, order, rows):
    n = rows.seq
    width = spec_z.shape[3]
    tm, tn = min(1024, n), 512
    nb = n // tm
    g0, z0 = gate_col0 // tn, z_col0 // tn
    return pl.pallas_call(
        _dft_inv_kernel,
        grid=(rows.batch, width // tn, nb),
        in_specs=[
            pl.BlockSpec((None, tm, n), lambda b, j, i: (0, i, 0)),
            pl.BlockSpec((None, tm, n), lambda b, j, i: (1, i, 0)),
            pl.BlockSpec((None, 2, n, tn), lambda b, j, i: (b, 0, 0, j)),
            pl.BlockSpec((tm, tn), lambda b, j, i: (b * nb + i, g0 + j)),
            pl.BlockSpec((tm, tn), lambda b, j, i: (b * nb + i, z0 + j)),
            pl.BlockSpec((None, 1, tn), lambda b, j, i: (order, 0, j)),
        ],
        out_specs=pl.BlockSpec((tm, tn), lambda b, j, i: (b * nb + i, j)),
        out_shape=jax.ShapeDtypeStruct((rows.lat, width), BF16),
        compiler_params=_cparams(("arbitrary", "arbitrary", "arbitrary")),
        name="hyena_dft_inv",
    )(inv, inv, spec_z, gates, z, skip.reshape(HY_ORDER, 1, width).astype(F32))


def _router_kernel(x_ref, sh_ref, sc_ref, rw_ref, rb_ref, u_ref, ti_ref, tg_ref, rk_ref, cnt_ref, carry_ref,
                   w_ref):
    i = pl.program_id(0)

    @pl.when(i == 0)
    def _init():
        carry_ref[...] = jnp.zeros_like(carry_ref)
        w_ref[0], w_ref[1] = _split_bf16(rw_ref[...])

    u = _rms(x_ref[...]) * (1.0 + sc_ref[...]) + sh_ref[...]
    u_ref[...] = _pack_halves(u)
    uh, ul = _split_bf16(u)
    dot = functools.partial(jnp.dot, preferred_element_type=F32)
    logits = dot(uh, w_ref[0]) + dot(ul, w_ref[0]) + dot(uh, w_ref[1]) + rb_ref[...]
    tm = logits.shape[0]
    lane = lax.broadcasted_iota(I32, logits.shape, 1).astype(F32)
    neg = jnp.float32(-jnp.inf)
    work = jnp.where(lane < N_EXPERTS, logits, neg)
    vals, idxs = [], []
    for _ in range(TOP_K):
        m = jnp.max(work, -1, keepdims=True)
        idx = jnp.min(jnp.where(work == m, lane, float(LANES)), -1, keepdims=True)
        vals.append(m)
        idxs.append(idx)
        work = jnp.where(lane == idx, neg, work)
    es = [jnp.exp(v - vals[0]) for v in vals]
    den = es[0] + es[1] + es[2] + es[3]
    onehot = jnp.zeros(logits.shape, F32)
    for idx in idxs:
        onehot = onehot + (lane == idx).astype(F32)
    r = lax.broadcasted_iota(I32, (tm, tm), 0)
    c = lax.broadcasted_iota(I32, (tm, tm), 1)
    before = (c < r).astype(BF16)
    carry = carry_ref[0:1, :]
    prefix = jnp.dot(before, onehot.astype(BF16), preferred_element_type=F32) + carry
    ti = jnp.zeros(logits.shape, F32)
    tg = jnp.zeros(logits.shape, F32)
    rk = jnp.zeros(logits.shape, F32)
    for k in range(TOP_K):
        pos = jnp.sum(jnp.where(lane == idxs[k], prefix, 0.0), -1, keepdims=True)
        ti = jnp.where(lane == k, idxs[k], ti)
        tg = jnp.where(lane == k, es[k] / den, tg)
        rk = jnp.where(lane == k, pos, rk)
    ti_ref[...] = ti
    tg_ref[...] = tg
    rk_ref[...] = rk
    total = carry + jnp.sum(onehot, 0, keepdims=True)
    carry_ref[...] = jnp.broadcast_to(total, carry_ref.shape)
    cnt_ref[...] = jnp.broadcast_to(total, cnt_ref.shape)


def _router(h, mod4, rows, layer, n_rows, router_w, router_b):
    d = h.shape[1]
    tm = 256
    rw = jnp.pad(router_w.astype(F32), ((0, 0), (0, LANES - N_EXPERTS)))
    rb = jnp.pad(router_b.astype(F32), (0, LANES - N_EXPERTS)).reshape(1, LANES)
    small = lambda dt: jax.ShapeDtypeStruct((n_rows, LANES), dt)
    return pl.pallas_call(
        _router_kernel,
        grid=(n_rows // tm,),
        in_specs=[
            pl.BlockSpec((tm, d), lambda i: (i, 0)),
            _mod_spec(rows, tm, d, layer, 3, d, 1, 0),
            _mod_spec(rows, tm, d, layer, 4, d, 1, 0),
            pl.BlockSpec((d, LANES), lambda i: (0, 0)),
            pl.BlockSpec((1, LANES), lambda i: (0, 0)),
        ],
        out_specs=[
            pl.BlockSpec((tm, d // 2), lambda i: (i, 0)),
            pl.BlockSpec((tm, LANES), lambda i: (i, 0)),
            pl.BlockSpec((tm, LANES), lambda i: (i, 0)),
            pl.BlockSpec((tm, LANES), lambda i: (i, 0)),
            pl.BlockSpec((8, LANES), lambda i: (0, 0)),
        ],
        out_shape=[jax.ShapeDtypeStruct((n_rows, d // 2), U32), small(F32), small(F32), small(F32),
                   jax.ShapeDtypeStruct((8, LANES), F32)],
        scratch_shapes=[pltpu.VMEM((8, LANES), F32), pltpu.VMEM((2, d, LANES), BF16)],
        compiler_params=_cparams(("arbitrary",)),
        name="router",
    )(h, mod4, mod4, rw, rb)


def _row_copy(src_ref, src_row, dst_ref, dst_row, sem):
    return pltpu.make_async_copy(src_ref.at[pl.ds(src_row, 1)], dst_ref.at[pl.ds(dst_row, 1)], sem)


def _block_copy(src_ref, dst_ref, sem):
    return pltpu.make_async_copy(src_ref.at[pl.ds(0, dst_ref.shape[0])], dst_ref, sem)


def _expert_kernel(be_ref, used_ref, tok_ref, tok_next_ref, src_ref, wgu_ref, bgu_ref, wd_ref, bd_ref, o_ref,
                   xbuf, sems):
    i = pl.program_id(0)
    slot = lax.rem(i, 2)
    n_used = used_ref[0]

    def gather(ids_ref, s):
        for r in range(MOE_BLK):
            _row_copy(src_ref, ids_ref[0, 0, r], xbuf.at[s], r, sems.at[s]).start()

    @pl.when(i == 0)
    def _first():
        gather(tok_ref, 0)

    @pl.when(i + 1 < n_used)
    def _ahead():
        gather(tok_next_ref, 1 - slot)

    @pl.when(i < n_used)
    def _compute():
        _block_copy(src_ref, xbuf.at[slot], sems.at[slot]).wait()
        dot = functools.partial(jnp.dot, preferred_element_type=F32)
        def gate_up(r):
            lo, hi = _unpack_halves(xbuf[slot, r:r + EXPERT_CHUNK, :])
            half = lo.shape[1]
            return (dot(lo.astype(BF16), wgu_ref[0:half, :]) + dot(hi.astype(BF16), wgu_ref[half:2 * half, :])
                    + bgu_ref[...])

        def down(r, gu):
            de = gu.shape[1] // 2
            g = jnp.minimum(gu[:, :de], SWIGLU_LIMIT)
            up = jnp.clip(gu[:, de:], -SWIGLU_LIMIT, SWIGLU_LIMIT)
            a = (up + 1.0) * g * jax.nn.sigmoid(SWIGLU_ALPHA * g)
            y = dot(a.astype(BF16), wd_ref[...]) + bd_ref[...]
            o_ref[r:r + EXPERT_CHUNK, :] = _pack_halves(y)

        pending = None
        for r in range(0, MOE_BLK, EXPERT_CHUNK):
            gu = gate_up(r)
            if pending is not None:
                down(*pending)
            pending = (r, gu)
        down(*pending)

    @pl.when(i >= n_used)
    def _empty():
        o_ref[...] = jnp.zeros(o_ref.shape, o_ref.dtype)


def _experts(u_packed, slot_tok, block_e, n_used, layer, w_gu, b_gu, w_down, b_down):
    _, n_exp, d, de2 = w_gu.shape
    de = de2 // 2
    n_blocks = block_e.shape[0]
    tok = slot_tok.reshape(n_blocks, 1, MOE_BLK)
    grid_spec = pltpu.PrefetchScalarGridSpec(
        num_scalar_prefetch=2,
        grid=(n_blocks,),
        in_specs=[
            pl.BlockSpec((1, 1, MOE_BLK), lambda i, be, nu: (i, 0, 0), memory_space=pltpu.SMEM),
            pl.BlockSpec((1, 1, MOE_BLK), lambda i, be, nu: (jnp.minimum(i + 1, n_blocks - 1), 0, 0),
                         memory_space=pltpu.SMEM),
            pl.BlockSpec(memory_space=pl.ANY),
            pl.BlockSpec((None, None, d, de2), lambda i, be, nu: (layer, be[i], 0, 0)),
            pl.BlockSpec((None, None, 1, de2), lambda i, be, nu: (layer, be[i], 0, 0)),
            pl.BlockSpec((None, None, de, d), lambda i, be, nu: (layer, be[i], 0, 0)),
            pl.BlockSpec((None, None, 1, d), lambda i, be, nu: (layer, be[i], 0, 0)),
        ],
        out_specs=pl.BlockSpec((MOE_BLK, d // 2), lambda i, be, nu: (i, 0)),
        scratch_shapes=[pltpu.VMEM((2, MOE_BLK, d // 2), U32), pltpu.SemaphoreType.DMA((2,))],
    )
    return pl.pallas_call(
        _expert_kernel,
        grid_spec=grid_spec,
        out_shape=jax.ShapeDtypeStruct((n_blocks * MOE_BLK, d // 2), U32),
        compiler_params=_cparams(("arbitrary",)),
        name="moe_experts",
    )(block_e, n_used, tok, tok, u_packed, w_gu, b_gu.reshape(-1, n_exp, 1, de2).astype(F32), w_down,
      b_down.reshape(-1, n_exp, 1, d).astype(F32))


def _combine_kernel(dest_ref, dest_next_ref, gate_ref, h_ref, g2_ref, sh_ref, sc_ref, yb_ref, *rest, final):
    if final:
        o_ref, buf, sems = rest
    else:
        o_ref, u_ref, buf, sems = rest
    n = h_ref.shape[0]
    i = pl.program_id(0)
    slot = lax.rem(i, 2)

    def gather(ids_ref, s):
        for r in range(n):
            for k in range(TOP_K):
                _row_copy(yb_ref, ids_ref[0, 0, r * TOP_K + k], buf.at[s], k * n + r, sems.at[s]).start()

    @pl.when(i == 0)
    def _first():
        gather(dest_ref, 0)

    @pl.when(i + 1 < pl.num_programs(0))
    def _ahead():
        gather(dest_next_ref, 1 - slot)

    _block_copy(yb_ref, buf.at[slot], sems.at[slot]).wait()
    gates = gate_ref[...]
    acc_lo = None
    acc_hi = None
    for k in range(TOP_K):
        lo, hi = _unpack_halves(buf[slot, k * n:(k + 1) * n, :])
        gk = gates[:, k:k + 1]
        acc_lo = gk * lo if acc_lo is None else acc_lo + gk * lo
        acc_hi = gk * hi if acc_hi is None else acc_hi + gk * hi
    out = h_ref[...] + g2_ref[...] * jnp.concatenate([acc_lo, acc_hi], -1)
    if final:
        o_ref[...] = _rms(out) * sc_ref[...]
    else:
        o_ref[...] = out
        u_ref[...] = (_rms(out) * (1.0 + sc_ref[...]) + sh_ref[...]).astype(u_ref.dtype)


def _combine(dest, gates, h, mod4, rows, layer, n_rows, yb, final_gain, final):
    d = h.shape[1]
    tm = COMBINE_BLK
    nb = n_rows // tm
    dest3 = dest.reshape(nb, 1, tm * TOP_K)
    row_spec = pl.BlockSpec((tm, d), lambda i: (i, 0))
    if final:
        gain = final_gain.reshape(1, d).astype(F32)
        mod_specs = [pl.BlockSpec((1, d), lambda i: (0, 0))] * 2
        mod_args = [gain, gain]
        out_specs, out_shape = row_spec, jax.ShapeDtypeStruct((n_rows, d), F32)
    else:
        mod_specs = [_mod_spec(rows, tm, d, layer + 1, 0, d, 1, 0), _mod_spec(rows, tm, d, layer + 1, 1, d, 1, 0)]
        mod_args = [mod4, mod4]
        out_specs = [row_spec, row_spec]
        out_shape = [jax.ShapeDtypeStruct((n_rows, d), F32), jax.ShapeDtypeStruct((n_rows, d), BF16)]
    return pl.pallas_call(
        functools.partial(_combine_kernel, final=final),
        grid=(nb,),
        in_specs=[
            pl.BlockSpec((1, 1, tm * TOP_K), lambda i: (i, 0, 0), memory_space=pltpu.SMEM),
            pl.BlockSpec((1, 1, tm * TOP_K), lambda i: (jnp.minimum(i + 1, nb - 1), 0, 0), memory_space=pltpu.SMEM),
            pl.BlockSpec((tm, LANES), lambda i: (i, 0)),
            row_spec,
            _mod_spec(rows, tm, d, layer, 5, d, 1, 0),
            *mod_specs,
            pl.BlockSpec(memory_space=pl.ANY),
        ],
        out_specs=out_specs,
        out_shape=out_shape,
        scratch_shapes=[pltpu.VMEM((2, TOP_K * tm, d // 2), U32), pltpu.SemaphoreType.DMA((2,))],
        compiler_params=_cparams(("arbitrary",)),
        name="moe_combine",
    )(dest3, dest3, gates, h, mod4, *mod_args, yb)


def _moe(h, mod4, rows, layer, n_rows, router_w, router_b, w_gu, b_gu, w_down, b_down, final_gain, final):
    u_packed, ti, tg, rk, cnt = _router(h, mod4, rows, layer, n_rows, router_w, router_b)
    counts = cnt[0, :N_EXPERTS].astype(I32)
    padded = (counts + MOE_BLK - 1) // MOE_BLK * MOE_BLK
    pad_end = jnp.cumsum(padded)
    pad_start = pad_end - padded
    top_i = ti[:, :TOP_K].astype(I32)
    dest = pad_start[top_i] + rk[:, :TOP_K].astype(I32)
    n_blocks = n_rows * TOP_K // MOE_BLK + N_EXPERTS
    tok = jnp.repeat(jnp.arange(n_rows, dtype=I32), TOP_K)
    slot_tok = jnp.zeros((n_blocks * MOE_BLK,), I32).at[dest.reshape(-1)].set(tok)
    block_start = jnp.arange(n_blocks, dtype=I32)[:, None] * MOE_BLK
    block_e = jnp.minimum(jnp.sum((pad_end[None, :] <= block_start).astype(I32), 1), N_EXPERTS - 1)
    n_used = (pad_end[-1:] // MOE_BLK).astype(I32)
    yb = _experts(u_packed, slot_tok, block_e, n_used, layer, w_gu, b_gu, w_down, b_down)
    return _combine(dest, tg, h, mod4, rows, layer, n_rows, yb, final_gain, final)


def _rope_tables(seq):
    n_rows = seq // GRID_W
    row = jnp.repeat(jnp.arange(n_rows), GRID_W).astype(F32)
    col = jnp.tile(jnp.arange(GRID_W), n_rows).astype(F32)
    quarter = MLA_ROPE_DIM // 4
    inv = 1.0 / (ROPE_THETA ** (jnp.arange(quarter, dtype=F32) / quarter))
    ang = jnp.concatenate([row[:, None] * inv, col[:, None] * inv], -1)
    cos, sin = jnp.cos(ang), jnp.sin(ang)
    return jnp.tile(cos, (1, 4)), jnp.tile(jnp.concatenate([-sin, sin], -1), (1, 2))


def kernel(x, c, ctx, c_ctx, ada_w, ada_b, mla_in_w, mla_q_norm, mla_kv_norm, mla_w_uq, mla_w_ukv, sc_conv_w, even_out_w, odd_in_w, diff_lambda, diff_subln, hy_conv_w, hy_w1, hy_b1, hy_w2, hy_b2, hy_w3, hy_skip, odd_out_w, router_w, router_b, moe_w_gu, moe_b_gu, moe_w_down, moe_b_down, final_norm):
    batch, seq, d = x.shape
    n_ctx = ctx.shape[1]
    depth = ada_w.shape[0]
    assert depth == 2 and batch + 1 <= MOD_ROWS
    rows = _Rows(batch, seq, n_ctx)
    sc_width = d - MLA_OUT
    hy_width = d - DIFF_V

    cond = jnp.zeros((MOD_ROWS, d), F32).at[:batch].set(c).at[batch].set(c_ctx)
    mod4 = _adaln(cond, ada_w, ada_b)
    cos_t, sin_t = _rope_tables(seq)
    h0 = [x.reshape(rows.lat, d), ctx.reshape(batch * n_ctx, d)]

    kpe_col = MLA_Q_RANK + MLA_KV_RANK
    lat_cols = kpe_col + MLA_ROPE_DIM
    lat_pad = -lat_cols % LANES
    w_lat = jnp.pad(mla_in_w[0][:, :lat_cols], ((0, 0), (0, lat_pad))).astype(BF16)
    w_sc = mla_in_w[0][:, lat_cols:].astype(BF16)
    wq = (mla_w_uq[0] * (MLA_QK_DIM ** -0.5 * LOG2E)).reshape(MLA_Q_RANK, MLA_HEADS, MLA_QK_DIM)
    wq_pe = jnp.pad(wq[:, :, MLA_NOPE_DIM:], ((0, 0), (0, 0), (0, LANES - MLA_ROPE_DIM)))
    wq = jnp.concatenate([wq[:, :, :MLA_NOPE_DIM].reshape(MLA_Q_RANK, -1), wq_pe.reshape(MLA_Q_RANK, -1)], 1).astype(BF16)

    u = _modulate(h0, mod4, rows, 0, rows.all)
    hl = _matmul([u], w_lat, n_rows=rows.all, n_cols=w_lat.shape[1], tn=w_lat.shape[1], name="even_in_proj_latent")
    hs = _matmul([u], w_sc, n_rows=rows.all, n_cols=w_sc.shape[1], name="even_in_proj_conv")
    qup = _norm_matmul(hl, 0, mla_q_norm[0], wq, rows.all, "mla_q_up")
    kvup = _norm_matmul(hl, MLA_Q_RANK, mla_kv_norm[0], mla_w_ukv[0].astype(BF16), rows.all, "mla_kv_up")
    attn = _mla_attention(qup, kvup, hl, cos_t, sin_t, rows, kpe_col)
    sc = _short_conv(hs, 0, sc_width, sc_conv_w[0].astype(F32), rows)
    h = _matmul([attn, sc], even_out_w[0].astype(BF16), n_rows=rows.all, n_cols=d, out_dtype=F32,
                resid=(h0, mod4, rows, 0, 2), name="even_out_proj")
    w_gu_bf, w_down_bf = moe_w_gu.astype(BF16), moe_w_down.astype(BF16)
    h, u = _moe(h, mod4, rows, 0, rows.all, router_w[0], router_b[0], w_gu_bf, moe_b_gu, w_down_bf, moe_b_down,
                final_norm, False)

    lam_init = 0.8 - 0.6 * math.exp(-0.3 * 1)
    q_scale = jnp.where(jnp.arange(odd_in_w.shape[2]) < DIFF_QK, DIFF_HEAD_DIM ** -0.5 * LOG2E, 1.0).astype(F32)
    w_in1 = (odd_in_w[0] * q_scale[None, :]).astype(BF16)
    hl = _matmul([u], w_in1, n_rows=rows.lat, n_cols=w_in1.shape[1], name="odd_in_proj")
    hckv = _matmul([u], w_in1, n_rows=batch * n_ctx, n_cols=DIFF_QK + DIFF_V, a_row0=rows.lat, w_col0=DIFF_QK,
                   name="odd_ctx_kv_proj")
    attn = _diff_attention(hl, hckv, cos_t, sin_t, diff_lambda[0], diff_subln[0], rows, lam_init)

    hy_col = 2 * DIFF_QK + DIFF_V
    hc3 = _hyena_conv3(hl, hy_col, 3 * hy_width, hy_conv_w[0].astype(F32), rows)
    filt = _hyena_filters(seq, hy_width, hy_w1[0], hy_b1[0], hy_w2[0], hy_b2[0], hy_w3[0])
    fwd, inv = _dft_tables(seq)
    spec = _filter_spectrum(fwd, filt)
    z, z_col = hc3, 0
    for o in range(HY_ORDER):
        spec_z = _dft_forward(fwd, z, z_col, spec, o, rows)
        z = _dft_inverse(inv, spec_z, hc3, (1 + o) * hy_width, z, z_col, hy_skip[0], o, rows)
        z_col = 0

    h = _matmul([attn, z], odd_out_w[0].astype(BF16), n_rows=rows.lat, n_cols=d, out_dtype=F32,
                resid=([h], mod4, rows, 1, 2), name="odd_out_proj")
    out = _moe(h, mod4, rows, 1, rows.lat, router_w[1], router_b[1], w_gu_bf, moe_b_gu, w_down_bf, moe_b_down,
               final_norm, True)
    return out.reshape(batch, seq, d)
```

```python
import functools
import math

import jax
import jax.numpy as jnp
from jax import lax
from jax.experimental import pallas as pl
from jax.experimental.pallas import tpu as pltpu

F32 = jnp.float32
BF16 = jnp.bfloat16
U32 = jnp.uint32
I32 = jnp.int32

GRID_W = 64
NORM_EPS = 1e-6
ROPE_THETA = 10000.0

MLA_HEADS = 16
MLA_Q_RANK = 1024
MLA_KV_RANK = 512
MLA_NOPE_DIM = 128
MLA_ROPE_DIM = 64
MLA_V_DIM = 128
MLA_QK_DIM = MLA_NOPE_DIM + MLA_ROPE_DIM
MLA_OUT = MLA_HEADS * MLA_V_DIM

DIFF_HEADS = 16
DIFF_HEAD_DIM = 64
DIFF_QK = DIFF_HEADS * 2 * DIFF_HEAD_DIM
DIFF_V = DIFF_HEADS * 2 * DIFF_HEAD_DIM

HY_ORDER = 2
HY_BANDS = 16
HY_EMB = 1 + 2 * HY_BANDS
HY_FFN = 64
HY_SHIFT = 0.05
HY_MIN_DECAY = math.log(1e-2) / 1.5
HY_MAX_DECAY = math.log(1e-2) / 0.3

N_EXPERTS = 32
TOP_K = 4
SWIGLU_LIMIT = 7.0
SWIGLU_ALPHA = 1.702

LOG2E = 1.4426950408889634
LANES = 128
MOD_ROWS = 16
VMEM_LIMIT = 56 * 1024 * 1024
MOE_BLK = 256
EXPERT_CHUNK = 128
COMBINE_BLK = 128
ATTN_CHUNK = 128
ATTN_HEADS = 4
MLA_STEP_HEADS = 8
ATTN_AHEAD = 1


def _cparams(sem):
    return pltpu.CompilerParams(dimension_semantics=sem, vmem_limit_bytes=VMEM_LIMIT)


def _rms(x):
    return x * lax.rsqrt(jnp.mean(x * x, -1, keepdims=True) + NORM_EPS)


def _split_bf16(a):
    hi = a.astype(BF16)
    lo = (a - hi.astype(F32)).astype(BF16)
    return hi, lo


def _dot3(a, b):
    ah, al = _split_bf16(a)
    bh, bl = _split_bf16(b)
    d = functools.partial(jnp.dot, preferred_element_type=F32)
    return d(ah, bh) + d(al, bh) + d(ah, bl)


def _pack_halves(y):
    c = y.shape[1] // 2
    lo = lax.bitcast_convert_type(y[:, :c].astype(BF16).astype(F32), U32)
    hi = lax.bitcast_convert_type(y[:, c:].astype(BF16).astype(F32), U32)
    return (lo >> 16) | (hi & jnp.uint32(0xFFFF0000))


def _unpack_halves(w):
    lo = lax.bitcast_convert_type(w << 16, F32)
    hi = lax.bitcast_convert_type(w & jnp.uint32(0xFFFF0000), F32)
    return lo, hi


def _ones_column(shape):
    lane = lax.broadcasted_iota(I32, shape, 1)
    return jnp.where(lane == 0, 1.0, 0.0).astype(BF16)


def _rope_pairs(x, cos_t, sin_t):
    lane = lax.broadcasted_iota(I32, x.shape, 1)
    swapped = jnp.where((lane & 63) < 32, pltpu.roll(x, 96, 1), pltpu.roll(x, 32, 1))
    return x * cos_t + swapped * sin_t


def _adaln_kernel(c_ref, w_ref, b_ref, o_ref):
    x = c_ref[...]
    s = (x * jax.nn.sigmoid(x)).astype(BF16)
    o_ref[...] = jnp.dot(s, w_ref[...].astype(BF16), preferred_element_type=F32) + b_ref[...]


def _adaln(cond, ada_w, ada_b):
    depth, d, d6 = ada_w.shape
    tn = 512
    out = pl.pallas_call(
        _adaln_kernel,
        grid=(depth, d6 // tn),
        in_specs=[
            pl.BlockSpec((MOD_ROWS, d), lambda l, j: (0, 0)),
            pl.BlockSpec((None, d, tn), lambda l, j: (l, 0, j)),
            pl.BlockSpec((None, 1, tn), lambda l, j: (l, 0, j)),
        ],
        out_specs=pl.BlockSpec((None, MOD_ROWS, tn), lambda l, j: (l, 0, j)),
        out_shape=jax.ShapeDtypeStruct((depth, MOD_ROWS, d6), F32),
        compiler_params=_cparams(("arbitrary", "arbitrary")),
        name="adaln",
    )(cond, ada_w, ada_b.reshape(depth, 1, d6))
    return out.reshape(depth, MOD_ROWS, 1, d6)


class _Rows:
    def __init__(self, batch, seq, ctx):
        self.batch, self.seq, self.ctx = batch, seq, ctx
        self.lat = batch * seq
        self.all = batch * (seq + ctx)

    def mod_row(self, i, tm):
        r = i * tm
        return jnp.where(r < self.lat, r // self.seq, self.batch)


def _mod_spec(rows, tm, tn, layer, chunk, d, grid_rank, row_axis, col_axis=None):
    per = d // tn

    def imap(*g):
        j = 0 if col_axis is None else g[col_axis]
        return (layer, rows.mod_row(g[row_axis], tm), 0, chunk * per + j)

    return pl.BlockSpec((None, None, 1, tn), imap)


def _part_specs(parts, tm, tn, row_axis, col_axis):
    specs, starts, first = [], [], 0
    for p in parts:
        nb = p.shape[0] // tm

        def imap(*g, first=first, nb=nb):
            j = 0 if col_axis is None else g[col_axis]
            return (jnp.clip(g[row_axis] - first, 0, nb - 1), j)

        specs.append(pl.BlockSpec((tm, tn), imap))
        starts.append(first)
        first += nb
    return specs, tuple(starts)


def _pick_part(refs, starts, i):
    val = refs[0][...]
    for ref, start in zip(refs[1:], starts[1:]):
        val = jnp.where(i >= start, ref[...], val)
    return val


def _modulate_kernel(*refs, starts):
    x_refs, (sh_ref, sc_ref, o_ref) = refs[:len(starts)], refs[len(starts):]
    x = _pick_part(x_refs, starts, pl.program_id(0))
    o_ref[...] = (_rms(x) * (1.0 + sc_ref[...]) + sh_ref[...]).astype(o_ref.dtype)


def _modulate(parts, mod4, rows, layer, n_rows):
    d = parts[0].shape[1]
    tm = 256
    specs, starts = _part_specs(parts, tm, d, 0, None)
    return pl.pallas_call(
        functools.partial(_modulate_kernel, starts=starts),
        grid=(n_rows // tm,),
        in_specs=specs + [
            _mod_spec(rows, tm, d, layer, 0, d, 1, 0),
            _mod_spec(rows, tm, d, layer, 1, d, 1, 0),
        ],
        out_specs=pl.BlockSpec((tm, d), lambda i: (i, 0)),
        out_shape=jax.ShapeDtypeStruct((n_rows, d), BF16),
        compiler_params=_cparams(("arbitrary",)),
        name="modulate",
    )(*parts, mod4, mod4)


def _mm_kernel(*refs, ks, h_starts):
    a_refs = refs[:len(ks)]
    w_ref = refs[len(ks)]
    o_ref = refs[-1]
    acc = None
    off = 0
    for a_ref, k in zip(a_refs, ks):
        part = jnp.dot(a_ref[...], w_ref[off:off + k, :], preferred_element_type=F32)
        acc = part if acc is None else acc + part
        off += k
    if h_starts:
        h_refs = refs[len(ks) + 1:len(ks) + 1 + len(h_starts)]
        g_ref = refs[len(ks) + 1 + len(h_starts)]
        o_ref[...] = _pick_part(h_refs, h_starts, pl.program_id(1)) + g_ref[...] * acc
    else:
        o_ref[...] = acc.astype(o_ref.dtype)


def _matmul(a_list, w, *, n_rows, n_cols, a_row0=0, w_col0=0, tm=512, tn=1024, out_dtype=BF16,
            resid=None, name="matmul"):
    tn = min(tn, n_cols)
    tm = min(tm, n_rows)
    ks = tuple(a.shape[1] for a in a_list)
    k_all = sum(ks)
    r0, c0 = a_row0 // tm, w_col0 // tn
    in_specs = [pl.BlockSpec((tm, k), lambda j, i: (r0 + i, 0)) for k in ks]
    in_specs.append(pl.BlockSpec((k_all, tn), lambda j, i: (0, c0 + j)))
    args = list(a_list) + [w]
    h_starts = ()
    if resid is not None:
        h_parts, mod4, rows, layer, chunk = resid
        h_specs, h_starts = _part_specs(h_parts, tm, tn, 1, 0)
        in_specs += h_specs
        in_specs.append(_mod_spec(rows, tm, tn, layer, chunk, h_parts[0].shape[1], 2, 1, 0))
        args += [*h_parts, mod4]
    return pl.pallas_call(
        functools.partial(_mm_kernel, ks=ks, h_starts=h_starts),
        grid=(n_cols // tn, n_rows // tm),
        in_specs=in_specs,
        out_specs=pl.BlockSpec((tm, tn), lambda j, i: (i, j)),
        out_shape=jax.ShapeDtypeStruct((n_rows, n_cols), out_dtype),
        compiler_params=_cparams(("arbitrary", "arbitrary")),
        name=name,
    )(*args)


def _norm_mm_kernel(a_ref, g_ref, w_ref, o_ref):
    y = _rms(a_ref[...].astype(F32)) * g_ref[...]
    o_ref[...] = jnp.dot(y.astype(BF16), w_ref[...], preferred_element_type=F32).astype(o_ref.dtype)


def _norm_matmul(a, a_col0, gain, w, n_rows, name):
    kw, n = w.shape
    tm = 512
    cb = a_col0 // kw
    return pl.pallas_call(
        _norm_mm_kernel,
        grid=(n_rows // tm,),
        in_specs=[
            pl.BlockSpec((tm, kw), lambda i: (i, cb)),
            pl.BlockSpec((1, kw), lambda i: (0, 0)),
            pl.BlockSpec((kw, n), lambda i: (0, 0)),
        ],
        out_specs=pl.BlockSpec((tm, n), lambda i: (i, 0)),
        out_shape=jax.ShapeDtypeStruct((n_rows, n), BF16),
        compiler_params=_cparams(("arbitrary",)),
        name=name,
    )(a, gain.reshape(1, kw).astype(F32), w)


def _mla_attn_kernel(qn_ref, qp_ref, kvl_ref, kvc_ref, kpl_ref, kpc_ref, cq_ref, sq_ref, ck_ref, sk_ref,
                     o_ref, kf_ref, vf_ref, *, seq, ctx, n_lat_blocks):
    qi = pl.program_id(2)

    @pl.when(qi == 0)
    def _build_keys():
        kpe = _rope_pairs(kpl_ref[...].astype(F32), ck_ref[...], sk_ref[...]).astype(BF16)
        kpc = kpc_ref[...]
        ones = _ones_column((seq + ctx, LANES))
        for hh in range(MLA_STEP_HEADS):
            c = 2 * LANES * hh
            kf_ref[hh, 0:seq, 0:LANES] = kvl_ref[:, c:c + LANES]
            kf_ref[hh, seq:seq + ctx, 0:LANES] = kvc_ref[:, c:c + LANES]
            vf_ref[hh, 0:seq, 0:LANES] = kvl_ref[:, c + LANES:c + 2 * LANES]
            vf_ref[hh, seq:seq + ctx, 0:LANES] = kvc_ref[:, c + LANES:c + 2 * LANES]
            vf_ref[hh, :, LANES:2 * LANES] = ones
            kf_ref[hh, 0:seq, LANES:2 * LANES] = kpe
            kf_ref[hh, seq:seq + ctx, LANES:2 * LANES] = kpc

    def attend(chains):
        def scores(hh, q, k, v):
            return lax.dot_general(q, k, (((1,), (1,)), ((), ())), preferred_element_type=F32)

        def finish(hh, r, s, v):
            e = jnp.exp2(s - jnp.max(s, -1, keepdims=True)).astype(BF16)
            ov = jnp.dot(e, v, preferred_element_type=F32)
            out = (ov[:, :LANES] / ov[:, LANES:LANES + 1]).astype(o_ref.dtype)
            o_ref[r:r + ATTN_CHUNK, hh * LANES:(hh + 1) * LANES] = out

        units = [(hh, r, q[r:r + ATTN_CHUNK], k, v) for hh, q, k, v in chains
                 for r in range(0, q.shape[0], ATTN_CHUNK)]
        pending = []
        for hh, r, q, k, v in units:
            pending.append((hh, r, scores(hh, q, k, v), v))
            if len(pending) > ATTN_AHEAD:
                finish(*pending.pop(0))
        for p in pending:
            finish(*p)

    @pl.when(qi < n_lat_blocks)
    def _latent_queries():
        chains = []
        for hh in range(MLA_STEP_HEADS):
            cols = slice(hh * LANES, (hh + 1) * LANES)
            qpe = _rope_pairs(qp_ref[:, cols].astype(F32), cq_ref[...], sq_ref[...]).astype(BF16)
            chains.append((hh, jnp.concatenate([qn_ref[:, cols], qpe], -1), kf_ref[hh], vf_ref[hh]))
        attend(chains)

    @pl.when(qi >= n_lat_blocks)
    def _context_queries():
        chains = []
        for hh in range(MLA_STEP_HEADS):
            cols = slice(hh * LANES, (hh + 1) * LANES)
            q = jnp.concatenate([qn_ref[:, cols], qp_ref[:, cols]], -1)
            chains.append((hh, q, kf_ref[hh, seq:seq + ctx, :], vf_ref[hh, seq:seq + ctx, :]))
        attend(chains)


def _mla_attention(qup, kvup, hl, cos_t, sin_t, rows, kpe_col):
    batch, seq, ctx = rows.batch, rows.seq, rows.ctx
    tq = 256
    nl, nc = seq // tq, ctx // tq
    lat_blocks = rows.lat // tq

    def qrow(b, qi):
        return jnp.where(qi < nl, b * nl + qi, lat_blocks + b * nc + (qi - nl))

    def qtab(b, h, qi):
        return (jnp.minimum(qi, nl - 1), 0)

    kb = kpe_col // LANES
    ctx0 = rows.lat // ctx
    hw = MLA_STEP_HEADS * LANES
    groups = MLA_HEADS // MLA_STEP_HEADS
    kv_scratch = pltpu.VMEM((MLA_STEP_HEADS, seq + ctx, 2 * LANES), BF16)
    return pl.pallas_call(
        functools.partial(_mla_attn_kernel, seq=seq, ctx=ctx, n_lat_blocks=nl),
        grid=(batch, groups, nl + nc),
        in_specs=[
            pl.BlockSpec((tq, hw), lambda b, h, qi: (qrow(b, qi), h)),
            pl.BlockSpec((tq, hw), lambda b, h, qi: (qrow(b, qi), groups + h)),
            pl.BlockSpec((seq, 2 * hw), lambda b, h, qi: (b, h)),
            pl.BlockSpec((ctx, 2 * hw), lambda b, h, qi: (ctx0 + b, h)),
            pl.BlockSpec((seq, LANES), lambda b, h, qi: (b, kb)),
            pl.BlockSpec((ctx, LANES), lambda b, h, qi: (ctx0 + b, kb)),
            pl.BlockSpec((tq, LANES), qtab),
            pl.BlockSpec((tq, LANES), qtab),
            pl.BlockSpec((seq, LANES), lambda b, h, qi: (0, 0)),
            pl.BlockSpec((seq, LANES), lambda b, h, qi: (0, 0)),
        ],
        out_specs=pl.BlockSpec((tq, hw), lambda b, h, qi: (qrow(b, qi), h)),
        out_shape=jax.ShapeDtypeStruct((rows.all, MLA_OUT), BF16),
        scratch_shapes=[kv_scratch, kv_scratch],
        compiler_params=_cparams(("arbitrary", "arbitrary", "arbitrary")),
        name="mla_attn",
    )(qup, qup, kvup, kvup, hl, hl, cos_t, sin_t, cos_t, sin_t)


def _conv3(p, w):
    n = p.shape[0]
    row = lax.broadcasted_iota(I32, p.shape, 0)
    prev = jnp.where(row == 0, 0.0, pltpu.roll(p, 1, 0))
    nxt = jnp.where(row == n - 1, 0.0, pltpu.roll(p, n - 1, 0))
    return prev * w[0:1] + p * w[1:2] + nxt * w[2:3]


HALO = 16


def _sconv_kernel(gb_ref, gc_ref, hh_ref, gcp_ref, hhp_ref, gcn_ref, hhn_ref, w_ref, o_ref, *, seq, ctx, lat):
    i = pl.program_id(0)
    tr = gc_ref.shape[0]
    r0 = i * tr
    length = jnp.where(r0 < lat, seq, ctx)
    pos = jnp.where(r0 < lat, r0, r0 - lat)
    starts = lax.rem(pos, length) == 0
    ends = lax.rem(pos + tr, length) == 0
    p = gc_ref[...].astype(F32) * hh_ref[...].astype(F32)
    before = gcp_ref[HALO - 1:HALO, :].astype(F32) * hhp_ref[HALO - 1:HALO, :].astype(F32)
    after = gcn_ref[0:1, :].astype(F32) * hhn_ref[0:1, :].astype(F32)
    before = jnp.where(starts, 0.0, before)
    after = jnp.where(ends, 0.0, after)
    row = lax.broadcasted_iota(I32, p.shape, 0)
    prev = jnp.where(row == 0, before, pltpu.roll(p, 1, 0))
    nxt = jnp.where(row == tr - 1, after, pltpu.roll(p, tr - 1, 0))
    w = w_ref[...]
    conv = prev * w[0:1] + p * w[1:2] + nxt * w[2:3]
    o_ref[...] = (gb_ref[...].astype(F32) * conv).astype(o_ref.dtype)


def _short_conv(hl, col0, width, conv_w, rows):
    tr, tc = 256, 1024
    assert rows.seq % tr == 0 and rows.ctx % tr == 0
    nct = width // tc
    c0 = col0 // tc
    per = tr // HALO
    last = rows.all // HALO - 1

    def main(k):
        return pl.BlockSpec((tr, tc), lambda i, j: (i, c0 + k * nct + j))

    def before(k):
        return pl.BlockSpec((HALO, tc), lambda i, j: (jnp.maximum(i * per - 1, 0), c0 + k * nct + j))

    def after(k):
        return pl.BlockSpec((HALO, tc), lambda i, j: (jnp.minimum((i + 1) * per, last), c0 + k * nct + j))

    return pl.pallas_call(
        functools.partial(_sconv_kernel, seq=rows.seq, ctx=rows.ctx, lat=rows.lat),
        grid=(rows.all // tr, nct),
        in_specs=[main(0), main(1), main(2), before(1), before(2), after(1), after(2),
                  pl.BlockSpec((3, tc), lambda i, j: (0, j))],
        out_specs=pl.BlockSpec((tr, tc), lambda i, j: (i, j)),
        out_shape=jax.ShapeDtypeStruct((rows.all, width), BF16),
        compiler_params=_cparams(("arbitrary", "arbitrary")),
        name="short_conv",
    )(hl, hl, hl, hl, hl, hl, hl, conv_w)


def _dwconv_kernel(x_ref, w_ref, o_ref):
    o_ref[...] = _conv3(x_ref[...].astype(F32), w_ref[...]).astype(o_ref.dtype)


def _hyena_conv3(hl, col0, width, conv_w, rows):
    tc = 512
    c0 = col0 // tc
    return pl.pallas_call(
        _dwconv_kernel,
        grid=(rows.batch, width // tc),
        in_specs=[
            pl.BlockSpec((rows.seq, tc), lambda s, j: (s, c0 + j)),
            pl.BlockSpec((3, tc), lambda s, j: (0, j)),
        ],
        out_specs=pl.BlockSpec((rows.seq, tc), lambda s, j: (s, j)),
        out_shape=jax.ShapeDtypeStruct((rows.lat, width), BF16),
        compiler_params=_cparams(("arbitrary", "arbitrary")),
        name="hyena_conv3",
    )(hl, conv_w)


def _diff_attn_kernel(q_ref, kl_ref, kc_ref, vl_ref, vc_ref, cq_ref, sq_ref, ck_ref, sk_ref, lam_ref, sub_ref,
                      o_ref, kf_ref, vf_ref, *, seq, ctx, lam_init):
    qi = pl.program_id(2)

    @pl.when(qi == 0)
    def _build_keys():
        ones = _ones_column((seq + ctx, LANES))
        for hh in range(ATTN_HEADS):
            cols = slice(hh * LANES, (hh + 1) * LANES)
            kf_ref[hh, 0:seq, :] = _rope_pairs(kl_ref[:, cols].astype(F32), ck_ref[...], sk_ref[...]).astype(BF16)
            kf_ref[hh, seq:seq + ctx, :] = kc_ref[:, cols]
            vf_ref[hh, 0:seq, 0:LANES] = vl_ref[:, cols]
            vf_ref[hh, seq:seq + ctx, 0:LANES] = vc_ref[:, cols]
            vf_ref[hh, :, LANES:2 * LANES] = ones

    lp = lam_ref[...]
    lam = (jnp.exp(jnp.sum(lp[0:1] * lp[1:2], -1, keepdims=True))
           - jnp.exp(jnp.sum(lp[2:3] * lp[3:4], -1, keepdims=True)) + lam_init)
    def scores(hh, r):
        cols = slice(hh * LANES, (hh + 1) * LANES)
        q = _rope_pairs(q_ref[r:r + ATTN_CHUNK, cols].astype(F32), cq_ref[r:r + ATTN_CHUNK, :],
                        sq_ref[r:r + ATTN_CHUNK, :])
        first = lax.broadcasted_iota(I32, q.shape, 1) < DIFF_HEAD_DIM
        qq = jnp.concatenate([jnp.where(first, q, 0.0), jnp.where(first, 0.0, q)], 0).astype(BF16)
        return lax.dot_general(qq, kf_ref[hh], (((1,), (1,)), ((), ())), preferred_element_type=F32)

    def finish(hh, r, s):
        e = jnp.exp2(s - jnp.max(s, -1, keepdims=True)).astype(BF16)
        ov = jnp.dot(e, vf_ref[hh], preferred_element_type=F32)
        on = ov[:, :LANES] / ov[:, LANES:LANES + 1]
        o = on[:ATTN_CHUNK] - lam * on[ATTN_CHUNK:]
        out = (_rms(o) * sub_ref[...] * (1.0 - lam_init)).astype(o_ref.dtype)
        o_ref[r:r + ATTN_CHUNK, hh * LANES:(hh + 1) * LANES] = out

    pending = []
    for hh in range(ATTN_HEADS):
        for r in range(0, q_ref.shape[0], ATTN_CHUNK):
            pending.append((hh, r, scores(hh, r)))
            if len(pending) > ATTN_AHEAD:
                finish(*pending.pop(0))
    for p in pending:
        finish(*p)


def _diff_attention(hl, hckv, cos_t, sin_t, lam_p, subln, rows, lam_init):
    batch, seq, ctx = rows.batch, rows.seq, rows.ctx
    tq = 512
    nl = seq // tq
    hw = ATTN_HEADS * LANES
    kc0 = DIFF_QK // hw
    vc0 = 2 * DIFF_QK // hw
    lam_pad = jnp.pad(lam_p.astype(F32), ((0, 0), (0, LANES - lam_p.shape[1])))
    return pl.pallas_call(
        functools.partial(_diff_attn_kernel, seq=seq, ctx=ctx, lam_init=lam_init),
        grid=(batch, DIFF_HEADS // ATTN_HEADS, nl),
        in_specs=[
            pl.BlockSpec((tq, hw), lambda b, h, qi: (b * nl + qi, h)),
            pl.BlockSpec((seq, hw), lambda b, h, qi: (b, kc0 + h)),
            pl.BlockSpec((ctx, hw), lambda b, h, qi: (b, h)),
            pl.BlockSpec((seq, hw), lambda b, h, qi: (b, vc0 + h)),
            pl.BlockSpec((ctx, hw), lambda b, h, qi: (b, kc0 + h)),
            pl.BlockSpec((tq, LANES), lambda b, h, qi: (qi, 0)),
            pl.BlockSpec((tq, LANES), lambda b, h, qi: (qi, 0)),
            pl.BlockSpec((seq, LANES), lambda b, h, qi: (0, 0)),
            pl.BlockSpec((seq, LANES), lambda b, h, qi: (0, 0)),
            pl.BlockSpec((4, LANES), lambda b, h, qi: (0, 0)),
            pl.BlockSpec((1, LANES), lambda b, h, qi: (0, 0)),
        ],
        out_specs=pl.BlockSpec((tq, hw), lambda b, h, qi: (b * nl + qi, h)),
        out_shape=jax.ShapeDtypeStruct((rows.lat, DIFF_V), BF16),
        scratch_shapes=[pltpu.VMEM((ATTN_HEADS, seq + ctx, LANES), BF16),
                        pltpu.VMEM((ATTN_HEADS, seq + ctx, 2 * LANES), BF16)],
        compiler_params=_cparams(("arbitrary", "arbitrary", "arbitrary")),
        name="diff_attn",
    )(hl, hl, hckv, hl, hckv, cos_t, sin_t, cos_t, sin_t, lam_pad, subln.reshape(1, LANES).astype(F32))


def _hid_kernel(z_ref, w1_ref, b1_ref, w2_ref, b2_ref, o_ref):
    h1 = jnp.sin(_dot3(z_ref[...], w1_ref[...]) + b1_ref[...])
    o_ref[...] = jnp.sin(_dot3(h1, w2_ref[...]) + b2_ref[...])


def _filt_kernel(hid_ref, w3f_ref, w3b_ref, dec_ref, o_ref, *, n):
    hf = hid_ref[0:n, :]
    hr = hid_ref[n:2 * n, :]
    w3f, w3b = w3f_ref[...], w3b_ref[...]
    kf = _dot3(hf, w3f)
    kb0 = _dot3(hf[0:8], w3b)[0:1]
    kbr = _dot3(hr, w3b)
    dec = dec_ref[...]
    row = lax.broadcasted_iota(I32, kf.shape, 0)
    inv = 1.0 / (n - 1)
    win_f = jnp.exp(-(row.astype(F32) * inv) * dec) + HY_SHIFT
    win_r = jnp.exp(-((n - row).astype(F32) * inv) * dec) + HY_SHIFT
    head = kf * win_f + jnp.where(row == 0, kb0 * (1.0 + HY_SHIFT), 0.0)
    tail = jnp.where(row == 0, 0.0, kbr * win_r)
    nrm = jnp.sum(jnp.abs(head), 0, keepdims=True) + jnp.sum(jnp.abs(tail), 0, keepdims=True)
    o_ref[0:n, :] = (head / nrm).astype(o_ref.dtype)
    o_ref[n:2 * n, :] = (tail / nrm).astype(o_ref.dtype)


def _hyena_filters(n, width, w1, b1, w2, b2, w3):
    t = jnp.linspace(0.0, 1.0, n, dtype=F32)[:, None]
    ang = (2.0 * math.pi / n) * jnp.arange(n, dtype=F32)[:, None] * jnp.linspace(1e-4, HY_BANDS - 1, HY_BANDS, dtype=F32)[None, :]
    z = jnp.concatenate([t, jnp.cos(ang), -jnp.sin(ang)], -1)
    z_rev = jnp.concatenate([z[:1], z[:0:-1]], 0)
    z2 = jnp.pad(jnp.concatenate([z, z_rev], 0), ((0, 0), (0, LANES - HY_EMB)))
    pad = LANES - HY_FFN
    w1p = jnp.pad(w1.astype(F32), ((0, LANES - HY_EMB), (0, pad)))
    b1p = jnp.pad(b1.astype(F32), (0, pad)).reshape(1, LANES)
    w2p = jnp.pad(w2.astype(F32), ((0, pad), (0, pad)))
    b2p = jnp.pad(b2.astype(F32), (0, pad)).reshape(1, LANES)
    w3p = jnp.pad(w3.astype(F32), ((0, pad), (0, 0)))
    th = min(1024, 2 * n)
    hid = pl.pallas_call(
        _hid_kernel,
        grid=(2 * n // th,),
        in_specs=[
            pl.BlockSpec((th, LANES), lambda i: (i, 0)),
            pl.BlockSpec((LANES, LANES), lambda i: (0, 0)),
            pl.BlockSpec((1, LANES), lambda i: (0, 0)),
            pl.BlockSpec((LANES, LANES), lambda i: (0, 0)),
            pl.BlockSpec((1, LANES), lambda i: (0, 0)),
        ],
        out_specs=pl.BlockSpec((th, LANES), lambda i: (i, 0)),
        out_shape=jax.ShapeDtypeStruct((2 * n, LANES), F32),
        compiler_params=_cparams(("arbitrary",)),
        name="hyena_hidden",
    )(z2, w1p, b1p, w2p, b2p)
    decay = jnp.abs(jnp.linspace(HY_MIN_DECAY, HY_MAX_DECAY, width, dtype=F32)).reshape(1, width)
    tn = 256
    per = width // tn
    return pl.pallas_call(
        functools.partial(_filt_kernel, n=n),
        grid=(HY_ORDER, per),
        in_specs=[
            pl.BlockSpec((2 * n, LANES), lambda o, j: (0, 0)),
            pl.BlockSpec((LANES, tn), lambda o, j: (0, (2 * o) * per + j)),
            pl.BlockSpec((LANES, tn), lambda o, j: (0, (2 * o + 1) * per + j)),
            pl.BlockSpec((1, tn), lambda o, j: (0, j)),
        ],
        out_specs=pl.BlockSpec((None, 2 * n, tn), lambda o, j: (o, 0, j)),
        out_shape=jax.ShapeDtypeStruct((HY_ORDER, 2 * n, width), BF16),
        compiler_params=_cparams(("arbitrary", "arbitrary")),
        name="hyena_filters",
    )(hid, w3p, w3p, decay)


def _dft_tables(n):
    p = 2 * n
    lo = 64
    k = jnp.arange(n, dtype=I32)[:, None]
    t1 = jnp.arange(p // lo, dtype=I32)[None, :]
    t2 = jnp.arange(lo, dtype=I32)[None, :]
    ang_a = ((k * t1 * lo) % p).astype(F32) * (2.0 * math.pi / p)
    ang_b = ((k * t2) % p).astype(F32) * (2.0 * math.pi / p)
    ca, sa = jnp.cos(ang_a)[:, :, None], jnp.sin(ang_a)[:, :, None]
    cb, sb = jnp.cos(ang_b)[:, None, :], jnp.sin(ang_b)[:, None, :]
    cos_kt = (ca * cb - sa * sb).reshape(n, p)
    sin_kt = (sa * cb + ca * sb).reshape(n, p)
    t = jnp.arange(p, dtype=I32)[None, :]
    alt = jnp.where(t % 2 == 0, 1.0, -1.0)
    fwd = jnp.stack([cos_kt, jnp.where(k == 0, alt, -sin_kt)]).astype(BF16)
    kk = t[:, :n]
    gx = jnp.where(kk == 0, 1.0, 2.0 * cos_kt[:, :n]) / p
    gy = jnp.where(kk == 0, jnp.where(k % 2 == 0, 1.0, -1.0), -2.0 * sin_kt[:, :n]) / p
    inv = jnp.stack([gx, gy]).astype(BF16)
    return fwd, inv


def _spec_kernel(fx_ref, fy_ref, f_ref, o_ref):
    f = f_ref[...]
    o_ref[0] = jnp.dot(fx_ref[...], f, preferred_element_type=F32)
    o_ref[1] = jnp.dot(fy_ref[...], f, preferred_element_type=F32)


def _filter_spectrum(fwd, filt):
    _, n, p = fwd.shape
    width = filt.shape[2]
    tm, tn = min(512, n), 512
    return pl.pallas_call(
        _spec_kernel,
        grid=(HY_ORDER, width // tn, n // tm),
        in_specs=[
            pl.BlockSpec((None, tm, p), lambda o, j, i: (0, i, 0)),
            pl.BlockSpec((None, tm, p), lambda o, j, i: (1, i, 0)),
            pl.BlockSpec((None, p, tn), lambda o, j, i: (o, 0, j)),
        ],
        out_specs=pl.BlockSpec((None, 2, tm, tn), lambda o, j, i: (o, 0, i, j)),
        out_shape=jax.ShapeDtypeStruct((HY_ORDER, 2, n, width), F32),
        compiler_params=_cparams(("arbitrary", "arbitrary", "arbitrary")),
        name="filter_spectrum",
    )(fwd, fwd, filt)


def _dft_fwd_kernel(fx_ref, fy_ref, z_ref, hx_ref, hy_ref, o_ref):
    i = pl.program_id(2)
    z = z_ref[...]
    ux = jnp.dot(fx_ref[...], z, preferred_element_type=F32)
    uy = jnp.dot(fy_ref[...], z, preferred_element_type=F32)
    hx, hy = hx_ref[...], hy_ref[...]
    row = lax.broadcasted_iota(I32, ux.shape, 0) + i * ux.shape[0]
    dc = row == 0
    o_ref[0] = (ux * hx - jnp.where(dc, 0.0, uy * hy)).astype(o_ref.dtype)
    o_ref[1] = jnp.where(dc, uy * hy, ux * hy + uy * hx).astype(o_ref.dtype)


def _dft_forward(fwd, z, z_col0, spec, order, rows):
    n = rows.seq
    width = spec.shape[3]
    tm, tn = min(1024, n), 512
    c0 = z_col0 // tn
    return pl.pallas_call(
        _dft_fwd_kernel,
        grid=(rows.batch, width // tn, n // tm),
        in_specs=[
            pl.BlockSpec((None, tm, n), lambda b, j, i: (0, i, 0)),
            pl.BlockSpec((None, tm, n), lambda b, j, i: (1, i, 0)),
            pl.BlockSpec((n, tn), lambda b, j, i: (b, c0 + j)),
            pl.BlockSpec((None, None, tm, tn), lambda b, j, i: (order, 0, i, j)),
            pl.BlockSpec((None, None, tm, tn), lambda b, j, i: (order, 1, i, j)),
        ],
        out_specs=pl.BlockSpec((None, 2, tm, tn), lambda b, j, i: (b, 0, i, j)),
        out_shape=jax.ShapeDtypeStruct((rows.batch, 2, n, width), BF16),
        compiler_params=_cparams(("arbitrary", "arbitrary", "arbitrary")),
        name="hyena_dft_fwd",
    )(fwd, fwd, z, spec, spec)


def _dft_inv_kernel(gx_ref, gy_ref, s_ref, gate_ref, z_ref, skip_ref, o_ref):
    y = (jnp.dot(gx_ref[...], s_ref[0], preferred_element_type=F32)
         + jnp.dot(gy_ref[...], s_ref[1], preferred_element_type=F32))
    z = z_ref[...].astype(F32)
    o_ref[...] = (gate_ref[...].astype(F32) * (y + z * skip_ref[...])).astype(o_ref.dtype)


def _dft_inverse(inv, spec_z, gates, gate_col0, z, z_col0, skip, order, rows):
    n = rows.seq
    width = spec_z.shape[3]
    tm, tn = min(1024, n), 512
    nb = n // tm
    g0, z0 = gate_col0 // tn, z_col0 // tn
    return pl.pallas_call(
        _dft_inv_kernel,
        grid=(rows.batch, width // tn, nb),
        in_specs=[
            pl.BlockSpec((None, tm, n), lambda b, j, i: (0, i, 0)),
            pl.BlockSpec((None, tm, n), lambda b, j, i: (1, i, 0)),
            pl.BlockSpec((None, 2, n, tn), lambda b, j, i: (b, 0, 0, j)),
            pl.BlockSpec((tm, tn), lambda b, j, i: (b * nb + i, g0 + j)),
            pl.BlockSpec((tm, tn), lambda b, j, i: (b * nb + i, z0 + j)),
            pl.BlockSpec((None, 1, tn), lambda b, j, i: (order, 0, j)),
        ],
        out_specs=pl.BlockSpec((tm, tn), lambda b, j, i: (b * nb + i, j)),
        out_shape=jax.ShapeDtypeStruct((rows.lat, width), BF16),
        compiler_params=_cparams(("arbitrary", "arbitrary", "arbitrary")),
        name="hyena_dft_inv",
    )(inv, inv, spec_z, gates, z, skip.reshape(HY_ORDER, 1, width).astype(F32))


def _router_kernel(x_ref, sh_ref, sc_ref, rw_ref, rb_ref, u_ref, ti_ref, tg_ref, rk_ref, cnt_ref, carry_ref,
                   w_ref):
    i = pl.program_id(0)

    @pl.when(i == 0)
    def _init():
        carry_ref[...] = jnp.zeros_like(carry_ref)
        w_ref[0], w_ref[1] = _split_bf16(rw_ref[...])

    u = _rms(x_ref[...]) * (1.0 + sc_ref[...]) + sh_ref[...]
    u_ref[...] = _pack_halves(u)
    uh, ul = _split_bf16(u)
    dot = functools.partial(jnp.dot, preferred_element_type=F32)
    logits = dot(uh, w_ref[0]) + dot(ul, w_ref[0]) + dot(uh, w_ref[1]) + rb_ref[...]
    tm = logits.shape[0]
    lane = lax.broadcasted_iota(I32, logits.shape, 1).astype(F32)
    neg = jnp.float32(-jnp.inf)
    work = jnp.where(lane < N_EXPERTS, logits, neg)
    vals, idxs = [], []
    for _ in range(TOP_K):
        m = jnp.max(work, -1, keepdims=True)
        idx = jnp.min(jnp.where(work == m, lane, float(LANES)), -1, keepdims=True)
        vals.append(m)
        idxs.append(idx)
        work = jnp.where(lane == idx, neg, work)
    es = [jnp.exp(v - vals[0]) for v in vals]
    den = es[0] + es[1] + es[2] + es[3]
    onehot = jnp.zeros(logits.shape, F32)
    for idx in idxs:
        onehot = onehot + (lane == idx).astype(F32)
    r = lax.broadcasted_iota(I32, (tm, tm), 0)
    c = lax.broadcasted_iota(I32, (tm, tm), 1)
    before = (c < r).astype(BF16)
    carry = carry_ref[0:1, :]
    prefix = jnp.dot(before, onehot.astype(BF16), preferred_element_type=F32) + carry
    ti = jnp.zeros(logits.shape, F32)
    tg = jnp.zeros(logits.shape, F32)
    rk = jnp.zeros(logits.shape, F32)
    for k in range(TOP_K):
        pos = jnp.sum(jnp.where(lane == idxs[k], prefix, 0.0), -1, keepdims=True)
        ti = jnp.where(lane == k, idxs[k], ti)
        tg = jnp.where(lane == k, es[k] / den, tg)
        rk = jnp.where(lane == k, pos, rk)
    ti_ref[...] = ti
    tg_ref[...] = tg
    rk_ref[...] = rk
    total = carry + jnp.sum(onehot, 0, keepdims=True)
    carry_ref[...] = jnp.broadcast_to(total, carry_ref.shape)
    cnt_ref[...] = jnp.broadcast_to(total, cnt_ref.shape)


def _router(h, mod4, rows, layer, n_rows, router_w, router_b):
    d = h.shape[1]
    tm = 256
    rw = jnp.pad(router_w.astype(F32), ((0, 0), (0, LANES - N_EXPERTS)))
    rb = jnp.pad(router_b.astype(F32), (0, LANES - N_EXPERTS)).reshape(1, LANES)
    small = lambda dt: jax.ShapeDtypeStruct((n_rows, LANES), dt)
    return pl.pallas_call(
        _router_kernel,
        grid=(n_rows // tm,),
        in_specs=[
            pl.BlockSpec((tm, d), lambda i: (i, 0)),
            _mod_spec(rows, tm, d, layer, 3, d, 1, 0),
            _mod_spec(rows, tm, d, layer, 4, d, 1, 0),
            pl.BlockSpec((d, LANES), lambda i: (0, 0)),
            pl.BlockSpec((1, LANES), lambda i: (0, 0)),
        ],
        out_specs=[
            pl.BlockSpec((tm, d // 2), lambda i: (i, 0)),
            pl.BlockSpec((tm, LANES), lambda i: (i, 0)),
            pl.BlockSpec((tm, LANES), lambda i: (i, 0)),
            pl.BlockSpec((tm, LANES), lambda i: (i, 0)),
            pl.BlockSpec((8, LANES), lambda i: (0, 0)),
        ],
        out_shape=[jax.ShapeDtypeStruct((n_rows, d // 2), U32), small(F32), small(F32), small(F32),
                   jax.ShapeDtypeStruct((8, LANES), F32)],
        scratch_shapes=[pltpu.VMEM((8, LANES), F32), pltpu.VMEM((2, d, LANES), BF16)],
        compiler_params=_cparams(("arbitrary",)),
        name="router",
    )(h, mod4, mod4, rw, rb)


def _row_copy(src_ref, src_row, dst_ref, dst_row, sem):
    return pltpu.make_async_copy(src_ref.at[pl.ds(src_row, 1)], dst_ref.at[pl.ds(dst_row, 1)], sem)


def _block_copy(src_ref, dst_ref, sem):
    return pltpu.make_async_copy(src_ref.at[pl.ds(0, dst_ref.shape[0])], dst_ref, sem)


def _expert_kernel(be_ref, used_ref, tok_ref, tok_next_ref, src_ref, wgu_ref, bgu_ref, wd_ref, bd_ref, o_ref,
                   xbuf, sems):
    i = pl.program_id(0)
    slot = lax.rem(i, 2)
    n_used = used_ref[0]

    def gather(ids_ref, s):
        for r in range(MOE_BLK):
            _row_copy(src_ref, ids_ref[0, 0, r], xbuf.at[s], r, sems.at[s]).start()

    @pl.when(i == 0)
    def _first():
        gather(tok_ref, 0)

    @pl.when(i + 1 < n_used)
    def _ahead():
        gather(tok_next_ref, 1 - slot)

    @pl.when(i < n_used)
    def _compute():
        _block_copy(src_ref, xbuf.at[slot], sems.at[slot]).wait()
        dot = functools.partial(jnp.dot, preferred_element_type=F32)
        def gate_up(r):
            lo, hi = _unpack_halves(xbuf[slot, r:r + EXPERT_CHUNK, :])
            half = lo.shape[1]
            return (dot(lo.astype(BF16), wgu_ref[0:half, :]) + dot(hi.astype(BF16), wgu_ref[half:2 * half, :])
                    + bgu_ref[...])

        def down(r, gu):
            de = gu.shape[1] // 2
            g = jnp.minimum(gu[:, :de], SWIGLU_LIMIT)
            up = jnp.clip(gu[:, de:], -SWIGLU_LIMIT, SWIGLU_LIMIT)
            a = (up + 1.0) * g * jax.nn.sigmoid(SWIGLU_ALPHA * g)
            y = dot(a.astype(BF16), wd_ref[...]) + bd_ref[...]
            o_ref[r:r + EXPERT_CHUNK, :] = _pack_halves(y)

        pending = None
        for r in range(0, MOE_BLK, EXPERT_CHUNK):
            gu = gate_up(r)
            if pending is not None:
                down(*pending)
            pending = (r, gu)
        down(*pending)

    @pl.when(i >= n_used)
    def _empty():
        o_ref[...] = jnp.zeros(o_ref.shape, o_ref.dtype)


def _experts(u_packed, slot_tok, block_e, n_used, layer, w_gu, b_gu, w_down, b_down):
    _, n_exp, d, de2 = w_gu.shape
    de = de2 // 2
    n_blocks = block_e.shape[0]
    tok = slot_tok.reshape(n_blocks, 1, MOE_BLK)
    grid_spec = pltpu.PrefetchScalarGridSpec(
        num_scalar_prefetch=2,
        grid=(n_blocks,),
        in_specs=[
            pl.BlockSpec((1, 1, MOE_BLK), lambda i, be, nu: (i, 0, 0), memory_space=pltpu.SMEM),
            pl.BlockSpec((1, 1, MOE_BLK), lambda i, be, nu: (jnp.minimum(i + 1, n_blocks - 1), 0, 0),
                         memory_space=pltpu.SMEM),
            pl.BlockSpec(memory_space=pl.ANY),
            pl.BlockSpec((None, None, d, de2), lambda i, be, nu: (layer, be[i], 0, 0)),
            pl.BlockSpec((None, None, 1, de2), lambda i, be, nu: (layer, be[i], 0, 0)),
            pl.BlockSpec((None, None, de, d), lambda i, be, nu: (layer, be[i], 0, 0)),
            pl.BlockSpec((None, None, 1, d), lambda i, be, nu: (layer, be[i], 0, 0)),
        ],
        out_specs=pl.BlockSpec((MOE_BLK, d // 2), lambda i, be, nu: (i, 0)),
        scratch_shapes=[pltpu.VMEM((2, MOE_BLK, d // 2), U32), pltpu.SemaphoreType.DMA((2,))],
    )
    return pl.pallas_call(
        _expert_kernel,
        grid_spec=grid_spec,
        out_shape=jax.ShapeDtypeStruct((n_blocks * MOE_BLK, d // 2), U32),
        compiler_params=_cparams(("arbitrary",)),
        name="moe_experts",
    )(block_e, n_used, tok, tok, u_packed, w_gu, b_gu.reshape(-1, n_exp, 1, de2).astype(F32), w_down,
      b_down.reshape(-1, n_exp, 1, d).astype(F32))


def _combine_kernel(dest_ref, dest_next_ref, gate_ref, h_ref, g2_ref, sh_ref, sc_ref, yb_ref, *rest, final):
    if final:
        o_ref, buf, sems = rest
    else:
        o_ref, u_ref, buf, sems = rest
    n = h_ref.shape[0]
    i = pl.program_id(0)
    slot = lax.rem(i, 2)

    def gather(ids_ref, s):
        for r in range(n):
            for k in range(TOP_K):
                _row_copy(yb_ref, ids_ref[0, 0, r * TOP_K + k], buf.at[s], k * n + r, sems.at[s]).start()

    @pl.when(i == 0)
    def _first():
        gather(dest_ref, 0)

    @pl.when(i + 1 < pl.num_programs(0))
    def _ahead():
        gather(dest_next_ref, 1 - slot)

    _block_copy(yb_ref, buf.at[slot], sems.at[slot]).wait()
    gates = gate_ref[...]
    acc_lo = None
    acc_hi = None
    for k in range(TOP_K):
        lo, hi = _unpack_halves(buf[slot, k * n:(k + 1) * n, :])
        gk = gates[:, k:k + 1]
        acc_lo = gk * lo if acc_lo is None else acc_lo + gk * lo
        acc_hi = gk * hi if acc_hi is None else acc_hi + gk * hi
    out = h_ref[...] + g2_ref[...] * jnp.concatenate([acc_lo, acc_hi], -1)
    if final:
        o_ref[...] = _rms(out) * sc_ref[...]
    else:
        o_ref[...] = out
        u_ref[...] = (_rms(out) * (1.0 + sc_ref[...]) + sh_ref[...]).astype(u_ref.dtype)


def _combine(dest, gates, h, mod4, rows, layer, n_rows, yb, final_gain, final):
    d = h.shape[1]
    tm = COMBINE_BLK
    nb = n_rows // tm
    dest3 = dest.reshape(nb, 1, tm * TOP_K)
    row_spec = pl.BlockSpec((tm, d), lambda i: (i, 0))
    if final:
        gain = final_gain.reshape(1, d).astype(F32)
        mod_specs = [pl.BlockSpec((1, d), lambda i: (0, 0))] * 2
        mod_args = [gain, gain]
        out_specs, out_shape = row_spec, jax.ShapeDtypeStruct((n_rows, d), F32)
    else:
        mod_specs = [_mod_spec(rows, tm, d, layer + 1, 0, d, 1, 0), _mod_spec(rows, tm, d, layer + 1, 1, d, 1, 0)]
        mod_args = [mod4, mod4]
        out_specs = [row_spec, row_spec]
        out_shape = [jax.ShapeDtypeStruct((n_rows, d), F32), jax.ShapeDtypeStruct((n_rows, d), BF16)]
    return pl.pallas_call(
        functools.partial(_combine_kernel, final=final),
        grid=(nb,),
        in_specs=[
            pl.BlockSpec((1, 1, tm * TOP_K), lambda i: (i, 0, 0), memory_space=pltpu.SMEM),
            pl.BlockSpec((1, 1, tm * TOP_K), lambda i: (jnp.minimum(i + 1, nb - 1), 0, 0), memory_space=pltpu.SMEM),
            pl.BlockSpec((tm, LANES), lambda i: (i, 0)),
            row_spec,
            _mod_spec(rows, tm, d, layer, 5, d, 1, 0),
            *mod_specs,
            pl.BlockSpec(memory_space=pl.ANY),
        ],
        out_specs=out_specs,
        out_shape=out_shape,
        scratch_shapes=[pltpu.VMEM((2, TOP_K * tm, d // 2), U32), pltpu.SemaphoreType.DMA((2,))],
        compiler_params=_cparams(("arbitrary",)),
        name="moe_combine",
    )(dest3, dest3, gates, h, mod4, *mod_args, yb)


def _moe(h, mod4, rows, layer, n_rows, router_w, router_b, w_gu, b_gu, w_down, b_down, final_gain, final):
    u_packed, ti, tg, rk, cnt = _router(h, mod4, rows, layer, n_rows, router_w, router_b)
    counts = cnt[0, :N_EXPERTS].astype(I32)
    padded = (counts + MOE_BLK - 1) // MOE_BLK * MOE_BLK
    pad_end = jnp.cumsum(padded)
    pad_start = pad_end - padded
    top_i = ti[:, :TOP_K].astype(I32)
    dest = pad_start[top_i] + rk[:, :TOP_K].astype(I32)
    n_blocks = n_rows * TOP_K // MOE_BLK + N_EXPERTS
    tok = jnp.repeat(jnp.arange(n_rows, dtype=I32), TOP_K)
    slot_tok = jnp.zeros((n_blocks * MOE_BLK,), I32).at[dest.reshape(-1)].set(
        tok, unique_indices=True, mode="promise_in_bounds")
    block_start = jnp.arange(n_blocks, dtype=I32)[:, None] * MOE_BLK
    block_e = jnp.minimum(jnp.sum((pad_end[None, :] <= block_start).astype(I32), 1), N_EXPERTS - 1)
    n_used = (pad_end[-1:] // MOE_BLK).astype(I32)
    yb = _experts(u_packed, slot_tok, block_e, n_used, layer, w_gu, b_gu, w_down, b_down)
    return _combine(dest, tg, h, mod4, rows, layer, n_rows, yb, final_gain, final)


def _rope_tables(seq):
    n_rows = seq // GRID_W
    row = jnp.repeat(jnp.arange(n_rows), GRID_W).astype(F32)
    col = jnp.tile(jnp.arange(GRID_W), n_rows).astype(F32)
    quarter = MLA_ROPE_DIM // 4
    inv = 1.0 / (ROPE_THETA ** (jnp.arange(quarter, dtype=F32) / quarter))
    ang = jnp.concatenate([row[:, None] * inv, col[:, None] * inv], -1)
    cos, sin = jnp.cos(ang), jnp.sin(ang)
    return jnp.tile(cos, (1, 4)), jnp.tile(jnp.concatenate([-sin, sin], -1), (1, 2))


def kernel(x, c, ctx, c_ctx, ada_w, ada_b, mla_in_w, mla_q_norm, mla_kv_norm, mla_w_uq, mla_w_ukv, sc_conv_w, even_out_w, odd_in_w, diff_lambda, diff_subln, hy_conv_w, hy_w1, hy_b1, hy_w2, hy_b2, hy_w3, hy_skip, odd_out_w, router_w, router_b, moe_w_gu, moe_b_gu, moe_w_down, moe_b_down, final_norm):
    batch, seq, d = x.shape
    n_ctx = ctx.shape[1]
    depth = ada_w.shape[0]
    assert depth == 2 and batch + 1 <= MOD_ROWS
    rows = _Rows(batch, seq, n_ctx)
    sc_width = d - MLA_OUT
    hy_width = d - DIFF_V

    cond = jnp.zeros((MOD_ROWS, d), F32).at[:batch].set(c).at[batch].set(c_ctx)
    mod4 = _adaln(cond, ada_w, ada_b)
    cos_t, sin_t = _rope_tables(seq)
    h0 = [x.reshape(rows.lat, d), ctx.reshape(batch * n_ctx, d)]

    kpe_col = MLA_Q_RANK + MLA_KV_RANK
    lat_cols = kpe_col + MLA_ROPE_DIM
    lat_pad = -lat_cols % LANES
    w_lat = jnp.pad(mla_in_w[0][:, :lat_cols], ((0, 0), (0, lat_pad))).astype(BF16)
    w_sc = mla_in_w[0][:, lat_cols:].astype(BF16)
    wq = (mla_w_uq[0] * (MLA_QK_DIM ** -0.5 * LOG2E)).reshape(MLA_Q_RANK, MLA_HEADS, MLA_QK_DIM)
    wq_pe = jnp.pad(wq[:, :, MLA_NOPE_DIM:], ((0, 0), (0, 0), (0, LANES - MLA_ROPE_DIM)))
    wq = jnp.concatenate([wq[:, :, :MLA_NOPE_DIM].reshape(MLA_Q_RANK, -1), wq_pe.reshape(MLA_Q_RANK, -1)], 1).astype(BF16)

    u = _modulate(h0, mod4, rows, 0, rows.all)
    hl = _matmul([u], w_lat, n_rows=rows.all, n_cols=w_lat.shape[1], tn=w_lat.shape[1], name="even_in_proj_latent")
    hs = _matmul([u], w_sc, n_rows=rows.all, n_cols=w_sc.shape[1], name="even_in_proj_conv")
    qup = _norm_matmul(hl, 0, mla_q_norm[0], wq, rows.all, "mla_q_up")
    kvup = _norm_matmul(hl, MLA_Q_RANK, mla_kv_norm[0], mla_w_ukv[0].astype(BF16), rows.all, "mla_kv_up")
    attn = _mla_attention(qup, kvup, hl, cos_t, sin_t, rows, kpe_col)
    sc = _short_conv(hs, 0, sc_width, sc_conv_w[0].astype(F32), rows)
    h = _matmul([attn, sc], even_out_w[0].astype(BF16), n_rows=rows.all, n_cols=d, out_dtype=F32,
                resid=(h0, mod4, rows, 0, 2), name="even_out_proj")
    w_gu_bf, w_down_bf = moe_w_gu.astype(BF16), moe_w_down.astype(BF16)
    h, u = _moe(h, mod4, rows, 0, rows.all, router_w[0], router_b[0], w_gu_bf, moe_b_gu, w_down_bf, moe_b_down,
                final_norm, False)

    lam_init = 0.8 - 0.6 * math.exp(-0.3 * 1)
    q_scale = jnp.where(jnp.arange(odd_in_w.shape[2]) < DIFF_QK, DIFF_HEAD_DIM ** -0.5 * LOG2E, 1.0).astype(F32)
    w_in1 = (odd_in_w[0] * q_scale[None, :]).astype(BF16)
    hl = _matmul([u], w_in1, n_rows=rows.lat, n_cols=w_in1.shape[1], name="odd_in_proj")
    hckv = _matmul([u], w_in1, n_rows=batch * n_ctx, n_cols=DIFF_QK + DIFF_V, a_row0=rows.lat, w_col0=DIFF_QK,
                   name="odd_ctx_kv_proj")
    attn = _diff_attention(hl, hckv, cos_t, sin_t, diff_lambda[0], diff_subln[0], rows, lam_init)

    hy_col = 2 * DIFF_QK + DIFF_V
    hc3 = _hyena_conv3(hl, hy_col, 3 * hy_width, hy_conv_w[0].astype(F32), rows)
    filt = _hyena_filters(seq, hy_width, hy_w1[0], hy_b1[0], hy_w2[0], hy_b2[0], hy_w3[0])
    fwd, inv = _dft_tables(seq)
    spec = _filter_spectrum(fwd, filt)
    z, z_col = hc3, 0
    for o in range(HY_ORDER):
        spec_z = _dft_forward(fwd, z, z_col, spec, o, rows)
        z = _dft_inverse(inv, spec_z, hc3, (1 + o) * hy_width, z, z_col, hy_skip[0], o, rows)
        z_col = 0

    h = _matmul([attn, z], odd_out_w[0].astype(BF16), n_rows=rows.lat, n_cols=d, out_dtype=F32,
                resid=([h], mod4, rows, 1, 2), name="odd_out_proj")
    out = _moe(h, mod4, rows, 1, rows.lat, router_w[1], router_b[1], w_gu_bf, moe_b_gu, w_down_bf, moe_b_down,
               final_norm, True)
    return out.reshape(batch, seq, d)
```
